```python
import jax, jax.numpy as jnp
from jax import lax
import numpy as np

D_MODEL = 1024
BATCH = 8
SEQ = 2048
DEPTH = 4
DEC_BATCH = 128
DEC_SEQ = 4
PAST_LEN = 16384
PAGE_SIZE = 128

D_CONV = D_MODEL
CONV_W = 3
N_HEADS = 4
DK = D_MODEL // (2 * N_HEADS)
DV = D_MODEL // N_HEADS
D_QK = N_HEADS * DK
D_V = N_HEADS * DV
D_FF = 4 * D_MODEL
CHUNK = 128
GATE_CAP = 15.0
EPS = 1e-6
IN_SIZES = (D_CONV, D_CONV, D_CONV, D_QK, D_QK, D_V, D_V, N_HEADS, N_HEADS, D_MODEL, D_MODEL)
N_IN = 3 * D_CONV + 2 * D_QK + 2 * D_V + 2 * N_HEADS + 2 * D_MODEL

kernel_name = "hybrid_conv_mlstm_adaln_decode_step"


def _rmsnorm(x, g):
    xf = x.astype(jnp.float32)
    y = xf * lax.rsqrt(jnp.mean(xf * xf, axis=-1, keepdims=True) + EPS)
    return (y * g.astype(jnp.float32)).astype(x.dtype)


def _split_in(z):
    idx = []
    acc = 0
    for s in IN_SIZES[:-1]:
        acc += s
        idx.append(acc)
    return jnp.split(z, idx, axis=-1)


def _softcap(a):
    return GATE_CAP * jnp.tanh(a / GATE_CAP)


def _mlstm_chunkwise(q, k, v, i_pre, logf, C0, n0, m0):
    B, H, T, _ = q.shape
    L = CHUNK if T % CHUNK == 0 else T
    nc = T // L

    def chunks(a):
        return jnp.moveaxis(a.reshape((B, H, nc, L) + a.shape[3:]), 2, 0)

    causal = jnp.tril(jnp.ones((L, L), dtype=bool))

    def step(carry, inp):
        C, n, m = carry
        qc, kc, vc, ic, fc = inp
        b = jnp.cumsum(fc, axis=-1)
        logD = jnp.where(causal, b[..., :, None] - b[..., None, :] + ic[..., None, :], -jnp.inf)
        a = b + m[..., None]
        m_row = jnp.maximum(jnp.max(logD, axis=-1), a)
        inter = jnp.exp(a - m_row)
        S = jnp.einsum('bhtd,bhsd->bhts', qc, kc) * jnp.exp(logD - m_row[..., None])
        num = jnp.einsum('bhts,bhsv->bhtv', S, vc) + inter[..., None] * jnp.einsum('bhtd,bhdv->bhtv', qc, C)
        den = jnp.sum(S, axis=-1) + inter * jnp.einsum('bhtd,bhd->bht', qc, n)
        h = num / jnp.maximum(jnp.abs(den), jnp.exp(-m_row))[..., None]
        bL = b[..., -1]
        w_log = bL[..., None] - b + ic
        m_new = jnp.maximum(bL + m, jnp.max(w_log, axis=-1))
        decay = jnp.exp(bL + m - m_new)
        ws = jnp.exp(w_log - m_new[..., None])
        C_new = decay[..., None, None] * C + jnp.einsum('bhs,bhsd,bhsv->bhdv', ws, kc, vc)
        n_new = decay[..., None] * n + jnp.einsum('bhs,bhsd->bhd', ws, kc)
        return (C_new, n_new, m_new), h

    (C1, n1, m1), hs = lax.scan(step, (C0, n0, m0), (chunks(q), chunks(k), chunks(v), chunks(i_pre), chunks(logf)))
    h = jnp.moveaxis(hs, 0, 2).reshape(B, H, T, v.shape[-1])
    return h, C1, n1, m1


def _mixer(h, conv_prev, C0, n0, m0, w_in, b_gate, conv_w, conv_b, hn_g, w_out):
    Bsz, T, _ = h.shape
    f32 = jnp.float32
    xc, bg, cg, q, k, v, o, ig, fg, ga, gb = _split_in(h @ w_in)
    u = cg * xc
    pad = jnp.concatenate([conv_prev.astype(u.dtype), u], axis=1)
    conv = conv_b + conv_w[0] * pad[:, 0:T] + conv_w[1] * pad[:, 1:T + 1] + conv_w[2] * pad[:, 2:T + 2]
    y_a = bg * conv
    new_conv = pad[:, T:]
    qh = q.reshape(Bsz, T, N_HEADS, DK).transpose(0, 2, 1, 3).astype(f32)
    kh = k.reshape(Bsz, T, N_HEADS, DK).transpose(0, 2, 1, 3).astype(f32) * (DK ** -0.5)
    vh = v.reshape(Bsz, T, N_HEADS, DV).transpose(0, 2, 1, 3).astype(f32)
    bgf = b_gate.astype(f32)
    i_pre = _softcap(ig.astype(f32) + bgf[:N_HEADS]).transpose(0, 2, 1)
    logf = jax.nn.log_sigmoid(_softcap(fg.astype(f32) + bgf[N_HEADS:])).transpose(0, 2, 1)
    hm, C1, n1, m1 = _mlstm_chunkwise(qh, kh, vh, i_pre, logf, C0.astype(f32), n0.astype(f32), m0.astype(f32))
    hm = hm * lax.rsqrt(jnp.mean(hm * hm, axis=-1, keepdims=True) + EPS)
    hm = hm.transpose(0, 2, 1, 3).reshape(Bsz, T, D_V) * hn_g.astype(f32)
    y_b = jax.nn.sigmoid(o) * hm.astype(o.dtype)
    merged = jax.nn.sigmoid(ga) * y_a + jax.nn.sigmoid(gb) * y_b
    return merged @ w_out, new_conv, C1, n1, m1


def _layer(x, c, conv_prev, C0, n0, m0, w_ada, b_ada, g1n, g2n, w_in, b_gate, conv_w, conv_b, hn_g, w_out, w_up, w_down):
    mod = (jax.nn.silu(c) @ w_ada + b_ada)[:, None, :]
    sh1, sc1, gt1, sh2, sc2, gt2 = jnp.split(mod, 6, axis=-1)
    h = _rmsnorm(x, g1n) * (1 + sc1) + sh1
    mix, new_conv, C1, n1, m1 = _mixer(h, conv_prev, C0, n0, m0, w_in, b_gate, conv_w, conv_b, hn_g, w_out)
    x = x + gt1 * mix
    h = _rmsnorm(x, g2n) * (1 + sc2) + sh2
    x = x + gt2 * (jnp.square(jax.nn.relu(h @ w_up)) @ w_down)
    return x, new_conv, C1, n1, m1


def setup_inputs(seed: int = 0) -> dict:
    key = jax.random.key(seed)
    ks = jax.random.split(key, 24)
    nrm = jax.random.normal
    f = jnp.float32
    b_gate = jnp.concatenate([0.5 * nrm(ks[0], (DEPTH, N_HEADS), f),
                              4.0 + 0.5 * nrm(ks[1], (DEPTH, N_HEADS), f)], axis=-1)
    return {
        "x_prompt": nrm(ks[2], (BATCH, SEQ, D_MODEL), f),
        "x_sample": nrm(ks[3], (DEC_BATCH, DEC_SEQ, D_MODEL), f),
        "state_conv": nrm(ks[4], (DEPTH, DEC_BATCH, CONV_W - 1, D_CONV), f),
        "state_C": 0.1 * nrm(ks[5], (DEPTH, DEC_BATCH, N_HEADS, DK, DV), f),
        "state_n": 0.1 * nrm(ks[6], (DEPTH, DEC_BATCH, N_HEADS, DK), f),
        "state_m": nrm(ks[7], (DEPTH, DEC_BATCH, N_HEADS), f),
        "c_prompt": nrm(ks[8], (BATCH, D_MODEL), f),
        "c_sample": nrm(ks[9], (DEC_BATCH, D_MODEL), f),
        "w_ada": 0.5 * D_MODEL ** -0.5 * nrm(ks[10], (DEPTH, D_MODEL, 6 * D_MODEL), f),
        "b_ada": 0.1 * nrm(ks[11], (DEPTH, 6 * D_MODEL), f),
        "g_norm1": 1.0 + 0.05 * nrm(ks[12], (DEPTH, D_MODEL), f),
        "g_norm2": 1.0 + 0.05 * nrm(ks[13], (DEPTH, D_MODEL), f),
        "w_in": D_MODEL ** -0.5 * nrm(ks[14], (DEPTH, D_MODEL, N_IN), f),
        "b_gate": b_gate,
        "conv_w": CONV_W ** -0.5 * nrm(ks[15], (DEPTH, CONV_W, D_CONV), f),
        "conv_b": 0.02 * nrm(ks[16], (DEPTH, D_CONV), f),
        "hn_g": 1.0 + 0.05 * nrm(ks[17], (DEPTH, D_V), f),
        "w_out": D_MODEL ** -0.5 * nrm(ks[18], (DEPTH, D_MODEL, D_MODEL), f),
        "w_up": D_MODEL ** -0.5 * nrm(ks[19], (DEPTH, D_MODEL, D_FF), f),
        "w_down": D_FF ** -0.5 * nrm(ks[20], (DEPTH, D_FF, D_MODEL), f),
        "g_final": 1.0 + 0.05 * nrm(ks[21], (D_MODEL,), f),
    }


def reference(x_prompt, x_sample, state_conv, state_C, state_n, state_m, c_prompt, c_sample,
              w_ada, b_ada, g_norm1, g_norm2, w_in, b_gate, conv_w, conv_b, hn_g, w_out, w_up, w_down, g_final):
    f32 = jnp.float32
    xp, xs = x_prompt, x_sample
    p_conv, p_C, p_n, p_m = [], [], [], []
    s_conv, s_C, s_n, s_m = [], [], [], []
    zc = jnp.zeros((BATCH, CONV_W - 1, D_CONV), xp.dtype)
    zC = jnp.zeros((BATCH, N_HEADS, DK, DV), f32)
    zn = jnp.zeros((BATCH, N_HEADS, DK), f32)
    zm = jnp.zeros((BATCH, N_HEADS), f32)
    for l in range(DEPTH):
        w = (w_ada[l], b_ada[l], g_norm1[l], g_norm2[l], w_in[l], b_gate[l], conv_w[l], conv_b[l],
             hn_g[l], w_out[l], w_up[l], w_down[l])
        xp, cv, C1, n1, m1 = _layer(xp, c_prompt, zc, zC, zn, zm, *w)
        p_conv.append(cv.astype(state_conv.dtype)); p_C.append(C1.astype(state_C.dtype))
        p_n.append(n1.astype(state_n.dtype)); p_m.append(m1.astype(state_m.dtype))
        xs, cv, C1, n1, m1 = _layer(xs, c_sample, state_conv[l], state_C[l], state_n[l], state_m[l], *w)
        s_conv.append(cv.astype(state_conv.dtype)); s_C.append(C1.astype(state_C.dtype))
        s_n.append(n1.astype(state_n.dtype)); s_m.append(m1.astype(state_m.dtype))
    y_prompt = _rmsnorm(xp, g_final)
    y_sample = _rmsnorm(xs, g_final)
    return (y_prompt, y_sample,
            jnp.stack(p_conv), jnp.stack(p_C), jnp.stack(p_n), jnp.stack(p_m),
            jnp.stack(s_conv), jnp.stack(s_C), jnp.stack(s_n), jnp.stack(s_m))
```

```python
import functools

import jax
import jax.numpy as jnp
from jax import lax
from jax.experimental import pallas as pl
from jax.experimental.pallas import tpu as pltpu

D_MODEL = 1024
N_HEADS = 4
DK = 128
DV = 256
D_FF = 4096
DEPTH = 4
CONV_W = 3
GATE_CAP = 15.0
EPS = 1e-6
CHUNK = 128
HEAD_COLS = 1024
STEP_COLS = 2048
NEG_BIG = -1e30
VMEM_LIMIT = 56 * 1024 * 1024

_OFF_XC, _OFF_BG, _OFF_CG = 0, 1024, 2048
_OFF_Q, _OFF_K, _OFF_V, _OFF_O = 3072, 3584, 4096, 5120
_OFF_IG, _OFF_GA, _OFF_GB = 6144, 6152, 7176

_M_YA, _M_Q, _M_K, _M_V, _M_OG = 0, 256, 384, 512, 768

_BF = jnp.bfloat16
_F32 = jnp.float32


def _dot(a, b):
    return jnp.dot(a, b, preferred_element_type=_F32)


def _sigmoid(x):
    return 1.0 / (1.0 + jnp.exp(-x))


def _rms_mod(x, g, sc, sh):
    ms = jnp.mean(x * x, axis=-1, keepdims=True)
    return (x * lax.rsqrt(ms + EPS) * g) * (1.0 + sc) + sh


def _ada_kernel(c_ref, w_ref, b_ref, o_ref):
    c = c_ref[...]
    a = (c * _sigmoid(c)).astype(_BF)
    o_ref[0] = _dot(a, w_ref[0].astype(_BF)) + b_ref[0]


def _ada_call(c_all, w_ada, b_ada):
    n_rows = c_all.shape[0]
    tn = 1024
    return pl.pallas_call(
        _ada_kernel,
        grid=(DEPTH, 6 * D_MODEL // tn),
        in_specs=[
            pl.BlockSpec((n_rows, D_MODEL), lambda l, j: (0, 0)),
            pl.BlockSpec((1, D_MODEL, tn), lambda l, j: (l, 0, j)),
            pl.BlockSpec((1, 1, tn), lambda l, j: (l, 0, j)),
        ],
        out_specs=pl.BlockSpec((1, n_rows, tn), lambda l, j: (l, 0, j)),
        out_shape=jax.ShapeDtypeStruct((DEPTH, n_rows, 6 * D_MODEL), _F32),
        compiler_params=pltpu.CompilerParams(
            dimension_semantics=("arbitrary", "arbitrary"), vmem_limit_bytes=VMEM_LIMIT),
        name="ada_mod",
    )(c_all, w_ada, b_ada.reshape(DEPTH, 1, 6 * D_MODEL))


def _inproj_common(x_ref, sc, sh, g_ref, wg_ref, gt_ref, h_scr):
    h = _rms_mod(x_ref[...], g_ref[...], sc, sh).astype(_BF)
    h_scr[...] = h
    gates_t = _dot(h, wg_ref[...]).T
    for c in range(gt_ref.shape[0]):
        gt_ref[c] = gates_t[0:8, c * CHUNK:(c + 1) * CHUNK]


def _inproj_epilogue(z, conv, mid_ref, out_dtype):
    bg = z[:, 256:512]
    ga = z[:, 768:1024]
    q = z[:, 1024:1152]
    k = z[:, 1152:1280] * (DK ** -0.5)
    v = z[:, 1280:1536]
    o = z[:, 1536:1792]
    gb = z[:, 1792:2048]
    mid_ref[:, _M_YA:_M_YA + 256] = (_sigmoid(ga) * (bg * conv)).astype(out_dtype)
    mid_ref[:, _M_Q:_M_Q + 128] = q.astype(out_dtype)
    mid_ref[:, _M_K:_M_K + 128] = k.astype(out_dtype)
    mid_ref[:, _M_V:_M_V + 256] = v.astype(out_dtype)
    mid_ref[:, _M_OG:_M_OG + 256] = (_sigmoid(o) * _sigmoid(gb)).astype(out_dtype)


def _inproj_prompt_kernel(x_ref, sc_ref, sh_ref, g_ref, w_ref, wg_ref, cw_ref, cb_ref, cprev_ref,
                          mid_ref, gt_ref, nconv_ref, h_scr, ubuf, carry, *, tiles_per_seq):
    i = pl.program_id(0)
    s = pl.program_id(1)
    tm = x_ref.shape[0]

    @pl.when(s == 0)
    def _():
        _inproj_common(x_ref, sc_ref[0], sh_ref[0], g_ref, wg_ref, gt_ref, h_scr)

    z = _dot(h_scr[...], w_ref[...])
    u = z[:, 512:768] * z[:, 0:256]
    seq_start = (i % tiles_per_seq) == 0

    @pl.when(seq_start)
    def _():
        ubuf[6:8, :] = cprev_ref[0]

    @pl.when(jnp.logical_not(seq_start))
    def _():
        ubuf[6:8, :] = carry[s]

    ubuf[8:8 + tm, :] = u
    cw = cw_ref[...]
    conv = (cb_ref[...] + cw[0:1] * ubuf[6:6 + tm, :] + cw[1:2] * ubuf[7:7 + tm, :] + cw[2:3] * u)
    last2 = u[tm - 2:tm, :]
    carry[s] = last2
    nconv_ref[0] = last2
    _inproj_epilogue(z, conv, mid_ref, _BF)


def _inproj_prompt_call(x, sc, sh, g, w_main, w_gate, conv_w, conv_b, conv_prev, *, seq_len, tm):
    m_rows = x.shape[0]
    n_seq = m_rows // seq_len
    tiles_per_seq = seq_len // tm
    n_steps = N_HEADS
    kern = functools.partial(_inproj_prompt_kernel, tiles_per_seq=tiles_per_seq)
    return pl.pallas_call(
        kern,
        grid=(m_rows // tm, n_steps),
        in_specs=[
            pl.BlockSpec((tm, D_MODEL), lambda i, s: (i, 0)),
            pl.BlockSpec((1, 1, D_MODEL), lambda i, s: (i // tiles_per_seq, 0, 0)),
            pl.BlockSpec((1, 1, D_MODEL), lambda i, s: (i // tiles_per_seq, 0, 0)),
            pl.BlockSpec((1, D_MODEL), lambda i, s: (0, 0)),
            pl.BlockSpec((D_MODEL, STEP_COLS), lambda i, s: (0, s)),
            pl.BlockSpec((D_MODEL, 128), lambda i, s: (0, 0)),
            pl.BlockSpec((CONV_W, 256), lambda i, s: (0, s)),
            pl.BlockSpec((1, 256), lambda i, s: (0, s)),
            pl.BlockSpec((1, CONV_W - 1, 256), lambda i, s: (i // tiles_per_seq, 0, s)),
        ],
        out_specs=[
            pl.BlockSpec((tm, HEAD_COLS), lambda i, s: (i, s)),
            pl.BlockSpec((tm // CHUNK, 8, CHUNK), lambda i, s: (i, 0, 0)),
            pl.BlockSpec((1, CONV_W - 1, 256), lambda i, s: (i // tiles_per_seq, 0, s)),
        ],
        out_shape=[
            jax.ShapeDtypeStruct((m_rows, N_HEADS * HEAD_COLS), _BF),
            jax.ShapeDtypeStruct((m_rows // CHUNK, 8, CHUNK), _F32),
            jax.ShapeDtypeStruct((n_seq, CONV_W - 1, D_MODEL), _F32),
        ],
        scratch_shapes=[
            pltpu.VMEM((tm, D_MODEL), _BF),
            pltpu.VMEM((tm + 8, 256), _F32),
            pltpu.VMEM((n_steps, CONV_W - 1, 256), _F32),
        ],
        compiler_params=pltpu.CompilerParams(
            dimension_semantics=("arbitrary", "arbitrary"), vmem_limit_bytes=VMEM_LIMIT),
        name="inproj_prompt",
    )(x, sc, sh, g, w_main, w_gate, conv_w, conv_b, conv_prev)


def _inproj_sample_kernel(x_ref, sc_ref, sh_ref, g_ref, w_ref, wg_ref, cw_ref, cb_ref, s1_ref, s2_ref,
                          mid_ref, gt_ref, u_ref, h_scr, ubuf, *, seq_len):
    s = pl.program_id(0)
    tm = x_ref.shape[0]

    @pl.when(s == 0)
    def _():
        _inproj_common(x_ref, sc_ref[...], sh_ref[...], g_ref, wg_ref, gt_ref, h_scr)

    z = _dot(h_scr[...], w_ref[...])
    u = z[:, 512:768] * z[:, 0:256]
    ubuf[0:8, :] = jnp.zeros((8, 256), _F32)
    ubuf[8:8 + tm, :] = u
    t_in_seq = lax.broadcasted_iota(jnp.int32, (tm, 256), 0) % seq_len
    p1 = jnp.where(t_in_seq >= 1, ubuf[7:7 + tm, :], s1_ref[...])
    p2 = jnp.where(t_in_seq >= 2, ubuf[6:6 + tm, :], s2_ref[...])
    cw = cw_ref[...]
    conv = cb_ref[...] + cw[0:1] * p2 + cw[1:2] * p1 + cw[2:3] * u
    u_ref[...] = u
    _inproj_epilogue(z, conv, mid_ref, _F32)


def _inproj_sample_call(x, sc, sh, g, w_main, w_gate, conv_w, conv_b, s1, s2, *, seq_len):
    tm = x.shape[0]
    kern = functools.partial(_inproj_sample_kernel, seq_len=seq_len)
    return pl.pallas_call(
        kern,
        grid=(N_HEADS,),
        in_specs=[
            pl.BlockSpec((tm, D_MODEL), lambda s: (0, 0)),
            pl.BlockSpec((tm, D_MODEL), lambda s: (0, 0)),
            pl.BlockSpec((tm, D_MODEL), lambda s: (0, 0)),
            pl.BlockSpec((1, D_MODEL), lambda s: (0, 0)),
            pl.BlockSpec((D_MODEL, STEP_COLS), lambda s: (0, s)),
            pl.BlockSpec((D_MODEL, 128), lambda s: (0, 0)),
            pl.BlockSpec((CONV_W, 256), lambda s: (0, s)),
            pl.BlockSpec((1, 256), lambda s: (0, s)),
            pl.BlockSpec((tm, 256), lambda s: (0, s)),
            pl.BlockSpec((tm, 256), lambda s: (0, s)),
        ],
        out_specs=[
            pl.BlockSpec((tm, HEAD_COLS), lambda s: (0, s)),
            pl.BlockSpec((tm // CHUNK, 8, CHUNK), lambda s: (0, 0, 0)),
            pl.BlockSpec((tm, 256), lambda s: (0, s)),
        ],
        out_shape=[
            jax.ShapeDtypeStruct((tm, N_HEADS * HEAD_COLS), _F32),
            jax.ShapeDtypeStruct((tm // CHUNK, 8, CHUNK), _F32),
            jax.ShapeDtypeStruct((tm, D_MODEL), _F32),
        ],
        scratch_shapes=[
            pltpu.VMEM((tm, D_MODEL), _BF),
            pltpu.VMEM((tm + 8, 256), _F32),
        ],
        compiler_params=pltpu.CompilerParams(
            dimension_semantics=("arbitrary",), vmem_limit_bytes=VMEM_LIMIT),
        name="inproj_sample",
    )(x, sc, sh, g, w_main, w_gate, conv_w, conv_b, s1, s2)


def _scan_lanes(x, combine, identity):
    lane = lax.broadcasted_iota(jnp.int32, x.shape, 1)
    shift = 1
    while shift < x.shape[1]:
        x = combine(x, jnp.where(lane >= shift, pltpu.roll(x, shift, 1), identity))
        shift *= 2
    return x


def _softcap(a):
    return GATE_CAP * jnp.tanh(a / GATE_CAP)


def _mlstm_chunk(gates, bgate, m_prev, get, c_ref, n_ref, hng, *, lq, t_valid):
    pre = _softcap(gates + bgate)
    i_pre = pre
    f_pre = pltpu.roll(pre, 4, 0)
    logf = -(jnp.maximum(-f_pre, 0.0) + jnp.log(1.0 + jnp.exp(-jnp.abs(f_pre))))
    if t_valid < CHUNK:
        lane = lax.broadcasted_iota(jnp.int32, (8, CHUNK), 1)
        i_pre = jnp.where(lane < t_valid, i_pre, NEG_BIG)
        logf = jnp.where(lane < t_valid, logf, 0.0)
    b = _scan_lanes(logf, jnp.add, 0.0)
    g = i_pre - b
    m_run = jnp.maximum(_scan_lanes(g, jnp.maximum, -3e38), m_prev)
    inter = jnp.exp(m_prev - m_run)
    em = jnp.exp(-(b + m_run))
    m_last = jnp.max(m_run, axis=1, keepdims=True)
    ws = jnp.exp(g - m_last)
    decay = jnp.exp(m_prev - m_last)
    m_new = b[:, CHUNK - 1:CHUNK] + m_last
    stack = jnp.concatenate([m_run, inter, em, ws, jnp.zeros((CHUNK - 32, CHUNK), _F32)], axis=0)
    cols = stack.T
    row_id = lax.broadcasted_iota(jnp.int32, (lq, CHUNK), 0)
    col_id = lax.broadcasted_iota(jnp.int32, (lq, CHUNK), 1)
    causal = col_id <= row_id
    outs = []
    for h in range(N_HEADS):
        q = get("q", h)
        k = get("k", h)
        v = get("v", h)
        m_col = cols[0:lq, h:h + 1]
        inter_col = cols[0:lq, 8 + h:9 + h]
        em_col = cols[0:lq, 16 + h:17 + h]
        ws_col = cols[:, 24 + h:25 + h]
        dmat = jnp.where(causal, jnp.exp(g[h:h + 1, :] - m_col), 0.0)
        qk = lax.dot_general(q, k, (((1,), (1,)), ((), ())), preferred_element_type=_F32)
        smat = qk * dmat
        c_old = c_ref[h]
        n_old = n_ref[h:h + 1, :]
        num = _dot(smat.astype(_BF), v) + inter_col * _dot(q, c_old.astype(_BF))
        den = (jnp.sum(smat, axis=-1, keepdims=True)
               + inter_col * jnp.sum(q.astype(_F32) * n_old, axis=-1, keepdims=True))
        hraw = num * (1.0 / jnp.maximum(jnp.abs(den), em_col))
        ms = jnp.mean(hraw * hraw, axis=-1, keepdims=True)
        hn = hraw * lax.rsqrt(ms + EPS) * hng[:, h * DV:(h + 1) * DV]
        outs.append(get("ya", h) + get("og", h) * hn)
        dec = decay[h:h + 1, 0:1]
        wv = (ws_col * v.astype(_F32)).astype(_BF)
        k_t = k.astype(_F32).T.astype(_BF)
        c_ref[h] = dec * c_old + _dot(k_t, wv)
        n_ref[h:h + 1, :] = dec * n_old + jnp.sum(ws_col * k.astype(_F32), axis=0, keepdims=True)
    return jnp.concatenate(outs, axis=1), m_new


def _mlstm_prompt_kernel(mid_ref, gt_ref, bg_ref, hng_ref, c0_ref, n0_ref, m0_ref,
                         mg_ref, c_ref, n_ref, m_ref):
    t = pl.program_id(1)

    @pl.when(t == 0)
    def _():
        c_ref[...] = c0_ref[...]
        n_ref[...] = n0_ref[...]
        m_ref[...] = m0_ref[...]

    bgate = bg_ref[...]
    hng = hng_ref[...]
    n_chunks = mid_ref.shape[0] // CHUNK

    def body(c, carry):
        r0 = pl.multiple_of(c * CHUNK, CHUNK)

        def get(name, h):
            base = h * HEAD_COLS
            if name == "q":
                return mid_ref[pl.ds(r0, CHUNK), base + _M_Q:base + _M_Q + 128]
            if name == "k":
                return mid_ref[pl.ds(r0, CHUNK), base + _M_K:base + _M_K + 128]
            if name == "v":
                return mid_ref[pl.ds(r0, CHUNK), base + _M_V:base + _M_V + 256]
            if name == "ya":
                return mid_ref[pl.ds(r0, CHUNK), base + _M_YA:base + _M_YA + 256].astype(_F32)
            return mid_ref[pl.ds(r0, CHUNK), base + _M_OG:base + _M_OG + 256].astype(_F32)

        merged, m_new = _mlstm_chunk(gt_ref[c], bgate, m_ref[0], get, c_ref.at[0], n_ref.at[0], hng,
                                     lq=CHUNK, t_valid=CHUNK)
        mg_ref[pl.ds(r0, CHUNK), :] = merged.astype(_BF)
        m_ref[0] = jnp.broadcast_to(m_new, (8, CHUNK))
        return carry

    lax.fori_loop(0, n_chunks, body, 0)


def _mlstm_prompt_call(mid, gates_t, bgate, hng, c0, n0, m0, *, seq_len, tb):
    m_rows = mid.shape[0]
    n_seq = m_rows // seq_len
    steps = seq_len // tb
    return pl.pallas_call(
        _mlstm_prompt_kernel,
        grid=(n_seq, steps),
        in_specs=[
            pl.BlockSpec((tb, N_HEADS * HEAD_COLS), lambda b, t: (b * steps + t, 0)),
            pl.BlockSpec((tb // CHUNK, 8, CHUNK), lambda b, t: (b * steps + t, 0, 0)),
            pl.BlockSpec((8, CHUNK), lambda b, t: (0, 0)),
            pl.BlockSpec((1, D_MODEL), lambda b, t: (0, 0)),
            pl.BlockSpec((1, N_HEADS, DK, DV), lambda b, t: (b, 0, 0, 0)),
            pl.BlockSpec((1, N_HEADS, DK), lambda b, t: (b, 0, 0)),
            pl.BlockSpec((1, 8, CHUNK), lambda b, t: (b, 0, 0)),
        ],
        out_specs=[
            pl.BlockSpec((tb, D_MODEL), lambda b, t: (b * steps + t, 0)),
            pl.BlockSpec((1, N_HEADS, DK, DV), lambda b, t: (b, 0, 0, 0)),
            pl.BlockSpec((1, N_HEADS, DK), lambda b, t: (b, 0, 0)),
            pl.BlockSpec((1, 8, CHUNK), lambda b, t: (b, 0, 0)),
        ],
        out_shape=[
            jax.ShapeDtypeStruct((m_rows, D_MODEL), _BF),
            jax.ShapeDtypeStruct((n_seq, N_HEADS, DK, DV), _F32),
            jax.ShapeDtypeStruct((n_seq, N_HEADS, DK), _F32),
            jax.ShapeDtypeStruct((n_seq, 8, CHUNK), _F32),
        ],
        compiler_params=pltpu.CompilerParams(
            dimension_semantics=("arbitrary", "arbitrary"), vmem_limit_bytes=VMEM_LIMIT),
        name="mlstm_prompt",
    )(mid, gates_t, bgate, hng, c0, n0, m0)


def _mlstm_sample_kernel(mid_ref, gt_ref, bg_ref, hng_ref, c0_ref, n0_ref, m0_ref,
                         mg_ref, c_ref, n_ref, m_ref, rows, *, seq_len):
    bb = mid_ref.shape[0]
    c_ref[...] = c0_ref[...]
    n_ref[...] = n0_ref[...]
    rows[...] = jnp.zeros(rows.shape, _F32)
    bgate = bg_ref[...]
    hng = hng_ref[...]

    def body(b, carry):
        rows[0:seq_len, :] = mid_ref[b]

        def get(name, h):
            base = h * HEAD_COLS
            if name == "q":
                return rows[0:8, base + _M_Q:base + _M_Q + 128].astype(_BF)
            if name == "k":
                return rows[:, base + _M_K:base + _M_K + 128].astype(_BF)
            if name == "v":
                return rows[:, base + _M_V:base + _M_V + 256].astype(_BF)
            if name == "ya":
                return rows[0:8, base + _M_YA:base + _M_YA + 256]
            return rows[0:8, base + _M_OG:base + _M_OG + 256]

        merged, m_new = _mlstm_chunk(gt_ref[b], bgate, m0_ref[b], get, c_ref.at[b], n_ref.at[b], hng,
                                     lq=8, t_valid=seq_len)
        mg_ref[b] = merged[0:seq_len, :]
        m_ref[b] = jnp.broadcast_to(m_new, (8, CHUNK))
        return carry

    lax.fori_loop(0, bb, body, 0)


def _mlstm_sample_call(mid3, gates, bgate, hng, c0, n0, m0, *, bb):
    n_seq, seq_len, _ = mid3.shape
    kern = functools.partial(_mlstm_sample_kernel, seq_len=seq_len)
    return pl.pallas_call(
        kern,
        grid=(n_seq // bb,),
        in_specs=[
            pl.BlockSpec((bb, seq_len, N_HEADS * HEAD_COLS), lambda i: (i, 0, 0)),
            pl.BlockSpec((bb, 8, CHUNK), lambda i: (i, 0, 0)),
            pl.BlockSpec((8, CHUNK), lambda i: (0, 0)),
            pl.BlockSpec((1, D_MODEL), lambda i: (0, 0)),
            pl.BlockSpec((bb, N_HEADS, DK, DV), lambda i: (i, 0, 0, 0)),
            pl.BlockSpec((bb, N_HEADS, DK), lambda i: (i, 0, 0)),
            pl.BlockSpec((bb, 8, CHUNK), lambda i: (i, 0, 0)),
        ],
        out_specs=[
            pl.BlockSpec((bb, seq_len, D_MODEL), lambda i: (i, 0, 0)),
            pl.BlockSpec((bb, N_HEADS, DK, DV), lambda i: (i, 0, 0, 0)),
            pl.BlockSpec((bb, N_HEADS, DK), lambda i: (i, 0, 0)),
            pl.BlockSpec((bb, 8, CHUNK), lambda i: (i, 0, 0)),
        ],
        out_shape=[
            jax.ShapeDtypeStruct((n_seq, seq_len, D_MODEL), _F32),
            jax.ShapeDtypeStruct((n_seq, N_HEADS, DK, DV), _F32),
            jax.ShapeDtypeStruct((n_seq, N_HEADS, DK), _F32),
            jax.ShapeDtypeStruct((n_seq, 8, CHUNK), _F32),
        ],
        scratch_shapes=[pltpu.VMEM((CHUNK, N_HEADS * HEAD_COLS), _F32)],
        compiler_params=pltpu.CompilerParams(
            dimension_semantics=("arbitrary",), vmem_limit_bytes=VMEM_LIMIT),
        name="mlstm_sample",
    )(mid3, gates, bgate, hng, c0, n0, m0)


def _mlp_kernel(x_ref, mg_ref, gt1_ref, sc2_ref, sh2_ref, gt2_ref, g2_ref, wo_ref, wu_ref, wd_ref, gf_ref,
                o_ref, xmid, h2, acc, *, per_seq_mod, final_norm):
    f = pl.program_id(1)

    def mod(ref):
        return ref[0] if per_seq_mod else ref[...]

    @pl.when(f == 0)
    def _():
        mix = _dot(mg_ref[...].astype(_BF), wo_ref[...])
        xm = x_ref[...] + mod(gt1_ref) * mix
        xmid[...] = xm
        h2[...] = _rms_mod(xm, g2_ref[...], mod(sc2_ref), mod(sh2_ref)).astype(_BF)
        acc[...] = jnp.zeros(acc.shape, _F32)

    a = jnp.maximum(_dot(h2[...], wu_ref[...]), 0.0)
    acc[...] += _dot((a * a).astype(_BF), wd_ref[...])

    @pl.when(f == pl.num_programs(1) - 1)
    def _():
        y = xmid[...] + mod(gt2_ref) * acc[...]
        if final_norm:
            ms = jnp.mean(y * y, axis=-1, keepdims=True)
            y = y * lax.rsqrt(ms + EPS) * gf_ref[...]
        o_ref[...] = y


def _mlp_call(x, merged, gt1, sc2, sh2, gt2, g2, w_out, w_up, w_down, g_final, *,
              tm, tf, seq_len, final_norm):
    m_rows = x.shape[0]
    per_seq_mod = seq_len is not None
    if per_seq_mod:
        tiles_per_seq = seq_len // tm
        mod_spec = pl.BlockSpec((1, 1, D_MODEL), lambda i, f: (i // tiles_per_seq, 0, 0))
    else:
        mod_spec = pl.BlockSpec((tm, D_MODEL), lambda i, f: (i, 0))
    kern = functools.partial(_mlp_kernel, per_seq_mod=per_seq_mod, final_norm=final_norm)
    return pl.pallas_call(
        kern,
        grid=(m_rows // tm, D_FF // tf),
        in_specs=[
            pl.BlockSpec((tm, D_MODEL), lambda i, f: (i, 0)),
            pl.BlockSpec((tm, D_MODEL), lambda i, f: (i, 0)),
            mod_spec, mod_spec, mod_spec, mod_spec,
            pl.BlockSpec((1, D_MODEL), lambda i, f: (0, 0)),
            pl.BlockSpec((D_MODEL, D_MODEL), lambda i, f: (0, 0)),
            pl.BlockSpec((D_MODEL, tf), lambda i, f: (0, f)),
            pl.BlockSpec((tf, D_MODEL), lambda i, f: (f, 0)),
            pl.BlockSpec((1, D_MODEL), lambda i, f: (0, 0)),
        ],
        out_specs=pl.BlockSpec((tm, D_MODEL), lambda i, f: (i, 0)),
        out_shape=jax.ShapeDtypeStruct((m_rows, D_MODEL), _F32),
        scratch_shapes=[
            pltpu.VMEM((tm, D_MODEL), _F32),
            pltpu.VMEM((tm, D_MODEL), _BF),
            pltpu.VMEM((tm, D_MODEL), _F32),
        ],
        compiler_params=pltpu.CompilerParams(
            dimension_semantics=("arbitrary", "arbitrary"), vmem_limit_bytes=VMEM_LIMIT),
        name="outproj_mlp",
    )(x, merged, gt1, sc2, sh2, gt2, g2, w_out, w_up, w_down, g_final)


def _arrange_w_in(w_in):
    blocks = []
    for s in range(N_HEADS):
        def cols(off, width):
            return w_in[:, :, off + s * width:off + (s + 1) * width]
        blocks += [cols(_OFF_XC, 256), cols(_OFF_BG, 256), cols(_OFF_CG, 256), cols(_OFF_GA, 256),
                   cols(_OFF_Q, 128), cols(_OFF_K, 128), cols(_OFF_V, 256), cols(_OFF_O, 256),
                   cols(_OFF_GB, 256)]
    w_main = jnp.concatenate(blocks, axis=-1).astype(_BF)
    w_gate = jnp.pad(w_in[:, :, _OFF_IG:_OFF_IG + 8], ((0, 0), (0, 0), (0, 120))).astype(_BF)
    return w_main, w_gate


def kernel(x_prompt, x_sample, state_conv, state_C, state_n, state_m, c_prompt, c_sample,
           w_ada, b_ada, g_norm1, g_norm2, w_in, b_gate, conv_w, conv_b, hn_g, w_out, w_up, w_down, g_final):
    n_p, seq_p, _ = x_prompt.shape
    n_s, seq_s, _ = x_sample.shape
    rows_s = n_s * seq_s

    w_main, w_gate = _arrange_w_in(w_in)
    w_out_b = w_out.astype(_BF)
    w_up_b = w_up.astype(_BF)
    w_down_b = w_down.astype(_BF)

    mod = _ada_call(jnp.concatenate([c_prompt, c_sample], axis=0), w_ada, b_ada)
    mod_p = mod[:, :n_p].reshape(DEPTH, n_p, 6, 1, D_MODEL)
    mod_s = jnp.repeat(mod[:, n_p:], seq_s, axis=1).reshape(DEPTH, rows_s, 6, D_MODEL)

    xp = x_prompt.reshape(n_p * seq_p, D_MODEL)
    xs = x_sample.reshape(rows_s, D_MODEL)
    gfin = g_final.reshape(1, D_MODEL)

    zeros_conv = jnp.zeros((n_p, CONV_W - 1, D_MODEL), _F32)
    zeros_c = jnp.zeros((n_p, N_HEADS, DK, DV), _F32)
    zeros_n = jnp.zeros((n_p, N_HEADS, DK), _F32)
    zeros_m = jnp.zeros((n_p, 8, CHUNK), _F32)
    m_s_in = jnp.broadcast_to(
        jnp.pad(state_m, ((0, 0), (0, 0), (0, 8 - N_HEADS)))[..., None], (DEPTH, n_s, 8, CHUNK))

    p_conv, p_c, p_n, p_m, s_conv, s_c, s_n, s_m = [], [], [], [], [], [], [], []
    for l in range(DEPTH):
        g1 = g_norm1[l].reshape(1, D_MODEL)
        g2 = g_norm2[l].reshape(1, D_MODEL)
        bgate = jnp.broadcast_to(b_gate[l][:, None], (8, CHUNK))
        hng = hn_g[l].reshape(1, D_MODEL)
        cb = conv_b[l].reshape(1, D_MODEL)
        final = l == DEPTH - 1

        sh1, sc1, gt1, sh2, sc2, gt2 = (mod_p[l, :, j] for j in range(6))
        mid, gates_t, nconv = _inproj_prompt_call(
            xp, sc1, sh1, g1, w_main[l], w_gate[l], conv_w[l], cb, zeros_conv, seq_len=seq_p, tm=512)
        merged, c1, n1, m1 = _mlstm_prompt_call(
            mid, gates_t, bgate, hng, zeros_c, zeros_n, zeros_m, seq_len=seq_p, tb=512)
        xp = _mlp_call(xp, merged, gt1, sc2, sh2, gt2, g2, w_out_b[l], w_up_b[l], w_down_b[l], gfin,
                       tm=512, tf=1024, seq_len=seq_p, final_norm=final)
        p_conv.append(nconv); p_c.append(c1); p_n.append(n1); p_m.append(m1[:, :N_HEADS, 0])

        sh1, sc1, gt1, sh2, sc2, gt2 = (mod_s[l, :, j] for j in range(6))
        prev = state_conv[l]
        zrow = jnp.zeros((n_s, 1, D_MODEL), _F32)
        s1 = jnp.concatenate([prev[:, 1:2], zrow, zrow, zrow], axis=1).reshape(rows_s, D_MODEL)
        s2 = jnp.concatenate([prev[:, 0:1], prev[:, 1:2], zrow, zrow], axis=1).reshape(rows_s, D_MODEL)
        mid, gates_t, u_all = _inproj_sample_call(
            xs, sc1, sh1, g1, w_main[l], w_gate[l], conv_w[l], cb, s1, s2, seq_len=seq_s)
        gates = gates_t.reshape(rows_s // CHUNK, 8, CHUNK // seq_s, seq_s).transpose(0, 2, 1, 3)
        gates = jnp.pad(gates.reshape(n_s, 8, seq_s), ((0, 0), (0, 0), (0, CHUNK - seq_s)))
        merged3, c1, n1, m1 = _mlstm_sample_call(
            mid.reshape(n_s, seq_s, N_HEADS * HEAD_COLS), gates, bgate, hng,
            state_C[l], state_n[l], m_s_in[l], bb=8)
        xs = _mlp_call(xs, merged3.reshape(rows_s, D_MODEL), gt1, sc2, sh2, gt2, g2,
                       w_out_b[l], w_up_b[l], w_down_b[l], gfin,
                       tm=rows_s, tf=1024, seq_len=None, final_norm=final)
        s_conv.append(u_all.reshape(n_s, seq_s, D_MODEL)[:, seq_s - (CONV_W - 1):])
        s_c.append(c1); s_n.append(n1); s_m.append(m1[:, :N_HEADS, 0])

    return (xp.reshape(n_p, seq_p, D_MODEL), xs.reshape(n_s, seq_s, D_MODEL),
            jnp.stack(p_conv), jnp.stack(p_c), jnp.stack(p_n), jnp.stack(p_m),
            jnp.stack(s_conv), jnp.stack(s_c), jnp.stack(s_n), jnp.stack(s_m))
```

```python
import functools

import jax
import jax.numpy as jnp
from jax import lax
from jax.experimental import pallas as pl
from jax.experimental.pallas import tpu as pltpu

D_MODEL = 1024
N_HEADS = 4
DK = 128
DV = 256
D_FF = 4096
DEPTH = 4
CONV_W = 3
GATE_CAP = 15.0
EPS = 1e-6
CHUNK = 128
HEAD_COLS = 1024
BIAS_ROWS = 16
GATE_ROWS = 24
EXT_COLS = DV + CHUNK
NEG_BIG = -1e30
VMEM_LIMIT = 56 * 1024 * 1024

_OFF_XC, _OFF_BG, _OFF_CG = 0, 1024, 2048
_OFF_Q, _OFF_K, _OFF_V, _OFF_O = 3072, 3584, 4096, 5120
_OFF_IG, _OFF_GA, _OFF_GB = 6144, 6152, 7176
_HEAD_W = _OFF_IG

_M_YA, _M_Q, _M_K, _M_V, _M_OG = 0, 256, 384, 512, 768

_BF = jnp.bfloat16
_F32 = jnp.float32


def _dot(a, b):
    return jnp.dot(a, b, preferred_element_type=_F32)


def _dot_nt(a, b):
    return lax.dot_general(a, b, (((1,), (1,)), ((), ())), preferred_element_type=_F32)


def _sigmoid(x):
    return 1.0 / (1.0 + jnp.exp(-x))


def _rms_mod(x, g, sc, sh):
    ms = jnp.mean(x * x, axis=-1, keepdims=True)
    return (x * lax.rsqrt(ms + EPS) * g) * (1.0 + sc) + sh


def _params(n_axes):
    return pltpu.CompilerParams(dimension_semantics=("arbitrary",) * n_axes, vmem_limit_bytes=VMEM_LIMIT)


def _ada_kernel(c_ref, w_ref, b_ref, o_ref):
    c = c_ref[...]
    a = (c * _sigmoid(c)).astype(_BF)
    o_ref[0] = _dot(a, w_ref[0].astype(_BF)) + b_ref[0]


def _ada_call(c_all, w_ada, b_ada):
    n_rows = c_all.shape[0]
    tn = 1024
    return pl.pallas_call(
        _ada_kernel,
        grid=(DEPTH, 6 * D_MODEL // tn),
        in_specs=[
            pl.BlockSpec((n_rows, D_MODEL), lambda l, j: (0, 0)),
            pl.BlockSpec((1, D_MODEL, tn), lambda l, j: (l, 0, j)),
            pl.BlockSpec((1, 1, tn), lambda l, j: (l, 0, j)),
        ],
        out_specs=pl.BlockSpec((1, n_rows, tn), lambda l, j: (l, 0, j)),
        out_shape=jax.ShapeDtypeStruct((DEPTH, n_rows, 6 * D_MODEL), _F32),
        compiler_params=_params(2),
        name="ada_mod",
    )(c_all, w_ada, b_ada.reshape(DEPTH, 1, 6 * D_MODEL))


def _w_in_specs(layer, idx):
    def spec(width, first_block):
        return pl.BlockSpec((1, D_MODEL, width), lambda *g: (layer, 0, first_block + idx(*g)))
    head = [spec(256, _OFF_XC // 256), spec(256, _OFF_BG // 256), spec(256, _OFF_CG // 256),
            spec(128, _OFF_Q // 128), spec(128, _OFF_K // 128), spec(256, _OFF_V // 256),
            spec(256, _OFF_O // 256)]
    tail = [spec(256, 0), spec(256, (_OFF_GB - _OFF_GA) // 256)]
    return head, tail


def _scan_lanes(x, combine, identity, seg_len):
    pos = lax.broadcasted_iota(jnp.int32, x.shape, 1) % seg_len
    shift = 1
    while shift < seg_len:
        x = combine(x, jnp.where(pos >= shift, pltpu.roll(x, shift, 1), identity))
        shift *= 2
    return x


def _softcap(a):
    return GATE_CAP * jnp.tanh(a / GATE_CAP)


def _inproj_common(x_ref, sc, sh, g_ref, wg_ref, bg_ref, gt_ref, h_scr, seg_len):
    h = _rms_mod(x_ref[...], g_ref[0], sc, sh).astype(_BF)
    h_scr[...] = h
    gates_t = _dot(h, wg_ref[0]).T
    n_chunks = gt_ref.shape[0]
    bias = bg_ref[0]

    def stack(r0):
        return jnp.concatenate([gates_t[r0:r0 + 8, c * CHUNK:(c + 1) * CHUNK] for c in range(n_chunks)], axis=0)

    i_pre = _softcap(stack(0) + jnp.concatenate([bias[0:8]] * n_chunks, axis=0))
    f_pre = _softcap(stack(8) + jnp.concatenate([bias[8:16]] * n_chunks, axis=0))
    logf = -(jnp.maximum(-f_pre, 0.0) + jnp.log(1.0 + jnp.exp(-jnp.abs(f_pre))))
    b = _scan_lanes(logf, jnp.add, 0.0, seg_len)
    g = i_pre - b
    cm = _scan_lanes(g, jnp.maximum, -3e38, seg_len)
    for c in range(n_chunks):
        gt_ref[c, 0:8] = b[c * 8:(c + 1) * 8]
        gt_ref[c, 8:16] = g[c * 8:(c + 1) * 8]
        gt_ref[c, 16:24] = cm[c * 8:(c + 1) * 8]


def _inproj_project(h, w_refs):
    wxc, wbg, wcg, wq, wk, wv, wo, wga, wgb = w_refs
    xc = _dot(h, wxc[0])
    cg = _dot(h, wcg[0])
    qk = _dot(h, jnp.concatenate([wq[0], wk[0]], axis=1))
    return dict(u=cg * xc, bg=_dot(h, wbg[0]), ga=_dot(h, wga[0]), qk=qk,
                v=_dot(h, wv[0]), o=_dot(h, wo[0]), gb=_dot(h, wgb[0]))


def _inproj_epilogue(p, conv, mid_ref, out_dtype):
    lane = lax.broadcasted_iota(jnp.int32, p["qk"].shape, 1)
    qk = jnp.where(lane < DK, p["qk"], p["qk"] * (DK ** -0.5))
    mid_ref[:, _M_YA:_M_YA + 256] = (_sigmoid(p["ga"]) * (p["bg"] * conv)).astype(out_dtype)
    mid_ref[:, _M_Q:_M_Q + 256] = qk.astype(out_dtype)
    mid_ref[:, _M_V:_M_V + 256] = p["v"].astype(out_dtype)
    mid_ref[:, _M_OG:_M_OG + 256] = (_sigmoid(p["o"]) * _sigmoid(p["gb"])).astype(out_dtype)


def _inproj_prompt_kernel(x_ref, sc_ref, sh_ref, g_ref, wxc, wbg, wcg, wq, wk, wv, wo, wga, wgb, wg_ref,
                          bg_ref, cw_ref, cb_ref, cprev_ref, mid_ref, gt_ref, nconv_ref, h_scr, ubuf, carry,
                          *, tiles_per_seq):
    i = pl.program_id(0)
    s = pl.program_id(1)
    tm = x_ref.shape[0]

    @pl.when(s == 0)
    def _():
        _inproj_common(x_ref, sc_ref[0], sh_ref[0], g_ref, wg_ref, bg_ref, gt_ref, h_scr, CHUNK)

    p = _inproj_project(h_scr[...], (wxc, wbg, wcg, wq, wk, wv, wo, wga, wgb))
    u = p["u"]
    seq_start = (i % tiles_per_seq) == 0

    @pl.when(seq_start)
    def _():
        ubuf[6:8, :] = cprev_ref[0]

    @pl.when(jnp.logical_not(seq_start))
    def _():
        ubuf[6:8, :] = carry[s]

    ubuf[8:8 + tm, :] = u
    cw = cw_ref[0]
    conv = (cb_ref[0] + cw[0:1] * ubuf[6:6 + tm, :] + cw[1:2] * ubuf[7:7 + tm, :] + cw[2:3] * u)
    last2 = u[tm - 2:tm, :]
    carry[s] = last2
    nconv_ref[0] = last2
    _inproj_epilogue(p, conv, mid_ref, _BF)


def _inproj_prompt_call(layer, x, sc, sh, g, w_head, w_tail, w_gate, bgate, conv_w, conv_b, conv_prev,
                        *, seq_len, tm):
    m_rows = x.shape[0]
    n_tiles = m_rows // tm
    tiles_per_seq = seq_len // tm
    n_steps = N_HEADS
    kern = functools.partial(_inproj_prompt_kernel, tiles_per_seq=tiles_per_seq)
    head_specs, tail_specs = _w_in_specs(layer, lambda i, s: s)
    return pl.pallas_call(
        kern,
        grid=(n_tiles, n_steps),
        in_specs=[
            pl.BlockSpec((tm, D_MODEL), lambda i, s: (i, 0)),
            pl.BlockSpec((1, 1, D_MODEL), lambda i, s: (i // tiles_per_seq, 0, 0)),
            pl.BlockSpec((1, 1, D_MODEL), lambda i, s: (i // tiles_per_seq, 0, 0)),
            pl.BlockSpec((1, 1, D_MODEL), lambda i, s: (layer, 0, 0)),
            *head_specs, *tail_specs,
            pl.BlockSpec((1, D_MODEL, 128), lambda i, s: (layer, 0, 0)),
            pl.BlockSpec((1, BIAS_ROWS, CHUNK), lambda i, s: (layer, 0, 0)),
            pl.BlockSpec((1, CONV_W, 256), lambda i, s: (layer, 0, s)),
            pl.BlockSpec((1, 1, 256), lambda i, s: (layer, 0, s)),
            pl.BlockSpec((1, CONV_W - 1, 256), lambda i, s: (i // tiles_per_seq, 0, s)),
        ],
        out_specs=[
            pl.BlockSpec((tm, HEAD_COLS), lambda i, s: (i, s)),
            pl.BlockSpec((tm // CHUNK, GATE_ROWS, CHUNK), lambda i, s: (i, 0, 0)),
            pl.BlockSpec((1, CONV_W - 1, 256), lambda i, s: (i, 0, s)),
        ],
        out_shape=[
            jax.ShapeDtypeStruct((m_rows, N_HEADS * HEAD_COLS), _BF),
            jax.ShapeDtypeStruct((m_rows // CHUNK, GATE_ROWS, CHUNK), _F32),
            jax.ShapeDtypeStruct((n_tiles, CONV_W - 1, D_MODEL), _F32),
        ],
        scratch_shapes=[
            pltpu.VMEM((tm, D_MODEL), _BF),
            pltpu.VMEM((tm + 8, 256), _F32),
            pltpu.VMEM((n_steps, CONV_W - 1, 256), _F32),
        ],
        compiler_params=_params(2),
        name="inproj_prompt",
    )(x, sc, sh, g, *([w_head] * 7), *([w_tail] * 2), w_gate, bgate, conv_w, conv_b, conv_prev)


def _inproj_sample_kernel(x_ref, sc_ref, sh_ref, g_ref, wxc, wbg, wcg, wq, wk, wv, wo, wga, wgb, wg_ref,
                          bg_ref, cw_ref, cb_ref, s1_ref, s2_ref, mid_ref, gt_ref, u_ref, h_scr, ubuf,
                          *, seq_len):
    s = pl.program_id(0)
    tm = x_ref.shape[0]

    @pl.when(s == 0)
    def _():
        _inproj_common(x_ref, sc_ref[...], sh_ref[...], g_ref, wg_ref, bg_ref, gt_ref, h_scr, seq_len)

    p = _inproj_project(h_scr[...], (wxc, wbg, wcg, wq, wk, wv, wo, wga, wgb))
    u = p["u"]
    ubuf[0:8, :] = jnp.zeros((8, 256), _F32)
    ubuf[8:8 + tm, :] = u
    t_in_seq = lax.broadcasted_iota(jnp.int32, (tm, 256), 0) % seq_len
    p1 = jnp.where(t_in_seq >= 1, ubuf[7:7 + tm, :], s1_ref[...])
    p2 = jnp.where(t_in_seq >= 2, ubuf[6:6 + tm, :], s2_ref[...])
    cw = cw_ref[0]
    conv = cb_ref[0] + cw[0:1] * p2 + cw[1:2] * p1 + cw[2:3] * u
    u_ref[...] = u
    _inproj_epilogue(p, conv, mid_ref, _F32)


def _inproj_sample_call(layer, x, sc, sh, g, w_head, w_tail, w_gate, bgate, conv_w, conv_b, s1, s2, *, seq_len):
    tm = x.shape[0]
    kern = functools.partial(_inproj_sample_kernel, seq_len=seq_len)
    head_specs, tail_specs = _w_in_specs(layer, lambda s: s)
    return pl.pallas_call(
        kern,
        grid=(N_HEADS,),
        in_specs=[
            pl.BlockSpec((tm, D_MODEL), lambda s: (0, 0)),
            pl.BlockSpec((tm, D_MODEL), lambda s: (0, 0)),
            pl.BlockSpec((tm, D_MODEL), lambda s: (0, 0)),
            pl.BlockSpec((1, 1, D_MODEL), lambda s: (layer, 0, 0)),
            *head_specs, *tail_specs,
            pl.BlockSpec((1, D_MODEL, 128), lambda s: (layer, 0, 0)),
            pl.BlockSpec((1, BIAS_ROWS, CHUNK), lambda s: (layer, 0, 0)),
            pl.BlockSpec((1, CONV_W, 256), lambda s: (layer, 0, s)),
            pl.BlockSpec((1, 1, 256), lambda s: (layer, 0, s)),
            pl.BlockSpec((tm, 256), lambda s: (0, s)),
            pl.BlockSpec((tm, 256), lambda s: (0, s)),
        ],
        out_specs=[
            pl.BlockSpec((tm, HEAD_COLS), lambda s: (0, s)),
            pl.BlockSpec((tm // CHUNK, GATE_ROWS, CHUNK), lambda s: (0, 0, 0)),
            pl.BlockSpec((tm, 256), lambda s: (0, s)),
        ],
        out_shape=[
            jax.ShapeDtypeStruct((tm, N_HEADS * HEAD_COLS), _F32),
            jax.ShapeDtypeStruct((tm // CHUNK, GATE_ROWS, CHUNK), _F32),
            jax.ShapeDtypeStruct((tm, D_MODEL), _F32),
        ],
        scratch_shapes=[
            pltpu.VMEM((tm, D_MODEL), _BF),
            pltpu.VMEM((tm + 8, 256), _F32),
        ],
        compiler_params=_params(1),
        name="inproj_sample",
    )(x, sc, sh, g, *([w_head] * 7), *([w_tail] * 2), w_gate, bgate, conv_w, conv_b, s1, s2)


def _causal_mask(lq):
    row_id = lax.broadcasted_iota(jnp.int32, (lq, CHUNK), 0)
    col_id = lax.broadcasted_iota(jnp.int32, (lq, CHUNK), 1)
    return col_id <= row_id


def _chunk_local(a, causal):
    dmat = jnp.where(causal, jnp.exp(a["g_row"] - a["m_col"]), 0.0)
    s_bf = (_dot_nt(a["q"], a["k"]) * dmat).astype(_BF)
    kw = a["k"].astype(_F32) * a["ws_col"]
    return s_bf, _dot(kw.T.astype(_BF), a["v_ext"])


def _chunk_output(a, s_bf, d_state, cx_ref, h, hng_h):
    q_inter = (a["q"].astype(_F32) * a["inter_col"]).astype(_BF)
    cx = cx_ref[h]
    rhs = jnp.concatenate([a["v_ext"], cx.astype(_BF)], axis=0)
    res = _dot(jnp.concatenate([s_bf, q_inter], axis=1), rhs)
    cx_ref[h] = a["dec"] * cx + d_state
    num = res[:, :DV]
    rden = 1.0 / jnp.maximum(jnp.abs(res[:, DV:]), a["em_col"])
    sq = jnp.sum(num * num, axis=-1, keepdims=True) * (1.0 / DV)
    scale = rden * lax.rsqrt(rden * rden * sq + EPS)
    hn = num * jnp.concatenate([scale, scale], axis=1) * hng_h
    return a["ya"] + a["og"] * hn


def _head_step_small(q, k, v, ya, og, g_row, m_col, inter_col, em_col, ws_col, dec, c_ref, n_ref, h, hng, causal):
    dmat = jnp.where(causal, jnp.exp(g_row - m_col), 0.0)
    smat = _dot_nt(q, k) * dmat
    c_old = c_ref[h]
    n_old = n_ref[h:h + 1, :]
    num = _dot(smat.astype(_BF), v) + inter_col * _dot(q, c_old.astype(_BF))
    den = (jnp.sum(smat, axis=-1, keepdims=True)
           + inter_col * jnp.sum(q.astype(_F32) * n_old, axis=-1, keepdims=True))
    hraw = num * (1.0 / jnp.maximum(jnp.abs(den), em_col))
    ms = jnp.mean(hraw * hraw, axis=-1, keepdims=True)
    hn = hraw * lax.rsqrt(ms + EPS) * hng[:, h * DV:(h + 1) * DV]
    kw = k.astype(_F32) * ws_col
    c_ref[h] = dec * c_old + _dot(kw.T.astype(_BF), v)
    n_ref[h:h + 1, :] = dec * n_old + jnp.sum(kw, axis=0, keepdims=True)
    return ya + og * hn


def _mlstm_prompt_kernel(mid_ref, gt_ref, hng_ref, c0_ref, n0_ref, m0_ref, cbuf_ref,
                         mg_ref, c_ref, n_ref, m_ref, cx_ref):
    del cbuf_ref
    t = pl.program_id(1)

    @pl.when(t == 0)
    def _():
        m_ref[...] = m0_ref[...]
        for h in range(N_HEADS):
            cx_ref[h, :, 0:DV] = c0_ref[0, h]
            cx_ref[h, :, DV:EXT_COLS] = jnp.broadcast_to(n0_ref[0, h:h + 1, :], (CHUNK, DK)).T

    n_chunks = mid_ref.shape[0] // CHUNK
    rows = n_chunks * 8
    hng = hng_ref[0]
    gates = gt_ref[...]
    b = gates[:, 0:8, :].reshape(rows, CHUNK)
    g = gates[:, 8:16, :].reshape(rows, CHUNK)
    cm = gates[:, 16:24, :].reshape(rows, CHUNK)
    b_last = jnp.broadcast_to(b[:, CHUNK - 1:CHUNK], (rows, CHUNK))
    cm_last = jnp.broadcast_to(cm[:, CHUNK - 1:CHUNK], (rows, CHUNK))

    m_prev = m_ref[0]
    m_prevs = []
    for c in range(n_chunks):
        m_prevs.append(m_prev)
        m_prev = b_last[c * 8:(c + 1) * 8] + jnp.maximum(cm_last[c * 8:(c + 1) * 8], m_prev)
    m_ref[0] = m_prev
    m_prev_all = jnp.concatenate(m_prevs, axis=0)

    m_run = jnp.maximum(cm, m_prev_all)
    inter = jnp.exp(m_prev_all - m_run)
    em = jnp.exp(-(b + m_run))
    m_last = jnp.maximum(cm_last, m_prev_all)
    ws = jnp.exp(g - m_last)
    decay = jnp.exp(m_prev_all - m_last)

    causal = _causal_mask(CHUNK)
    ones = jnp.ones((CHUNK, CHUNK), _BF)
    group = CHUNK // 32
    cols_of = []
    for c0 in range(0, n_chunks, group):
        pieces = []
        for c in range(c0, min(c0 + group, n_chunks)):
            sl = slice(c * 8, (c + 1) * 8)
            pieces += [m_run[sl], inter[sl], em[sl], ws[sl]]
        if len(pieces) * 8 < CHUNK:
            pieces.append(jnp.zeros((CHUNK - len(pieces) * 8, CHUNK), _F32))
        cols_of.append(jnp.concatenate(pieces, axis=0).T)

    def chunk_inputs(c):
        cols = cols_of[c // group]
        base = (c % group) * 32
        r0 = c * CHUNK
        heads = []
        for h in range(N_HEADS):
            mb = h * HEAD_COLS
            row = c * 8 + h
            v = mid_ref[r0:r0 + CHUNK, mb + _M_V:mb + _M_V + 256]
            heads.append(dict(
                q=mid_ref[r0:r0 + CHUNK, mb + _M_Q:mb + _M_Q + 128],
                k=mid_ref[r0:r0 + CHUNK, mb + _M_K:mb + _M_K + 128],
                v_ext=jnp.concatenate([v, ones], axis=1),
                ya=mid_ref[r0:r0 + CHUNK, mb + _M_YA:mb + _M_YA + 256].astype(_F32),
                og=mid_ref[r0:r0 + CHUNK, mb + _M_OG:mb + _M_OG + 256].astype(_F32),
                g_row=g[row:row + 1, :],
                m_col=cols[:, base + h:base + h + 1],
                inter_col=cols[:, base + 8 + h:base + 9 + h],
                em_col=cols[:, base + 16 + h:base + 17 + h],
                ws_col=cols[:, base + 24 + h:base + 25 + h],
                dec=decay[row:row + 1, 0:1]))
        return heads

    for c in range(n_chunks):
        outs = []
        for h, a in enumerate(chunk_inputs(c)):
            s_bf, d_state = _chunk_local(a, causal)
            outs.append(_chunk_output(a, s_bf, d_state, cx_ref, h, hng[:, h * DV:(h + 1) * DV]))
        mg_ref[c * CHUNK:(c + 1) * CHUNK, :] = jnp.concatenate(outs, axis=1).astype(_BF)

    @pl.when(t == pl.num_programs(1) - 1)
    def _():
        for h in range(N_HEADS):
            c_ref[0, 0, h] = cx_ref[h, :, 0:DV]
            n_ref[0, h:h + 1, :] = cx_ref[h, :, DV:EXT_COLS].T[0:1, :]


def _mlstm_prompt_call(layer, mid, gates_t, hng, c0, n0, m0, c_buf, *, seq_len, tb):
    m_rows = mid.shape[0]
    n_seq = m_rows // seq_len
    steps = seq_len // tb
    aliases = {} if c_buf is None else {6: 1}
    if c_buf is None:
        c_buf = jnp.zeros((1,), _F32)
    return pl.pallas_call(
        _mlstm_prompt_kernel,
        grid=(n_seq, steps),
        in_specs=[
            pl.BlockSpec((tb, N_HEADS * HEAD_COLS), lambda b, t: (b * steps + t, 0)),
            pl.BlockSpec((tb // CHUNK, GATE_ROWS, CHUNK), lambda b, t: (b * steps + t, 0, 0)),
            pl.BlockSpec((1, 1, D_MODEL), lambda b, t: (layer, 0, 0)),
            pl.BlockSpec((1, N_HEADS, DK, DV), lambda b, t: (b, 0, 0, 0)),
            pl.BlockSpec((1, N_HEADS, DK), lambda b, t: (b, 0, 0)),
            pl.BlockSpec((1, 8, CHUNK), lambda b, t: (b, 0, 0)),
            pl.BlockSpec(memory_space=pl.ANY),
        ],
        out_specs=[
            pl.BlockSpec((tb, D_MODEL), lambda b, t: (b * steps + t, 0)),
            pl.BlockSpec((1, 1, N_HEADS, DK, DV), lambda b, t: (layer, b, 0, 0, 0)),
            pl.BlockSpec((1, N_HEADS, DK), lambda b, t: (b, 0, 0)),
            pl.BlockSpec((1, 8, CHUNK), lambda b, t: (b, 0, 0)),
        ],
        out_shape=[
            jax.ShapeDtypeStruct((m_rows, D_MODEL), _BF),
            jax.ShapeDtypeStruct((DEPTH, n_seq, N_HEADS, DK, DV), _F32),
            jax.ShapeDtypeStruct((n_seq, N_HEADS, DK), _F32),
            jax.ShapeDtypeStruct((n_seq, 8, CHUNK), _F32),
        ],
        scratch_shapes=[pltpu.VMEM((N_HEADS, DK, EXT_COLS), _F32)],
        input_output_aliases=aliases,
        compiler_params=_params(2),
        name="mlstm_prompt",
    )(mid, gates_t, hng, c0, n0, m0, c_buf)


def _mlstm_sample_kernel(mid_ref, gt_ref, hng_ref, c0_ref, n0_ref, m0_ref, cbuf_ref,
                         mg_ref, c_ref, n_ref, m_ref, rows, *, seq_len):
    del cbuf_ref
    bb = mid_ref.shape[0]
    c_ref[...] = c0_ref[...]
    n_ref[...] = n0_ref[...]
    rows[...] = jnp.zeros(rows.shape, _F32)
    hng = hng_ref[0]
    causal = _causal_mask(8)
    valid = lax.broadcasted_iota(jnp.int32, (8, CHUNK), 1) < seq_len

    def body(i, carry):
        rows[0:seq_len, :] = mid_ref[i]
        gates = gt_ref[i]
        m_prev = m0_ref[i]
        b = gates[0:8]
        g = jnp.where(valid, gates[8:16], NEG_BIG)
        cm = jnp.where(valid, gates[16:24], NEG_BIG)
        m_run = jnp.maximum(cm, m_prev)
        inter = jnp.exp(m_prev - m_run)
        em = jnp.exp(-(b + m_run))
        m_last = jnp.max(m_run, axis=1, keepdims=True)
        ws = jnp.exp(g - m_last)
        decay = jnp.exp(m_prev - m_last)
        m_new = b[:, seq_len - 1:seq_len] + m_last
        cols = jnp.concatenate([m_run, inter, em, ws, jnp.zeros((CHUNK - 32, CHUNK), _F32)], axis=0).T
        outs = []
        for h in range(N_HEADS):
            mb = h * HEAD_COLS
            q = rows[0:8, mb + _M_Q:mb + _M_Q + 128].astype(_BF)
            k = rows[:, mb + _M_K:mb + _M_K + 128].astype(_BF)
            v = rows[:, mb + _M_V:mb + _M_V + 256].astype(_BF)
            ya = rows[0:8, mb + _M_YA:mb + _M_YA + 256]
            og = rows[0:8, mb + _M_OG:mb + _M_OG + 256]
            outs.append(_head_step_small(
                q, k, v, ya, og, g[h:h + 1, :],
                cols[0:8, h:h + 1], cols[0:8, 8 + h:9 + h], cols[0:8, 16 + h:17 + h],
                cols[:, 24 + h:25 + h], decay[h:h + 1, 0:1], c_ref.at[0, i], n_ref.at[i], h, hng, causal))
        mg_ref[i] = jnp.concatenate(outs, axis=1)[0:seq_len, :]
        m_ref[i] = jnp.broadcast_to(m_new, (8, CHUNK))
        return carry

    lax.fori_loop(0, bb, body, 0)


def _mlstm_sample_call(layer, mid3, gates, hng, c0, n0, m0, c_buf, *, bb):
    n_seq, seq_len, _ = mid3.shape
    kern = functools.partial(_mlstm_sample_kernel, seq_len=seq_len)
    aliases = {} if c_buf is None else {6: 1}
    if c_buf is None:
        c_buf = jnp.zeros((1,), _F32)
    return pl.pallas_call(
        kern,
        grid=(n_seq // bb,),
        in_specs=[
            pl.BlockSpec((bb, seq_len, N_HEADS * HEAD_COLS), lambda i: (i, 0, 0)),
            pl.BlockSpec((bb, GATE_ROWS, CHUNK), lambda i: (i, 0, 0)),
            pl.BlockSpec((1, 1, D_MODEL), lambda i: (layer, 0, 0)),
            pl.BlockSpec((1, bb, N_HEADS, DK, DV), lambda i: (layer, i, 0, 0, 0)),
            pl.BlockSpec((bb, N_HEADS, DK), lambda i: (i, 0, 0)),
            pl.BlockSpec((bb, 8, CHUNK), lambda i: (i, 0, 0)),
            pl.BlockSpec(memory_space=pl.ANY),
        ],
        out_specs=[
            pl.BlockSpec((bb, seq_len, D_MODEL), lambda i: (i, 0, 0)),
            pl.BlockSpec((1, bb, N_HEADS, DK, DV), lambda i: (layer, i, 0, 0, 0)),
            pl.BlockSpec((bb, N_HEADS, DK), lambda i: (i, 0, 0)),
            pl.BlockSpec((bb, 8, CHUNK), lambda i: (i, 0, 0)),
        ],
        out_shape=[
            jax.ShapeDtypeStruct((n_seq, seq_len, D_MODEL), _F32),
            jax.ShapeDtypeStruct((DEPTH, n_seq, N_HEADS, DK, DV), _F32),
            jax.ShapeDtypeStruct((n_seq, N_HEADS, DK), _F32),
            jax.ShapeDtypeStruct((n_seq, 8, CHUNK), _F32),
        ],
        scratch_shapes=[pltpu.VMEM((CHUNK, N_HEADS * HEAD_COLS), _F32)],
        input_output_aliases=aliases,
        compiler_params=_params(1),
        name="mlstm_sample",
    )(mid3, gates, hng, c0, n0, m0, c_buf)


def _mlp_kernel(x_ref, mg_ref, gt1_ref, sc2_ref, sh2_ref, gt2_ref, g2_ref, wo_ref, wu_ref, wd_ref, gf_ref,
                o_ref, xmid, h2, acc, *, per_seq_mod, final_norm):
    f = pl.program_id(1)

    def mod(ref):
        return ref[0] if per_seq_mod else ref[...]

    @pl.when(f == 0)
    def _():
        mix = _dot(mg_ref[...].astype(_BF), wo_ref[0])
        xm = x_ref[...] + mod(gt1_ref) * mix
        xmid[...] = xm
        h2[...] = _rms_mod(xm, g2_ref[0], mod(sc2_ref), mod(sh2_ref)).astype(_BF)
        acc[...] = jnp.zeros(acc.shape, _F32)

    a = jnp.maximum(_dot(h2[...], wu_ref[0]), 0.0)
    acc[...] += _dot((a * a).astype(_BF), wd_ref[0])

    @pl.when(f == pl.num_programs(1) - 1)
    def _():
        y = xmid[...] + mod(gt2_ref) * acc[...]
        if final_norm:
            ms = jnp.mean(y * y, axis=-1, keepdims=True)
            y = y * lax.rsqrt(ms + EPS) * gf_ref[...]
        o_ref[...] = y


def _mlp_call(layer, x, merged, gt1, sc2, sh2, gt2, g2, w_out, w_up, w_down, g_final, *,
              tm, tf, seq_len, final_norm):
    m_rows = x.shape[0]
    per_seq_mod = seq_len is not None
    if per_seq_mod:
        tiles_per_seq = seq_len // tm
        mod_spec = pl.BlockSpec((1, 1, D_MODEL), lambda i, f: (i // tiles_per_seq, 0, 0))
    else:
        mod_spec = pl.BlockSpec((tm, D_MODEL), lambda i, f: (i, 0))
    kern = functools.partial(_mlp_kernel, per_seq_mod=per_seq_mod, final_norm=final_norm)
    return pl.pallas_call(
        kern,
        grid=(m_rows // tm, D_FF // tf),
        in_specs=[
            pl.BlockSpec((tm, D_MODEL), lambda i, f: (i, 0)),
            pl.BlockSpec((tm, D_MODEL), lambda i, f: (i, 0)),
            mod_spec, mod_spec, mod_spec, mod_spec,
            pl.BlockSpec((1, 1, D_MODEL), lambda i, f: (layer, 0, 0)),
            pl.BlockSpec((1, D_MODEL, D_MODEL), lambda i, f: (layer, 0, 0)),
            pl.BlockSpec((1, D_MODEL, tf), lambda i, f: (layer, 0, f)),
            pl.BlockSpec((1, tf, D_MODEL), lambda i, f: (layer, f, 0)),
            pl.BlockSpec((1, D_MODEL), lambda i, f: (0, 0)),
        ],
        out_specs=pl.BlockSpec((tm, D_MODEL), lambda i, f: (i, 0)),
        out_shape=jax.ShapeDtypeStruct((m_rows, D_MODEL), _F32),
        scratch_shapes=[
            pltpu.VMEM((tm, D_MODEL), _F32),
            pltpu.VMEM((tm, D_MODEL), _BF),
            pltpu.VMEM((tm, D_MODEL), _F32),
        ],
        compiler_params=_params(2),
        name="outproj_mlp",
    )(x, merged, gt1, sc2, sh2, gt2, g2, w_out, w_up, w_down, g_final)


def kernel(x_prompt, x_sample, state_conv, state_C, state_n, state_m, c_prompt, c_sample,
           w_ada, b_ada, g_norm1, g_norm2, w_in, b_gate, conv_w, conv_b, hn_g, w_out, w_up, w_down, g_final):
    n_p, seq_p, _ = x_prompt.shape
    n_s, seq_s, _ = x_sample.shape
    rows_s = n_s * seq_s
    tm_p = 512

    w_head = w_in[:, :, :_HEAD_W].astype(_BF)
    w_tail = w_in[:, :, _OFF_GA:].astype(_BF)
    w_ig = w_in[:, :, _OFF_IG:_OFF_IG + N_HEADS]
    w_fg = w_in[:, :, _OFF_IG + N_HEADS:_OFF_IG + 2 * N_HEADS]
    zpad = jnp.zeros((DEPTH, D_MODEL, 4), _F32)
    w_gate = jnp.concatenate([w_ig, zpad, w_fg, zpad, jnp.zeros((DEPTH, D_MODEL, 112), _F32)], axis=-1).astype(_BF)
    w_out_b = w_out.astype(_BF)
    w_up_b = w_up.astype(_BF)
    w_down_b = w_down.astype(_BF)
    zb = jnp.zeros((DEPTH, 4), _F32)
    bgate = jnp.broadcast_to(
        jnp.concatenate([b_gate[:, :N_HEADS], zb, b_gate[:, N_HEADS:], zb], axis=-1)[:, :, None],
        (DEPTH, BIAS_ROWS, CHUNK))
    g1 = g_norm1.reshape(DEPTH, 1, D_MODEL)
    g2 = g_norm2.reshape(DEPTH, 1, D_MODEL)
    hng = hn_g.reshape(DEPTH, 1, D_MODEL)
    cb = conv_b.reshape(DEPTH, 1, D_MODEL)
    gfin = g_final.reshape(1, D_MODEL)

    mod = _ada_call(jnp.concatenate([c_prompt, c_sample], axis=0), w_ada, b_ada)
    mod_p = mod[:, :n_p].reshape(DEPTH, n_p, 6, 1, D_MODEL)
    mod_s = jnp.repeat(mod[:, n_p:], seq_s, axis=1).reshape(DEPTH, rows_s, 6, D_MODEL)

    xp = x_prompt.reshape(n_p * seq_p, D_MODEL)
    xs = x_sample.reshape(rows_s, D_MODEL)

    zeros_conv = jnp.zeros((n_p, CONV_W - 1, D_MODEL), _F32)
    zeros_c = jnp.zeros((n_p, N_HEADS, DK, DV), _F32)
    zeros_n = jnp.zeros((n_p, N_HEADS, DK), _F32)
    zeros_m = jnp.zeros((n_p, 8, CHUNK), _F32)
    m_s_in = jnp.broadcast_to(
        jnp.pad(state_m, ((0, 0), (0, 0), (0, 8 - N_HEADS)))[..., None], (DEPTH, n_s, 8, CHUNK))

    p_conv, p_n, p_m, s_conv, s_n, s_m = [], [], [], [], [], []
    p_c = s_c = None
    for l in range(DEPTH):
        final = l == DEPTH - 1

        sh1, sc1, gt1, sh2, sc2, gt2 = (mod_p[l, :, j] for j in range(6))
        mid, gates_t, nconv = _inproj_prompt_call(
            l, xp, sc1, sh1, g1, w_head, w_tail, w_gate, bgate, conv_w, cb, zeros_conv, seq_len=seq_p, tm=tm_p)
        merged, p_c, n1, m1 = _mlstm_prompt_call(
            l, mid, gates_t, hng, zeros_c, zeros_n, zeros_m, p_c, seq_len=seq_p, tb=1024)
        xp = _mlp_call(l, xp, merged, gt1, sc2, sh2, gt2, g2, w_out_b, w_up_b, w_down_b, gfin,
                       tm=512, tf=1024, seq_len=seq_p, final_norm=final)
        tiles_per_seq = seq_p // tm_p
        p_conv.append(nconv[tiles_per_seq - 1::tiles_per_seq]); p_n.append(n1); p_m.append(m1[:, :N_HEADS, 0])

        sh1, sc1, gt1, sh2, sc2, gt2 = (mod_s[l, :, j] for j in range(6))
        prev = state_conv[l]
        zrow = jnp.zeros((n_s, 1, D_MODEL), _F32)
        s1 = jnp.concatenate([prev[:, 1:2], zrow, zrow, zrow], axis=1).reshape(rows_s, D_MODEL)
        s2 = jnp.concatenate([prev[:, 0:1], prev[:, 1:2], zrow, zrow], axis=1).reshape(rows_s, D_MODEL)
        mid, gates_t, u_all = _inproj_sample_call(
            l, xs, sc1, sh1, g1, w_head, w_tail, w_gate, bgate, conv_w, cb, s1, s2, seq_len=seq_s)
        gates = gates_t.reshape(rows_s // CHUNK, GATE_ROWS, CHUNK // seq_s, seq_s).transpose(0, 2, 1, 3)
        gates = jnp.pad(gates.reshape(n_s, GATE_ROWS, seq_s), ((0, 0), (0, 0), (0, CHUNK - seq_s)))
        merged3, s_c, n1, m1 = _mlstm_sample_call(
            l, mid.reshape(n_s, seq_s, N_HEADS * HEAD_COLS), gates, hng,
            state_C, state_n[l], m_s_in[l], s_c, bb=8)
        xs = _mlp_call(l, xs, merged3.reshape(rows_s, D_MODEL), gt1, sc2, sh2, gt2, g2,
                       w_out_b, w_up_b, w_down_b, gfin,
                       tm=rows_s, tf=1024, seq_len=None, final_norm=final)
        s_conv.append(u_all.reshape(n_s, seq_s, D_MODEL)[:, seq_s - (CONV_W - 1):])
        s_n.append(n1); s_m.append(m1[:, :N_HEADS, 0])

    return (xp.reshape(n_p, seq_p, D_MODEL), xs.reshape(n_s, seq_s, D_MODEL),
            jnp.stack(p_conv), p_c, jnp.stack(p_n), jnp.stack(p_m),
            jnp.stack(s_conv), s_c, jnp.stack(s_n), jnp.stack(s_m))
```

```python
import functools

import jax
import jax.numpy as jnp
from jax import lax
from jax.experimental import pallas as pl
from jax.experimental.pallas import tpu as pltpu

D_MODEL = 1024
N_HEADS = 4
DK = 128
DV = 256
D_FF = 4096
DEPTH = 4
CONV_W = 3
GATE_CAP = 15.0
EPS = 1e-6
CHUNK = 128
HEAD_COLS = 1024
GATE_COLS = 256
GATE_F_COL = 128
BIAS_ROWS = 16
GATE_ROWS = 24
EXT_COLS = DV + CHUNK
VMEM_LIMIT = 56 * 1024 * 1024

_OFF_XC, _OFF_BG, _OFF_CG = 0, 1024, 2048
_OFF_Q, _OFF_K, _OFF_V, _OFF_O = 3072, 3584, 4096, 5120
_OFF_IG, _OFF_GA, _OFF_GB = 6144, 6152, 7176
_HEAD_W = _OFF_IG

_M_YA, _M_Q, _M_K, _M_V, _M_OG = 0, 256, 384, 512, 768

_BF = jnp.bfloat16
_F32 = jnp.float32


def _dot(a, b):
    return jnp.dot(a, b, preferred_element_type=_F32)


def _dot_nt(a, b):
    return lax.dot_general(a, b, (((1,), (1,)), ((), ())), preferred_element_type=_F32)


def _sigmoid(x):
    return 1.0 / (1.0 + jnp.exp(-x))


def _rms_mod(x, g, sc, sh):
    ms = jnp.mean(x * x, axis=-1, keepdims=True)
    return (x * lax.rsqrt(ms + EPS) * g) * (1.0 + sc) + sh


def _rep_rows(v, reps):
    return v if reps == 1 else jnp.concatenate([v] * reps, axis=0)


def _params(n_axes):
    return pltpu.CompilerParams(dimension_semantics=("arbitrary",) * n_axes, vmem_limit_bytes=VMEM_LIMIT)


def _ada_kernel(c_ref, w_ref, b_ref, o_ref):
    c = c_ref[...]
    a = (c * _sigmoid(c)).astype(_BF)
    o_ref[0] = _dot(a, w_ref[0].astype(_BF)) + b_ref[0]


def _ada_call(c_all, w_ada, b_ada):
    n_rows = c_all.shape[0]
    tn = 1024
    return pl.pallas_call(
        _ada_kernel,
        grid=(DEPTH, 6 * D_MODEL // tn),
        in_specs=[
            pl.BlockSpec((n_rows, D_MODEL), lambda l, j: (0, 0)),
            pl.BlockSpec((1, D_MODEL, tn), lambda l, j: (l, 0, j)),
            pl.BlockSpec((1, 1, tn), lambda l, j: (l, 0, j)),
        ],
        out_specs=pl.BlockSpec((1, n_rows, tn), lambda l, j: (l, 0, j)),
        out_shape=jax.ShapeDtypeStruct((DEPTH, n_rows, 6 * D_MODEL), _F32),
        compiler_params=_params(2),
        name="ada_mod",
    )(c_all, w_ada, b_ada.reshape(DEPTH, 1, 6 * D_MODEL))


def _w_in_specs(layer, idx):
    def spec(width, first_block):
        return pl.BlockSpec((1, D_MODEL, width), lambda *g: (layer, 0, first_block + idx(*g)))
    head = [spec(256, _OFF_XC // 256), spec(256, _OFF_BG // 256), spec(256, _OFF_CG // 256),
            spec(128, _OFF_Q // 128), spec(128, _OFF_K // 128), spec(256, _OFF_V // 256),
            spec(256, _OFF_O // 256)]
    tail = [spec(256, 0), spec(256, (_OFF_GB - _OFF_GA) // 256)]
    return head, tail


def _scan_lanes(x, combine, identity, seg_len):
    pos = lax.broadcasted_iota(jnp.int32, x.shape, 1) % seg_len
    shift = 1
    while shift < seg_len:
        x = combine(x, jnp.where(pos >= shift, pltpu.roll(x, shift, 1), identity))
        shift *= 2
    return x


def _softcap(a):
    return GATE_CAP * jnp.tanh(a / GATE_CAP)


def _log_sigmoid(x):
    return -(jnp.maximum(-x, 0.0) + jnp.log(1.0 + jnp.exp(-jnp.abs(x))))


def _gate_rows_prompt(h, wg_ref, bg_ref, gt_ref):
    gates_t = _dot(h, wg_ref[0]).T
    n_chunks = gt_ref.shape[0]
    bias = bg_ref[0]

    def stack(r0):
        return jnp.concatenate([gates_t[r0:r0 + 8, c * CHUNK:(c + 1) * CHUNK] for c in range(n_chunks)], axis=0)

    i_pre = _softcap(stack(0) + jnp.concatenate([bias[0:8]] * n_chunks, axis=0))
    f_pre = _softcap(stack(GATE_F_COL) + jnp.concatenate([bias[8:16]] * n_chunks, axis=0))
    b = _scan_lanes(_log_sigmoid(f_pre), jnp.add, 0.0, CHUNK)
    g = i_pre - b
    cm = _scan_lanes(g, jnp.maximum, -3e38, CHUNK)
    for c in range(n_chunks):
        gt_ref[c, 0:8] = b[c * 8:(c + 1) * 8]
        gt_ref[c, 8:16] = g[c * 8:(c + 1) * 8]
        gt_ref[c, 16:24] = cm[c * 8:(c + 1) * 8]


def _inproj_project(h, w_refs):
    wxc, wbg, wcg, wq, wk, wv, wo, wga, wgb = w_refs
    xc = _dot(h, wxc[0])
    cg = _dot(h, wcg[0])
    qk = _dot(h, jnp.concatenate([wq[0], wk[0]], axis=1))
    return dict(u=cg * xc, bg=_dot(h, wbg[0]), ga=_dot(h, wga[0]), qk=qk,
                v=_dot(h, wv[0]), o=_dot(h, wo[0]), gb=_dot(h, wgb[0]))


def _inproj_epilogue(p, conv, mid_ref, out_dtype):
    lane = lax.broadcasted_iota(jnp.int32, p["qk"].shape, 1)
    qk = jnp.where(lane < DK, p["qk"], p["qk"] * (DK ** -0.5))
    mid_ref[:, _M_YA:_M_YA + 256] = (_sigmoid(p["ga"]) * (p["bg"] * conv)).astype(out_dtype)
    mid_ref[:, _M_Q:_M_Q + 256] = qk.astype(out_dtype)
    mid_ref[:, _M_V:_M_V + 256] = p["v"].astype(out_dtype)
    mid_ref[:, _M_OG:_M_OG + 256] = (_sigmoid(p["o"]) * _sigmoid(p["gb"])).astype(out_dtype)


def _inproj_prompt_kernel(x_ref, sc_ref, sh_ref, g_ref, wxc, wbg, wcg, wq, wk, wv, wo, wga, wgb, wg_ref,
                          bg_ref, cw_ref, cb_ref, cprev_ref, mid_ref, gt_ref, nconv_ref, h_scr, ubuf, carry,
                          *, tiles_per_seq):
    i = pl.program_id(0)
    s = pl.program_id(1)
    tm = x_ref.shape[0]

    @pl.when(s == 0)
    def _():
        h = _rms_mod(x_ref[...], g_ref[0], sc_ref[0], sh_ref[0]).astype(_BF)
        h_scr[...] = h
        _gate_rows_prompt(h, wg_ref, bg_ref, gt_ref)

    p = _inproj_project(h_scr[...], (wxc, wbg, wcg, wq, wk, wv, wo, wga, wgb))
    u = p["u"]
    seq_start = (i % tiles_per_seq) == 0

    @pl.when(seq_start)
    def _():
        ubuf[6:8, :] = cprev_ref[0]

    @pl.when(jnp.logical_not(seq_start))
    def _():
        ubuf[6:8, :] = carry[s]

    ubuf[8:8 + tm, :] = u
    cw = cw_ref[0]
    conv = (cb_ref[0] + cw[0:1] * ubuf[6:6 + tm, :] + cw[1:2] * ubuf[7:7 + tm, :] + cw[2:3] * u)
    last2 = u[tm - 2:tm, :]
    carry[s] = last2
    nconv_ref[0] = last2
    _inproj_epilogue(p, conv, mid_ref, _BF)


def _inproj_prompt_call(layer, x, sc, sh, g, w_head, w_tail, w_gate, bgate, conv_w, conv_b, conv_prev,
                        *, seq_len, tm):
    m_rows = x.shape[0]
    n_tiles = m_rows // tm
    tiles_per_seq = seq_len // tm
    n_steps = N_HEADS
    kern = functools.partial(_inproj_prompt_kernel, tiles_per_seq=tiles_per_seq)
    head_specs, tail_specs = _w_in_specs(layer, lambda i, s: s)
    return pl.pallas_call(
        kern,
        grid=(n_tiles, n_steps),
        in_specs=[
            pl.BlockSpec((tm, D_MODEL), lambda i, s: (i, 0)),
            pl.BlockSpec((1, 1, D_MODEL), lambda i, s: (i // tiles_per_seq, 0, 0)),
            pl.BlockSpec((1, 1, D_MODEL), lambda i, s: (i // tiles_per_seq, 0, 0)),
            pl.BlockSpec((1, 1, D_MODEL), lambda i, s: (layer, 0, 0)),
            *head_specs, *tail_specs,
            pl.BlockSpec((1, D_MODEL, GATE_COLS), lambda i, s: (layer, 0, 0)),
            pl.BlockSpec((1, BIAS_ROWS, CHUNK), lambda i, s: (layer, 0, 0)),
            pl.BlockSpec((1, CONV_W, 256), lambda i, s: (layer, 0, s)),
            pl.BlockSpec((1, 1, 256), lambda i, s: (layer, 0, s)),
            pl.BlockSpec((1, CONV_W - 1, 256), lambda i, s: (i // tiles_per_seq, 0, s)),
        ],
        out_specs=[
            pl.BlockSpec((tm, HEAD_COLS), lambda i, s: (i, s)),
            pl.BlockSpec((tm // CHUNK, GATE_ROWS, CHUNK), lambda i, s: (i, 0, 0)),
            pl.BlockSpec((1, CONV_W - 1, 256), lambda i, s: (i, 0, s)),
        ],
        out_shape=[
            jax.ShapeDtypeStruct((m_rows, N_HEADS * HEAD_COLS), _BF),
            jax.ShapeDtypeStruct((m_rows // CHUNK, GATE_ROWS, CHUNK), _F32),
            jax.ShapeDtypeStruct((n_tiles, CONV_W - 1, D_MODEL), _F32),
        ],
        scratch_shapes=[
            pltpu.VMEM((tm, D_MODEL), _BF),
            pltpu.VMEM((tm + 8, 256), _F32),
            pltpu.VMEM((n_steps, CONV_W - 1, 256), _F32),
        ],
        compiler_params=_params(2),
        name="inproj_prompt",
    )(x, sc, sh, g, *([w_head] * 7), *([w_tail] * 2), w_gate, bgate, conv_w, conv_b, conv_prev)


def _inproj_sample_kernel(x_ref, sc_ref, sh_ref, g_ref, wxc, wbg, wcg, wq, wk, wv, wo, wga, wgb, wg_ref,
                          cw_ref, cb_ref, cprev_ref, mid_ref, gt_ref, nconv_ref, h_scr, *, seq_len):
    s = pl.program_id(0)
    n_b = sc_ref.shape[0]

    @pl.when(s == 0)
    def _():
        h = _rms_mod(x_ref[...], g_ref[0], _rep_rows(sc_ref[...], seq_len), _rep_rows(sh_ref[...], seq_len))
        h = h.astype(_BF)
        h_scr[...] = h
        gt_ref[...] = _dot(h, wg_ref[0])

    p = _inproj_project(h_scr[...], (wxc, wbg, wcg, wq, wk, wv, wo, wga, wgb))
    u = p["u"]
    prev0 = cprev_ref[0, 0]
    prev1 = cprev_ref[0, 1]
    p1 = jnp.concatenate([prev1, u[0:(seq_len - 1) * n_b]], axis=0)
    p2 = jnp.concatenate([prev0, prev1, u[0:(seq_len - 2) * n_b]], axis=0)
    cw = cw_ref[0]
    conv = cb_ref[0] + cw[0:1] * p2 + cw[1:2] * p1 + cw[2:3] * u
    nconv_ref[0] = u[(seq_len - 2) * n_b:(seq_len - 1) * n_b]
    nconv_ref[1] = u[(seq_len - 1) * n_b:seq_len * n_b]
    _inproj_epilogue(p, conv, mid_ref, _F32)


def _inproj_sample_call(layer, x, sc, sh, g, w_head, w_tail, w_gate, conv_w, conv_b, conv_prev, *, seq_len):
    tm = x.shape[0]
    n_b = tm // seq_len
    kern = functools.partial(_inproj_sample_kernel, seq_len=seq_len)
    head_specs, tail_specs = _w_in_specs(layer, lambda s: s)
    return pl.pallas_call(
        kern,
        grid=(N_HEADS,),
        in_specs=[
            pl.BlockSpec((tm, D_MODEL), lambda s: (0, 0)),
            pl.BlockSpec((n_b, D_MODEL), lambda s: (0, 0)),
            pl.BlockSpec((n_b, D_MODEL), lambda s: (0, 0)),
            pl.BlockSpec((1, 1, D_MODEL), lambda s: (layer, 0, 0)),
            *head_specs, *tail_specs,
            pl.BlockSpec((1, D_MODEL, GATE_COLS), lambda s: (layer, 0, 0)),
            pl.BlockSpec((1, CONV_W, 256), lambda s: (layer, 0, s)),
            pl.BlockSpec((1, 1, 256), lambda s: (layer, 0, s)),
            pl.BlockSpec((1, CONV_W - 1, n_b, 256), lambda s: (layer, 0, 0, s)),
        ],
        out_specs=[
            pl.BlockSpec((tm, HEAD_COLS), lambda s: (0, s)),
            pl.BlockSpec((tm, GATE_COLS), lambda s: (0, 0)),
            pl.BlockSpec((CONV_W - 1, n_b, 256), lambda s: (0, 0, s)),
        ],
        out_shape=[
            jax.ShapeDtypeStruct((tm, N_HEADS * HEAD_COLS), _F32),
            jax.ShapeDtypeStruct((tm, GATE_COLS), _F32),
            jax.ShapeDtypeStruct((CONV_W - 1, n_b, D_MODEL), _F32),
        ],
        scratch_shapes=[pltpu.VMEM((tm, D_MODEL), _BF)],
        compiler_params=_params(1),
        name="inproj_sample",
    )(x, sc, sh, g, *([w_head] * 7), *([w_tail] * 2), w_gate, conv_w, conv_b, conv_prev)


def _causal_mask(lq):
    row_id = lax.broadcasted_iota(jnp.int32, (lq, CHUNK), 0)
    col_id = lax.broadcasted_iota(jnp.int32, (lq, CHUNK), 1)
    return col_id <= row_id


def _chunk_local(a, causal):
    dmat = jnp.where(causal, jnp.exp(a["g_row"] - a["m_col"]), 0.0)
    s_bf = (_dot_nt(a["q"], a["k"]) * dmat).astype(_BF)
    kw = a["k"].astype(_F32) * a["ws_col"]
    return s_bf, _dot(kw.T.astype(_BF), a["v_ext"])


def _chunk_output(a, s_bf, d_state, cx_ref, h, hng_h):
    q_inter = (a["q"].astype(_F32) * a["inter_col"]).astype(_BF)
    cx = cx_ref[h]
    rhs = jnp.concatenate([a["v_ext"], cx.astype(_BF)], axis=0)
    res = _dot(jnp.concatenate([s_bf, q_inter], axis=1), rhs)
    cx_ref[h] = a["dec"] * cx + d_state
    num = res[:, :DV]
    rden = 1.0 / jnp.maximum(jnp.abs(res[:, DV:]), a["em_col"])
    sq = jnp.sum(num * num, axis=-1, keepdims=True) * (1.0 / DV)
    scale = rden * lax.rsqrt(rden * rden * sq + EPS)
    hn = num * jnp.concatenate([scale, scale], axis=1) * hng_h
    return a["ya"] + a["og"] * hn


def _mlstm_prompt_kernel(mid_ref, gt_ref, hng_ref, c0_ref, n0_ref, m0_ref, cbuf_ref,
                         mg_ref, c_ref, n_ref, m_ref, cx_ref):
    del cbuf_ref
    t = pl.program_id(1)

    @pl.when(t == 0)
    def _():
        m_ref[...] = m0_ref[...]
        for h in range(N_HEADS):
            cx_ref[h, :, 0:DV] = c0_ref[0, h]
            cx_ref[h, :, DV:EXT_COLS] = jnp.broadcast_to(n0_ref[0, h:h + 1, :], (CHUNK, DK)).T

    n_chunks = mid_ref.shape[0] // CHUNK
    rows = n_chunks * 8
    hng = hng_ref[0]
    gates = gt_ref[...]
    b = gates[:, 0:8, :].reshape(rows, CHUNK)
    g = gates[:, 8:16, :].reshape(rows, CHUNK)
    cm = gates[:, 16:24, :].reshape(rows, CHUNK)
    b_last = jnp.broadcast_to(b[:, CHUNK - 1:CHUNK], (rows, CHUNK))
    cm_last = jnp.broadcast_to(cm[:, CHUNK - 1:CHUNK], (rows, CHUNK))

    m_prev = m_ref[0]
    m_prevs = []
    for c in range(n_chunks):
        m_prevs.append(m_prev)
        m_prev = b_last[c * 8:(c + 1) * 8] + jnp.maximum(cm_last[c * 8:(c + 1) * 8], m_prev)
    m_ref[0] = m_prev
    m_prev_all = jnp.concatenate(m_prevs, axis=0)

    m_run = jnp.maximum(cm, m_prev_all)
    inter = jnp.exp(m_prev_all - m_run)
    em = jnp.exp(-(b + m_run))
    m_last = jnp.maximum(cm_last, m_prev_all)
    ws = jnp.exp(g - m_last)
    decay = jnp.exp(m_prev_all - m_last)

    causal = _causal_mask(CHUNK)
    ones = jnp.ones((CHUNK, CHUNK), _BF)
    group = CHUNK // 32
    cols_of = []
    for c0 in range(0, n_chunks, group):
        pieces = []
        for c in range(c0, min(c0 + group, n_chunks)):
            sl = slice(c * 8, (c + 1) * 8)
            pieces += [m_run[sl], inter[sl], em[sl], ws[sl]]
        if len(pieces) * 8 < CHUNK:
            pieces.append(jnp.zeros((CHUNK - len(pieces) * 8, CHUNK), _F32))
        cols_of.append(jnp.concatenate(pieces, axis=0).T)

    def chunk_inputs(c):
        cols = cols_of[c // group]
        base = (c % group) * 32
        r0 = c * CHUNK
        heads = []
        for h in range(N_HEADS):
            mb = h * HEAD_COLS
            row = c * 8 + h
            v = mid_ref[r0:r0 + CHUNK, mb + _M_V:mb + _M_V + 256]
            heads.append(dict(
                q=mid_ref[r0:r0 + CHUNK, mb + _M_Q:mb + _M_Q + 128],
                k=mid_ref[r0:r0 + CHUNK, mb + _M_K:mb + _M_K + 128],
                v_ext=jnp.concatenate([v, ones], axis=1),
                ya=mid_ref[r0:r0 + CHUNK, mb + _M_YA:mb + _M_YA + 256].astype(_F32),
                og=mid_ref[r0:r0 + CHUNK, mb + _M_OG:mb + _M_OG + 256].astype(_F32),
                g_row=g[row:row + 1, :],
                m_col=cols[:, base + h:base + h + 1],
                inter_col=cols[:, base + 8 + h:base + 9 + h],
                em_col=cols[:, base + 16 + h:base + 17 + h],
                ws_col=cols[:, base + 24 + h:base + 25 + h],
                dec=decay[row:row + 1, 0:1]))
        return heads

    for c in range(n_chunks):
        outs = []
        for h, a in enumerate(chunk_inputs(c)):
            s_bf, d_state = _chunk_local(a, causal)
            outs.append(_chunk_output(a, s_bf, d_state, cx_ref, h, hng[:, h * DV:(h + 1) * DV]))
        mg_ref[c * CHUNK:(c + 1) * CHUNK, :] = jnp.concatenate(outs, axis=1).astype(_BF)

    @pl.when(t == pl.num_programs(1) - 1)
    def _():
        for h in range(N_HEADS):
            c_ref[0, 0, h] = cx_ref[h, :, 0:DV]
            n_ref[0, h:h + 1, :] = cx_ref[h, :, DV:EXT_COLS].T[0:1, :]


def _mlstm_prompt_call(layer, mid, gates_t, hng, c0, n0, m0, c_buf, *, seq_len, tb):
    m_rows = mid.shape[0]
    n_seq = m_rows // seq_len
    steps = seq_len // tb
    aliases = {} if c_buf is None else {6: 1}
    if c_buf is None:
        c_buf = jnp.zeros((1,), _F32)
    return pl.pallas_call(
        _mlstm_prompt_kernel,
        grid=(n_seq, steps),
        in_specs=[
            pl.BlockSpec((tb, N_HEADS * HEAD_COLS), lambda b, t: (b * steps + t, 0)),
            pl.BlockSpec((tb // CHUNK, GATE_ROWS, CHUNK), lambda b, t: (b * steps + t, 0, 0)),
            pl.BlockSpec((1, 1, D_MODEL), lambda b, t: (layer, 0, 0)),
            pl.BlockSpec((1, N_HEADS, DK, DV), lambda b, t: (b, 0, 0, 0)),
            pl.BlockSpec((1, N_HEADS, DK), lambda b, t: (b, 0, 0)),
            pl.BlockSpec((1, 8, CHUNK), lambda b, t: (b, 0, 0)),
            pl.BlockSpec(memory_space=pl.ANY),
        ],
        out_specs=[
            pl.BlockSpec((tb, D_MODEL), lambda b, t: (b * steps + t, 0)),
            pl.BlockSpec((1, 1, N_HEADS, DK, DV), lambda b, t: (layer, b, 0, 0, 0)),
            pl.BlockSpec((1, N_HEADS, DK), lambda b, t: (b, 0, 0)),
            pl.BlockSpec((1, 8, CHUNK), lambda b, t: (b, 0, 0)),
        ],
        out_shape=[
            jax.ShapeDtypeStruct((m_rows, D_MODEL), _BF),
            jax.ShapeDtypeStruct((DEPTH, n_seq, N_HEADS, DK, DV), _F32),
            jax.ShapeDtypeStruct((n_seq, N_HEADS, DK), _F32),
            jax.ShapeDtypeStruct((n_seq, 8, CHUNK), _F32),
        ],
        scratch_shapes=[pltpu.VMEM((N_HEADS, DK, EXT_COLS), _F32)],
        input_output_aliases=aliases,
        compiler_params=_params(2),
        name="mlstm_prompt",
    )(mid, gates_t, hng, c0, n0, m0, c_buf)


def _mlstm_sample_kernel(mid_ref, gt_ref, bias_ref, hng_ref, c0_ref, n0_ref, m0_ref, cbuf_ref,
                         mg_ref, c_ref, n_ref, m_ref,
                         qs_scr, kk_scr, wv_scr, rs_scr, dec_scr, lhs_q, lhs_k, lhs_wv):
    del cbuf_ref
    seq_len, bb, _ = mid_ref.shape
    hng = hng_ref[0]
    bias = bias_ref[0]
    m_prev = m0_ref[0]

    b_t, g_t, cm_t = [], [], []
    for t in range(seq_len):
        pre = _softcap(gt_ref[t] + bias)
        logf = _log_sigmoid(pre[:, GATE_F_COL:GATE_F_COL + CHUNK])
        b_t.append(logf if t == 0 else b_t[-1] + logf)
        g_t.append(pre[:, 0:CHUNK] - b_t[-1])
        cm_t.append(g_t[-1] if t == 0 else jnp.maximum(cm_t[-1], g_t[-1]))
    m_run = [jnp.maximum(cm, m_prev) for cm in cm_t]
    inter = [jnp.exp(m_prev - mr) for mr in m_run]
    em = [jnp.exp(-(b + mr)) for b, mr in zip(b_t, m_run)]
    m_last = m_run[-1]
    ws = [jnp.exp(g - m_last) for g in g_t]
    decay = jnp.exp(m_prev - m_last)
    m_ref[...] = b_t[-1] + m_last
    dec_scr[...] = decay

    lhs_q[...] = jnp.zeros(lhs_q.shape, _F32)
    lhs_k[...] = jnp.zeros(lhs_k.shape, _F32)
    lhs_wv[...] = jnp.zeros(lhs_wv.shape, _F32)
    n_all = n0_ref[0]
    den_part = {}
    for h in range(N_HEADS):
        def col(x):
            return x[:, h:h + 1]
        mb = h * HEAD_COLS
        q = [mid_ref[t, :, mb + _M_Q:mb + _M_Q + DK] for t in range(seq_len)]
        k = [mid_ref[t, :, mb + _M_K:mb + _M_K + DK] for t in range(seq_len)]
        v = [mid_ref[t, :, mb + _M_V:mb + _M_V + DV] for t in range(seq_len)]
        n_h = n_all[:, h * DK:(h + 1) * DK]
        n_new = col(decay) * n_h
        for t in range(seq_len):
            qs = q[t] * col(inter[t])
            qs_scr[t, :, h * DK:(h + 1) * DK] = qs
            kk_scr[t, :, h * DK:(h + 1) * DK] = k[t]
            wv_scr[t, :, h * DV:(h + 1) * DV] = col(ws[t]) * v[t]
            n_new = n_new + col(ws[t]) * k[t]
            den = jnp.sum(qs * n_h, axis=-1, keepdims=True)
            num = None
            for s in range(t + 1):
                w = jnp.sum(q[t] * k[s], axis=-1, keepdims=True) * jnp.exp(col(g_t[s]) - col(m_run[t]))
                num = w * v[s] if num is None else num + w * v[s]
                den = den + w
            mg_ref[t, :, h * DV:(h + 1) * DV] = num
            den_part[h, t] = den
        n_ref[:, h * DK:(h + 1) * DK] = n_new

    def body(i, carry):
        for t in range(seq_len):
            lhs_q[t:t + 1, :] = qs_scr[t, pl.ds(i, 1), :]
            lhs_k[t:t + 1, :] = kk_scr[t, pl.ds(i, 1), :]
            lhs_wv[t:t + 1, :] = wv_scr[t, pl.ds(i, 1), :]
        for h in range(N_HEADS):
            c_old = c0_ref[0, i, h]
            r = _dot(lhs_q[:, h * DK:(h + 1) * DK].astype(_BF), c_old.astype(_BF))
            for t in range(seq_len):
                rs_scr[t, pl.ds(i, 1), h * DV:(h + 1) * DV] = r[t:t + 1, :]
            d_c = lax.dot_general(lhs_k[:, h * DK:(h + 1) * DK].astype(_BF),
                                  lhs_wv[:, h * DV:(h + 1) * DV].astype(_BF),
                                  (((0,), (0,)), ((), ())), preferred_element_type=_F32)
            c_ref[0, i, h] = dec_scr[pl.ds(i, 1), h:h + 1] * c_old + d_c
        return carry

    lax.fori_loop(0, bb, body, 0)

    for h in range(N_HEADS):
        mb = h * HEAD_COLS
        for t in range(seq_len):
            num = mg_ref[t, :, h * DV:(h + 1) * DV] + rs_scr[t, :, h * DV:(h + 1) * DV]
            hraw = num * (1.0 / jnp.maximum(jnp.abs(den_part[h, t]), em[t][:, h:h + 1]))
            ms = jnp.mean(hraw * hraw, axis=-1, keepdims=True)
            hn = hraw * lax.rsqrt(ms + EPS) * hng[:, h * DV:(h + 1) * DV]
            ya = mid_ref[t, :, mb + _M_YA:mb + _M_YA + DV]
            og = mid_ref[t, :, mb + _M_OG:mb + _M_OG + DV]
            mg_ref[t, :, h * DV:(h + 1) * DV] = ya + og * hn


def _mlstm_sample_call(layer, mid3, gates3, bias, hng, c0, n0, m0, c_buf, *, bb):
    seq_len, n_seq, _ = mid3.shape
    aliases = {} if c_buf is None else {7: 1}
    if c_buf is None:
        c_buf = jnp.zeros((1,), _F32)
    return pl.pallas_call(
        _mlstm_sample_kernel,
        grid=(n_seq // bb,),
        in_specs=[
            pl.BlockSpec((seq_len, bb, N_HEADS * HEAD_COLS), lambda i: (0, i, 0)),
            pl.BlockSpec((seq_len, bb, GATE_COLS), lambda i: (0, i, 0)),
            pl.BlockSpec((1, 1, GATE_COLS), lambda i: (layer, 0, 0)),
            pl.BlockSpec((1, 1, D_MODEL), lambda i: (layer, 0, 0)),
            pl.BlockSpec((1, bb, N_HEADS, DK, DV), lambda i: (layer, i, 0, 0, 0)),
            pl.BlockSpec((1, bb, N_HEADS * DK), lambda i: (layer, i, 0)),
            pl.BlockSpec((1, bb, CHUNK), lambda i: (layer, i, 0)),
            pl.BlockSpec(memory_space=pl.ANY),
        ],
        out_specs=[
            pl.BlockSpec((seq_len, bb, D_MODEL), lambda i: (0, i, 0)),
            pl.BlockSpec((1, bb, N_HEADS, DK, DV), lambda i: (layer, i, 0, 0, 0)),
            pl.BlockSpec((bb, N_HEADS * DK), lambda i: (i, 0)),
            pl.BlockSpec((bb, CHUNK), lambda i: (i, 0)),
        ],
        out_shape=[
            jax.ShapeDtypeStruct((seq_len, n_seq, D_MODEL), _F32),
            jax.ShapeDtypeStruct((DEPTH, n_seq, N_HEADS, DK, DV), _F32),
            jax.ShapeDtypeStruct((n_seq, N_HEADS * DK), _F32),
            jax.ShapeDtypeStruct((n_seq, CHUNK), _F32),
        ],
        scratch_shapes=[
            pltpu.VMEM((seq_len, bb, N_HEADS * DK), _F32),
            pltpu.VMEM((seq_len, bb, N_HEADS * DK), _F32),
            pltpu.VMEM((seq_len, bb, N_HEADS * DV), _F32),
            pltpu.VMEM((seq_len, bb, N_HEADS * DV), _F32),
            pltpu.VMEM((bb, CHUNK), _F32),
            pltpu.VMEM((8, N_HEADS * DK), _F32),
            pltpu.VMEM((8, N_HEADS * DK), _F32),
            pltpu.VMEM((8, N_HEADS * DV), _F32),
        ],
        input_output_aliases=aliases,
        compiler_params=_params(1),
        name="mlstm_sample",
    )(mid3, gates3, bias, hng, c0, n0, m0, c_buf)


def _mlp_kernel(x_ref, mg_ref, gt1_ref, sc2_ref, sh2_ref, gt2_ref, g2_ref, wo_ref, wu_ref, wd_ref, gf_ref,
                o_ref, xmid, h2, acc, *, mod_reps, final_norm):
    f = pl.program_id(1)

    def mod(ref):
        return ref[0] if mod_reps is None else _rep_rows(ref[...], mod_reps)

    @pl.when(f == 0)
    def _():
        mix = _dot(mg_ref[...].astype(_BF), wo_ref[0])
        xm = x_ref[...] + mod(gt1_ref) * mix
        xmid[...] = xm
        h2[...] = _rms_mod(xm, g2_ref[0], mod(sc2_ref), mod(sh2_ref)).astype(_BF)
        acc[...] = jnp.zeros(acc.shape, _F32)

    a = jnp.maximum(_dot(h2[...], wu_ref[0]), 0.0)
    acc[...] += _dot((a * a).astype(_BF), wd_ref[0])

    @pl.when(f == pl.num_programs(1) - 1)
    def _():
        y = xmid[...] + mod(gt2_ref) * acc[...]
        if final_norm:
            ms = jnp.mean(y * y, axis=-1, keepdims=True)
            y = y * lax.rsqrt(ms + EPS) * gf_ref[...]
        o_ref[...] = y


def _mlp_call(layer, x, merged, gt1, sc2, sh2, gt2, g2, w_out, w_up, w_down, g_final, *,
              tm, tf, seq_len, final_norm):
    m_rows = x.shape[0]
    if seq_len >= tm:
        tiles_per_seq = seq_len // tm
        mod_reps = None
        mod_spec = pl.BlockSpec((1, 1, D_MODEL), lambda i, f: (i // tiles_per_seq, 0, 0))
    else:
        mod_reps = seq_len
        mod_spec = pl.BlockSpec((tm // seq_len, D_MODEL), lambda i, f: (0, 0))
    kern = functools.partial(_mlp_kernel, mod_reps=mod_reps, final_norm=final_norm)
    return pl.pallas_call(
        kern,
        grid=(m_rows // tm, D_FF // tf),
        in_specs=[
            pl.BlockSpec((tm, D_MODEL), lambda i, f: (i, 0)),
            pl.BlockSpec((tm, D_MODEL), lambda i, f: (i, 0)),
            mod_spec, mod_spec, mod_spec, mod_spec,
            pl.BlockSpec((1, 1, D_MODEL), lambda i, f: (layer, 0, 0)),
            pl.BlockSpec((1, D_MODEL, D_MODEL), lambda i, f: (layer, 0, 0)),
            pl.BlockSpec((1, D_MODEL, tf), lambda i, f: (layer, 0, f)),
            pl.BlockSpec((1, tf, D_MODEL), lambda i, f: (layer, f, 0)),
            pl.BlockSpec((1, D_MODEL), lambda i, f: (0, 0)),
        ],
        out_specs=pl.BlockSpec((tm, D_MODEL), lambda i, f: (i, 0)),
        out_shape=jax.ShapeDtypeStruct((m_rows, D_MODEL), _F32),
        scratch_shapes=[
            pltpu.VMEM((tm, D_MODEL), _F32),
            pltpu.VMEM((tm, D_MODEL), _BF),
            pltpu.VMEM((tm, D_MODEL), _F32),
        ],
        compiler_params=_params(2),
        name="outproj_mlp",
    )(x, merged, gt1, sc2, sh2, gt2, g2, w_out, w_up, w_down, g_final)


def kernel(x_prompt, x_sample, state_conv, state_C, state_n, state_m, c_prompt, c_sample,
           w_ada, b_ada, g_norm1, g_norm2, w_in, b_gate, conv_w, conv_b, hn_g, w_out, w_up, w_down, g_final):
    n_p, seq_p, _ = x_prompt.shape
    n_s, seq_s, _ = x_sample.shape
    rows_s = n_s * seq_s
    tm_p = 512

    w_head = w_in[:, :, :_HEAD_W].astype(_BF)
    w_tail = w_in[:, :, _OFF_GA:].astype(_BF)
    w_ig = w_in[:, :, _OFF_IG:_OFF_IG + N_HEADS]
    w_fg = w_in[:, :, _OFF_IG + N_HEADS:_OFF_IG + 2 * N_HEADS]
    zpad = jnp.zeros((DEPTH, D_MODEL, GATE_F_COL - N_HEADS), _F32)
    w_gate = jnp.concatenate([w_ig, zpad, w_fg, zpad], axis=-1).astype(_BF)
    w_out_b = w_out.astype(_BF)
    w_up_b = w_up.astype(_BF)
    w_down_b = w_down.astype(_BF)
    zb = jnp.zeros((DEPTH, 4), _F32)
    bgate = jnp.broadcast_to(
        jnp.concatenate([b_gate[:, :N_HEADS], zb, b_gate[:, N_HEADS:], zb], axis=-1)[:, :, None],
        (DEPTH, BIAS_ROWS, CHUNK))
    zb = jnp.zeros((DEPTH, GATE_F_COL - N_HEADS), _F32)
    bias_row = jnp.concatenate([b_gate[:, :N_HEADS], zb, b_gate[:, N_HEADS:], zb], axis=-1)[:, None, :]
    g1 = g_norm1.reshape(DEPTH, 1, D_MODEL)
    g2 = g_norm2.reshape(DEPTH, 1, D_MODEL)
    hng = hn_g.reshape(DEPTH, 1, D_MODEL)
    cb = conv_b.reshape(DEPTH, 1, D_MODEL)
    gfin = g_final.reshape(1, D_MODEL)

    mod = _ada_call(jnp.concatenate([c_prompt, c_sample], axis=0), w_ada, b_ada)
    mod_p = mod[:, :n_p].reshape(DEPTH, n_p, 6, 1, D_MODEL)
    mod_s = mod[:, n_p:].reshape(DEPTH, n_s, 6, D_MODEL)

    xp = x_prompt.reshape(n_p * seq_p, D_MODEL)
    xs = x_sample.transpose(1, 0, 2).reshape(rows_s, D_MODEL)
    conv_s_in = state_conv.transpose(0, 2, 1, 3)
    n_s_in = state_n.reshape(DEPTH, n_s, N_HEADS * DK)
    m_s_in = jnp.pad(state_m, ((0, 0), (0, 0), (0, CHUNK - N_HEADS)))

    zeros_conv = jnp.zeros((n_p, CONV_W - 1, D_MODEL), _F32)
    zeros_c = jnp.zeros((n_p, N_HEADS, DK, DV), _F32)
    zeros_n = jnp.zeros((n_p, N_HEADS, DK), _F32)
    zeros_m = jnp.zeros((n_p, 8, CHUNK), _F32)

    p_conv, p_n, p_m, s_conv, s_n, s_m = [], [], [], [], [], []
    p_c = s_c = None
    for l in range(DEPTH):
        final = l == DEPTH - 1

        sh1, sc1, gt1, sh2, sc2, gt2 = (mod_p[l, :, j] for j in range(6))
        mid, gates_t, nconv = _inproj_prompt_call(
            l, xp, sc1, sh1, g1, w_head, w_tail, w_gate, bgate, conv_w, cb, zeros_conv, seq_len=seq_p, tm=tm_p)
        merged, p_c, n1, m1 = _mlstm_prompt_call(
            l, mid, gates_t, hng, zeros_c, zeros_n, zeros_m, p_c, seq_len=seq_p, tb=1024)
        xp = _mlp_call(l, xp, merged, gt1, sc2, sh2, gt2, g2, w_out_b, w_up_b, w_down_b, gfin,
                       tm=512, tf=1024, seq_len=seq_p, final_norm=final)
        tiles_per_seq = seq_p // tm_p
        p_conv.append(nconv[tiles_per_seq - 1::tiles_per_seq]); p_n.append(n1); p_m.append(m1[:, :N_HEADS, 0])

        sh1, sc1, gt1, sh2, sc2, gt2 = (mod_s[l, :, j] for j in range(6))
        mid, gates, nconv = _inproj_sample_call(
            l, xs, sc1, sh1, g1, w_head, w_tail, w_gate, conv_w, cb, conv_s_in, seq_len=seq_s)
        merged3, s_c, n1, m1 = _mlstm_sample_call(
            l, mid.reshape(seq_s, n_s, N_HEADS * HEAD_COLS), gates.reshape(seq_s, n_s, GATE_COLS), bias_row, hng,
            state_C, n_s_in, m_s_in, s_c, bb=16)
        xs = _mlp_call(l, xs, merged3.reshape(rows_s, D_MODEL), gt1, sc2, sh2, gt2, g2,
                       w_out_b, w_up_b, w_down_b, gfin,
                       tm=rows_s, tf=1024, seq_len=seq_s, final_norm=final)
        s_conv.append(nconv.transpose(1, 0, 2))
        s_n.append(n1.reshape(n_s, N_HEADS, DK)); s_m.append(m1[:, :N_HEADS])

    return (xp.reshape(n_p, seq_p, D_MODEL), xs.reshape(seq_s, n_s, D_MODEL).transpose(1, 0, 2),
            jnp.stack(p_conv), p_c, jnp.stack(p_n), jnp.stack(p_m),
            jnp.stack(s_conv), s_c, jnp.stack(s_n), jnp.stack(s_m))
```

```python
import functools

import jax
import jax.numpy as jnp
from jax import lax
from jax.experimental import pallas as pl
from jax.experimental.pallas import tpu as pltpu

D_MODEL = 1024
N_HEADS = 4
DK = 128
DV = 256
D_FF = 4096
DEPTH = 4
CONV_W = 3
GATE_CAP = 15.0
EPS = 1e-6
CHUNK = 128
HEAD_COLS = 1024
GATE_COLS = 256
GATE_F_COL = 128
BIAS_ROWS = 16
GATE_ROWS = 24
EXT_COLS = DV + CHUNK
VMEM_LIMIT = 56 * 1024 * 1024

_OFF_XC, _OFF_BG, _OFF_CG = 0, 1024, 2048
_OFF_Q, _OFF_K, _OFF_V, _OFF_O = 3072, 3584, 4096, 5120
_OFF_IG, _OFF_GA, _OFF_GB = 6144, 6152, 7176
_HEAD_W = _OFF_IG

_M_YA, _M_Q, _M_K, _M_V, _M_OG = 0, 256, 384, 512, 768

_BF = jnp.bfloat16
_F32 = jnp.float32


def _dot(a, b):
    return jnp.dot(a, b, preferred_element_type=_F32)


def _dot_nt(a, b):
    return lax.dot_general(a, b, (((1,), (1,)), ((), ())), preferred_element_type=_F32)


def _sigmoid(x):
    return 0.5 * jnp.tanh(0.5 * x) + 0.5


def _rms_mod(x, g, sc, sh):
    ms = jnp.mean(x * x, axis=-1, keepdims=True)
    return (x * lax.rsqrt(ms + EPS)) * (g * (1.0 + sc)) + sh


def _rep_rows(v, reps):
    return v if reps == 1 else jnp.concatenate([v] * reps, axis=0)


def _params(n_axes):
    return pltpu.CompilerParams(dimension_semantics=("arbitrary",) * n_axes, vmem_limit_bytes=VMEM_LIMIT)


def _ada_kernel(c_ref, w_ref, b_ref, o_ref):
    c = c_ref[...]
    a = (c * _sigmoid(c)).astype(_BF)
    o_ref[0] = _dot(a, w_ref[0].astype(_BF)) + b_ref[0]


def _ada_call(c_all, w_ada, b_ada):
    n_rows = c_all.shape[0]
    tn = 1024
    return pl.pallas_call(
        _ada_kernel,
        grid=(DEPTH, 6 * D_MODEL // tn),
        in_specs=[
            pl.BlockSpec((n_rows, D_MODEL), lambda l, j: (0, 0)),
            pl.BlockSpec((1, D_MODEL, tn), lambda l, j: (l, 0, j)),
            pl.BlockSpec((1, 1, tn), lambda l, j: (l, 0, j)),
        ],
        out_specs=pl.BlockSpec((1, n_rows, tn), lambda l, j: (l, 0, j)),
        out_shape=jax.ShapeDtypeStruct((DEPTH, n_rows, 6 * D_MODEL), _F32),
        compiler_params=_params(2),
        name="ada_mod",
    )(c_all, w_ada, b_ada.reshape(DEPTH, 1, 6 * D_MODEL))


def _w_in_specs(layer, idx):
    def spec(width, first_block):
        return pl.BlockSpec((1, D_MODEL, width), lambda *g: (layer, 0, first_block + idx(*g)))
    head = [spec(256, _OFF_XC // 256), spec(256, _OFF_BG // 256), spec(256, _OFF_CG // 256),
            spec(128, _OFF_Q // 128), spec(128, _OFF_K // 128), spec(256, _OFF_V // 256),
            spec(256, _OFF_O // 256)]
    tail = [spec(256, 0), spec(256, (_OFF_GB - _OFF_GA) // 256)]
    return head, tail


def _scan_lanes(x, combine, identity, seg_len):
    pos = lax.broadcasted_iota(jnp.int32, x.shape, 1) % seg_len
    shift = 1
    while shift < seg_len:
        x = combine(x, jnp.where(pos >= shift, pltpu.roll(x, shift, 1), identity))
        shift *= 2
    return x


def _softcap(a):
    return GATE_CAP * jnp.tanh(a / GATE_CAP)


def _log_sigmoid(x):
    return -(jnp.maximum(-x, 0.0) + jnp.log(1.0 + jnp.exp(-jnp.abs(x))))


def _gate_rows_prompt(gates, bias, gt_ref, first_chunk):
    n_chunks = gates.shape[0] // CHUNK
    gates_t = gates.T

    def stack(r0):
        return jnp.concatenate([gates_t[r0:r0 + 8, c * CHUNK:(c + 1) * CHUNK] for c in range(n_chunks)], axis=0)

    i_pre = _softcap(stack(0) + jnp.concatenate([bias[0:8]] * n_chunks, axis=0))
    f_pre = _softcap(stack(GATE_F_COL) + jnp.concatenate([bias[8:16]] * n_chunks, axis=0))
    b = _scan_lanes(_log_sigmoid(f_pre), jnp.add, 0.0, CHUNK)
    g = i_pre - b
    cm = _scan_lanes(g, jnp.maximum, -3e38, CHUNK)
    for c in range(n_chunks):
        gt_ref[first_chunk + c, 0:8] = b[c * 8:(c + 1) * 8]
        gt_ref[first_chunk + c, 8:16] = g[c * 8:(c + 1) * 8]
        gt_ref[first_chunk + c, 16:24] = cm[c * 8:(c + 1) * 8]


def _inproj_project(h, w_refs):
    wxc, wbg, wcg, wq, wk, wv, wo, wga, wgb = w_refs
    xc = _dot(h, wxc[0])
    cg = _dot(h, wcg[0])
    qk = _dot(h, jnp.concatenate([wq[0], wk[0]], axis=1))
    return dict(u=cg * xc, bg=_dot(h, wbg[0]), ga=_dot(h, wga[0]), qk=qk,
                v=_dot(h, wv[0]), o=_dot(h, wo[0]), gb=_dot(h, wgb[0]))


def _inproj_epilogue(p, conv, mid_ref, out_dtype):
    mid_ref[:, _M_YA:_M_YA + 256] = (_sigmoid(p["ga"]) * (p["bg"] * conv)).astype(out_dtype)
    mid_ref[:, _M_Q:_M_Q + DK] = p["qk"][:, :DK].astype(out_dtype)
    mid_ref[:, _M_K:_M_K + DK] = (p["qk"][:, DK:] * (DK ** -0.5)).astype(out_dtype)
    mid_ref[:, _M_V:_M_V + 256] = p["v"].astype(out_dtype)
    mid_ref[:, _M_OG:_M_OG + 256] = (_sigmoid(p["o"]) * _sigmoid(p["gb"])).astype(out_dtype)


_RAW_COLS = 8 * 256


def _inproj_prompt_kernel(x_ref, sc_ref, sh_ref, g_ref, wxc, wbg, wcg, wq, wk, wv, wo, wga, wgb, wg_ref,
                          bg_ref, cw_ref, cb_ref, cprev_ref, mid_ref, gt_ref, nconv_ref,
                          h_scr, gate_scr, raw, ubuf, carry, *, tiles_per_seq, n_work):
    j = pl.program_id(0)
    tm = x_ref.shape[0]
    cur = jnp.minimum(j, n_work - 1)
    s = cur % N_HEADS
    prev = jnp.maximum(j - 1, 0)
    tile_p = prev // N_HEADS
    s_p = prev % N_HEADS
    chunks_per_step = (tm // CHUNK) // N_HEADS

    @pl.when(j == 0)
    def _():
        raw[1] = jnp.zeros(raw.shape[1:], _F32)
        carry[...] = jnp.zeros(carry.shape, _F32)

    @pl.when(jnp.logical_and(s == 0, j < n_work))
    def _():
        h = _rms_mod(x_ref[...], g_ref[0], sc_ref[0], sh_ref[0]).astype(_BF)
        h_scr[...] = h
        gate_scr[...] = _dot(h, wg_ref[0])

    def step(slot):
        h = h_scr[...]
        for idx, w in enumerate((wxc[0], wcg[0], wbg[0], wga[0], jnp.concatenate([wq[0], wk[0]], axis=1),
                                 wv[0], wo[0], wgb[0])):
            raw[slot, :, idx * 256:(idx + 1) * 256] = _dot(h, w)

        rows0 = pl.multiple_of(s * (chunks_per_step * CHUNK), CHUNK)
        _gate_rows_prompt(gate_scr[pl.ds(rows0, chunks_per_step * CHUNK), :], bg_ref[0], gt_ref,
                          s * chunks_per_step)

        def piece(idx):
            return raw[1 - slot, :, idx * 256:(idx + 1) * 256]

        u = piece(1) * piece(0)
        seq_start = (tile_p % tiles_per_seq) == 0
        ubuf[6:8, :] = jnp.where(seq_start, cprev_ref[0], carry[s_p])
        ubuf[8:8 + tm, :] = u
        cw = cw_ref[0]
        conv = (cb_ref[0] + cw[0:1] * ubuf[6:6 + tm, :] + cw[1:2] * ubuf[7:7 + tm, :] + cw[2:3] * u)
        last2 = u[tm - 2:tm, :]
        carry[s_p] = last2
        nconv_ref[0] = last2
        _inproj_epilogue(dict(bg=piece(2), ga=piece(3), qk=piece(4), v=piece(5), o=piece(6), gb=piece(7)),
                         conv, mid_ref, _BF)

    for parity in range(2):
        pl.when(j % 2 == parity)(functools.partial(step, parity))


def _inproj_prompt_call(layer, x, sc, sh, g, w_head, w_tail, w_gate, bgate, conv_w, conv_b, conv_prev,
                        *, seq_len, tm):
    m_rows = x.shape[0]
    n_tiles = m_rows // tm
    tiles_per_seq = seq_len // tm
    n_work = n_tiles * N_HEADS

    def cur(j):
        return jnp.minimum(j, n_work - 1)

    def prev(j):
        return jnp.maximum(j - 1, 0)

    kern = functools.partial(_inproj_prompt_kernel, tiles_per_seq=tiles_per_seq, n_work=n_work)
    head_specs, tail_specs = _w_in_specs(layer, lambda j: cur(j) % N_HEADS)
    seq_of_cur = lambda j: (cur(j) // N_HEADS // tiles_per_seq, 0, 0)
    return pl.pallas_call(
        kern,
        grid=(n_work + 1,),
        in_specs=[
            pl.BlockSpec((tm, D_MODEL), lambda j: (cur(j) // N_HEADS, 0)),
            pl.BlockSpec((1, 1, D_MODEL), seq_of_cur),
            pl.BlockSpec((1, 1, D_MODEL), seq_of_cur),
            pl.BlockSpec((1, 1, D_MODEL), lambda j: (layer, 0, 0)),
            *head_specs, *tail_specs,
            pl.BlockSpec((1, D_MODEL, GATE_COLS), lambda j: (layer, 0, 0)),
            pl.BlockSpec((1, BIAS_ROWS, CHUNK), lambda j: (layer, 0, 0)),
            pl.BlockSpec((1, CONV_W, 256), lambda j: (layer, 0, prev(j) % N_HEADS)),
            pl.BlockSpec((1, 1, 256), lambda j: (layer, 0, prev(j) % N_HEADS)),
            pl.BlockSpec((1, CONV_W - 1, 256),
                         lambda j: (prev(j) // N_HEADS // tiles_per_seq, 0, prev(j) % N_HEADS)),
        ],
        out_specs=[
            pl.BlockSpec((tm, HEAD_COLS), lambda j: (prev(j) // N_HEADS, prev(j) % N_HEADS)),
            pl.BlockSpec((tm // CHUNK, GATE_ROWS, CHUNK), lambda j: (cur(j) // N_HEADS, 0, 0)),
            pl.BlockSpec((1, CONV_W - 1, 256), lambda j: (prev(j) // N_HEADS, 0, prev(j) % N_HEADS)),
        ],
        out_shape=[
            jax.ShapeDtypeStruct((m_rows, N_HEADS * HEAD_COLS), _BF),
            jax.ShapeDtypeStruct((m_rows // CHUNK, GATE_ROWS, CHUNK), _F32),
            jax.ShapeDtypeStruct((n_tiles, CONV_W - 1, D_MODEL), _F32),
        ],
        scratch_shapes=[
            pltpu.VMEM((tm, D_MODEL), _BF),
            pltpu.VMEM((tm, GATE_COLS), _F32),
            pltpu.VMEM((2, tm, _RAW_COLS), _F32),
            pltpu.VMEM((tm + 8, 256), _F32),
            pltpu.VMEM((N_HEADS, CONV_W - 1, 256), _F32),
        ],
        compiler_params=_params(1),
        name="inproj_prompt",
    )(x, sc, sh, g, *([w_head] * 7), *([w_tail] * 2), w_gate, bgate, conv_w, conv_b, conv_prev)


def _inproj_sample_kernel(x_ref, sc_ref, sh_ref, g_ref, wxc, wbg, wcg, wq, wk, wv, wo, wga, wgb, wg_ref,
                          cw_ref, cb_ref, cprev_ref, mid_ref, gt_ref, nconv_ref, h_scr, *, seq_len):
    s = pl.program_id(0)
    n_b = sc_ref.shape[0]

    @pl.when(s == 0)
    def _():
        h = _rms_mod(x_ref[...], g_ref[0], _rep_rows(sc_ref[...], seq_len), _rep_rows(sh_ref[...], seq_len))
        h = h.astype(_BF)
        h_scr[...] = h
        gt_ref[...] = _dot(h, wg_ref[0])

    p = _inproj_project(h_scr[...], (wxc, wbg, wcg, wq, wk, wv, wo, wga, wgb))
    u = p["u"]
    prev0 = cprev_ref[0, 0]
    prev1 = cprev_ref[0, 1]
    p1 = jnp.concatenate([prev1, u[0:(seq_len - 1) * n_b]], axis=0)
    p2 = jnp.concatenate([prev0, prev1, u[0:(seq_len - 2) * n_b]], axis=0)
    cw = cw_ref[0]
    conv = cb_ref[0] + cw[0:1] * p2 + cw[1:2] * p1 + cw[2:3] * u
    nconv_ref[0] = u[(seq_len - 2) * n_b:(seq_len - 1) * n_b]
    nconv_ref[1] = u[(seq_len - 1) * n_b:seq_len * n_b]
    _inproj_epilogue(p, conv, mid_ref, _F32)


def _inproj_sample_call(layer, x, sc, sh, g, w_head, w_tail, w_gate, conv_w, conv_b, conv_prev, *, seq_len):
    tm = x.shape[0]
    n_b = tm // seq_len
    kern = functools.partial(_inproj_sample_kernel, seq_len=seq_len)
    head_specs, tail_specs = _w_in_specs(layer, lambda s: s)
    return pl.pallas_call(
        kern,
        grid=(N_HEADS,),
        in_specs=[
            pl.BlockSpec((tm, D_MODEL), lambda s: (0, 0)),
            pl.BlockSpec((n_b, D_MODEL), lambda s: (0, 0)),
            pl.BlockSpec((n_b, D_MODEL), lambda s: (0, 0)),
            pl.BlockSpec((1, 1, D_MODEL), lambda s: (layer, 0, 0)),
            *head_specs, *tail_specs,
            pl.BlockSpec((1, D_MODEL, GATE_COLS), lambda s: (layer, 0, 0)),
            pl.BlockSpec((1, CONV_W, 256), lambda s: (layer, 0, s)),
            pl.BlockSpec((1, 1, 256), lambda s: (layer, 0, s)),
            pl.BlockSpec((1, CONV_W - 1, n_b, 256), lambda s: (layer, 0, 0, s)),
        ],
        out_specs=[
            pl.BlockSpec((tm, HEAD_COLS), lambda s: (0, s)),
            pl.BlockSpec((tm, GATE_COLS), lambda s: (0, 0)),
            pl.BlockSpec((CONV_W - 1, n_b, 256), lambda s: (0, 0, s)),
        ],
        out_shape=[
            jax.ShapeDtypeStruct((tm, N_HEADS * HEAD_COLS), _F32),
            jax.ShapeDtypeStruct((tm, GATE_COLS), _F32),
            jax.ShapeDtypeStruct((CONV_W - 1, n_b, D_MODEL), _F32),
        ],
        scratch_shapes=[pltpu.VMEM((tm, D_MODEL), _BF)],
        compiler_params=_params(1),
        name="inproj_sample",
    )(x, sc, sh, g, *([w_head] * 7), *([w_tail] * 2), w_gate, conv_w, conv_b, conv_prev)


def _causal_mask(lq):
    row_id = lax.broadcasted_iota(jnp.int32, (lq, CHUNK), 0)
    col_id = lax.broadcasted_iota(jnp.int32, (lq, CHUNK), 1)
    return col_id <= row_id


def _chunk_local(a, causal):
    dmat = jnp.where(causal, jnp.exp(a["g_row"] - a["m_col"]), 0.0)
    s_bf = (_dot_nt(a["q"], a["k"]) * dmat).astype(_BF)
    kw = a["k"].astype(_F32) * a["ws_col"]
    return s_bf, _dot(kw.T.astype(_BF), a["v_ext"])


def _chunk_output(a, s_bf, d_state, cx_ref, h, hng_h):
    q_inter = (a["q"].astype(_F32) * a["inter_col"]).astype(_BF)
    cx = cx_ref[h]
    rhs = jnp.concatenate([a["v_ext"], cx.astype(_BF)], axis=0)
    res = _dot(jnp.concatenate([s_bf, q_inter], axis=1), rhs)
    cx_ref[h] = a["dec"] * cx + d_state
    num = res[:, :DV]
    rden = 1.0 / jnp.maximum(jnp.abs(res[:, DV:]), a["em_col"])
    sq = jnp.sum(num * num, axis=-1, keepdims=True) * (1.0 / DV)
    scale = rden * lax.rsqrt(rden * rden * sq + EPS)
    hn = num * jnp.concatenate([scale, scale], axis=1) * hng_h
    return a["ya"] + a["og"] * hn


def _mlstm_prompt_kernel(mid_ref, gt_ref, hng_ref, c0_ref, n0_ref, m0_ref, cbuf_ref,
                         mg_ref, c_ref, n_ref, m_ref, cx_ref):
    del cbuf_ref
    t = pl.program_id(1)

    @pl.when(t == 0)
    def _():
        m_ref[...] = m0_ref[...]
        for h in range(N_HEADS):
            cx_ref[h, :, 0:DV] = c0_ref[0, h]
            cx_ref[h, :, DV:EXT_COLS] = jnp.broadcast_to(n0_ref[0, h:h + 1, :], (CHUNK, DK)).T

    n_chunks = mid_ref.shape[0] // CHUNK
    rows = n_chunks * 8
    hng = hng_ref[0]
    gates = gt_ref[...]
    b = gates[:, 0:8, :].reshape(rows, CHUNK)
    g = gates[:, 8:16, :].reshape(rows, CHUNK)
    cm = gates[:, 16:24, :].reshape(rows, CHUNK)
    b_last = jnp.broadcast_to(b[:, CHUNK - 1:CHUNK], (rows, CHUNK))
    cm_last = jnp.broadcast_to(cm[:, CHUNK - 1:CHUNK], (rows, CHUNK))

    m_prev = m_ref[0]
    m_prevs = []
    for c in range(n_chunks):
        m_prevs.append(m_prev)
        m_prev = b_last[c * 8:(c + 1) * 8] + jnp.maximum(cm_last[c * 8:(c + 1) * 8], m_prev)
    m_ref[0] = m_prev
    m_prev_all = jnp.concatenate(m_prevs, axis=0)

    m_run = jnp.maximum(cm, m_prev_all)
    inter = jnp.exp(m_prev_all - m_run)
    em = jnp.exp(-(b + m_run))
    m_last = jnp.maximum(cm_last, m_prev_all)
    ws = jnp.exp(g - m_last)
    decay = jnp.exp(m_prev_all - m_last)

    causal = _causal_mask(CHUNK)
    ones = jnp.ones((CHUNK, CHUNK), _BF)
    group = CHUNK // 32
    cols_of = []
    for c0 in range(0, n_chunks, group):
        pieces = []
        for c in range(c0, min(c0 + group, n_chunks)):
            sl = slice(c * 8, (c + 1) * 8)
            pieces += [m_run[sl], inter[sl], em[sl], ws[sl]]
        if len(pieces) * 8 < CHUNK:
            pieces.append(jnp.zeros((CHUNK - len(pieces) * 8, CHUNK), _F32))
        cols_of.append(jnp.concatenate(pieces, axis=0).T)

    def chunk_inputs(c):
        cols = cols_of[c // group]
        base = (c % group) * 32
        r0 = c * CHUNK
        heads = []
        for h in range(N_HEADS):
            mb = h * HEAD_COLS
            row = c * 8 + h
            v = mid_ref[r0:r0 + CHUNK, mb + _M_V:mb + _M_V + 256]
            heads.append(dict(
                q=mid_ref[r0:r0 + CHUNK, mb + _M_Q:mb + _M_Q + 128],
                k=mid_ref[r0:r0 + CHUNK, mb + _M_K:mb + _M_K + 128],
                v_ext=jnp.concatenate([v, ones], axis=1),
                ya=mid_ref[r0:r0 + CHUNK, mb + _M_YA:mb + _M_YA + 256].astype(_F32),
                og=mid_ref[r0:r0 + CHUNK, mb + _M_OG:mb + _M_OG + 256].astype(_F32),
                g_row=g[row:row + 1, :],
                m_col=cols[:, base + h:base + h + 1],
                inter_col=cols[:, base + 8 + h:base + 9 + h],
                em_col=cols[:, base + 16 + h:base + 17 + h],
                ws_col=cols[:, base + 24 + h:base + 25 + h],
                dec=decay[row:row + 1, 0:1]))
        return heads

    for c in range(n_chunks):
        outs = []
        for h, a in enumerate(chunk_inputs(c)):
            s_bf, d_state = _chunk_local(a, causal)
            outs.append(_chunk_output(a, s_bf, d_state, cx_ref, h, hng[:, h * DV:(h + 1) * DV]))
        mg_ref[c * CHUNK:(c + 1) * CHUNK, :] = jnp.concatenate(outs, axis=1).astype(_BF)

    @pl.when(t == pl.num_programs(1) - 1)
    def _():
        for h in range(N_HEADS):
            c_ref[0, 0, h] = cx_ref[h, :, 0:DV]
            n_ref[0, h:h + 1, :] = cx_ref[h, :, DV:EXT_COLS].T[0:1, :]


def _mlstm_prompt_call(layer, mid, gates_t, hng, c0, n0, m0, c_buf, *, seq_len, tb):
    m_rows = mid.shape[0]
    n_seq = m_rows // seq_len
    steps = seq_len // tb
    aliases = {} if c_buf is None else {6: 1}
    if c_buf is None:
        c_buf = jnp.zeros((1,), _F32)
    return pl.pallas_call(
        _mlstm_prompt_kernel,
        grid=(n_seq, steps),
        in_specs=[
            pl.BlockSpec((tb, N_HEADS * HEAD_COLS), lambda b, t: (b * steps + t, 0)),
            pl.BlockSpec((tb // CHUNK, GATE_ROWS, CHUNK), lambda b, t: (b * steps + t, 0, 0)),
            pl.BlockSpec((1, 1, D_MODEL), lambda b, t: (layer, 0, 0)),
            pl.BlockSpec((1, N_HEADS, DK, DV), lambda b, t: (b, 0, 0, 0)),
            pl.BlockSpec((1, N_HEADS, DK), lambda b, t: (b, 0, 0)),
            pl.BlockSpec((1, 8, CHUNK), lambda b, t: (b, 0, 0)),
            pl.BlockSpec(memory_space=pl.ANY),
        ],
        out_specs=[
            pl.BlockSpec((tb, D_MODEL), lambda b, t: (b * steps + t, 0)),
            pl.BlockSpec((1, 1, N_HEADS, DK, DV), lambda b, t: (layer, b, 0, 0, 0)),
            pl.BlockSpec((1, N_HEADS, DK), lambda b, t: (b, 0, 0)),
            pl.BlockSpec((1, 8, CHUNK), lambda b, t: (b, 0, 0)),
        ],
        out_shape=[
            jax.ShapeDtypeStruct((m_rows, D_MODEL), _BF),
            jax.ShapeDtypeStruct((DEPTH, n_seq, N_HEADS, DK, DV), _F32),
            jax.ShapeDtypeStruct((n_seq, N_HEADS, DK), _F32),
            jax.ShapeDtypeStruct((n_seq, 8, CHUNK), _F32),
        ],
        scratch_shapes=[pltpu.VMEM((N_HEADS, DK, EXT_COLS), _F32)],
        input_output_aliases=aliases,
        compiler_params=_params(2),
        name="mlstm_prompt",
    )(mid, gates_t, hng, c0, n0, m0, c_buf)


def _mlstm_sample_kernel(mid_ref, gt_ref, bias_ref, hng_ref, c0_ref, n0_ref, m0_ref, cbuf_ref,
                         mg_ref, c_ref, n_ref, m_ref,
                         qs_scr, kk_scr, wv_scr, rs_scr, dec_scr, lhs_q, lhs_k, lhs_wv):
    del cbuf_ref
    seq_len, bb, _ = mid_ref.shape
    hng = hng_ref[0]
    bias = bias_ref[0]
    m_prev = m0_ref[0]

    b_t, g_t, cm_t = [], [], []
    for t in range(seq_len):
        pre = _softcap(gt_ref[t] + bias)
        logf = _log_sigmoid(pre[:, GATE_F_COL:GATE_F_COL + CHUNK])
        b_t.append(logf if t == 0 else b_t[-1] + logf)
        g_t.append(pre[:, 0:CHUNK] - b_t[-1])
        cm_t.append(g_t[-1] if t == 0 else jnp.maximum(cm_t[-1], g_t[-1]))
    m_run = [jnp.maximum(cm, m_prev) for cm in cm_t]
    inter = [jnp.exp(m_prev - mr) for mr in m_run]
    em = [jnp.exp(-(b + mr)) for b, mr in zip(b_t, m_run)]
    m_last = m_run[-1]
    ws = [jnp.exp(g - m_last) for g in g_t]
    decay = jnp.exp(m_prev - m_last)
    m_ref[...] = b_t[-1] + m_last
    dec_scr[...] = decay

    lhs_q[...] = jnp.zeros(lhs_q.shape, _F32)
    lhs_k[...] = jnp.zeros(lhs_k.shape, _F32)
    lhs_wv[...] = jnp.zeros(lhs_wv.shape, _F32)
    n_all = n0_ref[0]
    den_part = {}
    for h in range(N_HEADS):
        def col(x):
            return x[:, h:h + 1]
        mb = h * HEAD_COLS
        q = [mid_ref[t, :, mb + _M_Q:mb + _M_Q + DK] for t in range(seq_len)]
        k = [mid_ref[t, :, mb + _M_K:mb + _M_K + DK] for t in range(seq_len)]
        v = [mid_ref[t, :, mb + _M_V:mb + _M_V + DV] for t in range(seq_len)]
        n_h = n_all[:, h * DK:(h + 1) * DK]
        n_new = col(decay) * n_h
        for t in range(seq_len):
            qs = q[t] * col(inter[t])
            qs_scr[t, :, h * DK:(h + 1) * DK] = qs
            kk_scr[t, :, h * DK:(h + 1) * DK] = k[t]
            wv_scr[t, :, h * DV:(h + 1) * DV] = col(ws[t]) * v[t]
            n_new = n_new + col(ws[t]) * k[t]
            den = jnp.sum(qs * n_h, axis=-1, keepdims=True)
            num = None
            for s in range(t + 1):
                w = jnp.sum(q[t] * k[s], axis=-1, keepdims=True) * jnp.exp(col(g_t[s]) - col(m_run[t]))
                num = w * v[s] if num is None else num + w * v[s]
                den = den + w
            mg_ref[t, :, h * DV:(h + 1) * DV] = num
            den_part[h, t] = den
        n_ref[:, h * DK:(h + 1) * DK] = n_new

    def body(i, carry):
        for t in range(seq_len):
            lhs_q[t:t + 1, :] = qs_scr[t, pl.ds(i, 1), :]
            lhs_k[t:t + 1, :] = kk_scr[t, pl.ds(i, 1), :]
            lhs_wv[t:t + 1, :] = wv_scr[t, pl.ds(i, 1), :]
        for h in range(N_HEADS):
            c_old = c0_ref[0, i, h]
            r = _dot(lhs_q[:, h * DK:(h + 1) * DK].astype(_BF), c_old.astype(_BF))
            for t in range(seq_len):
                rs_scr[t, pl.ds(i, 1), h * DV:(h + 1) * DV] = r[t:t + 1, :]
            d_c = lax.dot_general(lhs_k[:, h * DK:(h + 1) * DK].astype(_BF),
                                  lhs_wv[:, h * DV:(h + 1) * DV].astype(_BF),
                                  (((0,), (0,)), ((), ())), preferred_element_type=_F32)
            c_ref[0, i, h] = dec_scr[pl.ds(i, 1), h:h + 1] * c_old + d_c
        return carry

    lax.fori_loop(0, bb, body, 0)

    for h in range(N_HEADS):
        mb = h * HEAD_COLS
        for t in range(seq_len):
            num = mg_ref[t, :, h * DV:(h + 1) * DV] + rs_scr[t, :, h * DV:(h + 1) * DV]
            hraw = num * (1.0 / jnp.maximum(jnp.abs(den_part[h, t]), em[t][:, h:h + 1]))
            ms = jnp.mean(hraw * hraw, axis=-1, keepdims=True)
            hn = hraw * lax.rsqrt(ms + EPS) * hng[:, h * DV:(h + 1) * DV]
            ya = mid_ref[t, :, mb + _M_YA:mb + _M_YA + DV]
            og = mid_ref[t, :, mb + _M_OG:mb + _M_OG + DV]
            mg_ref[t, :, h * DV:(h + 1) * DV] = ya + og * hn


def _mlstm_sample_call(layer, mid3, gates3, bias, hng, c0, n0, m0, c_buf, *, bb):
    seq_len, n_seq, _ = mid3.shape
    aliases = {} if c_buf is None else {7: 1}
    if c_buf is None:
        c_buf = jnp.zeros((1,), _F32)
    return pl.pallas_call(
        _mlstm_sample_kernel,
        grid=(n_seq // bb,),
        in_specs=[
            pl.BlockSpec((seq_len, bb, N_HEADS * HEAD_COLS), lambda i: (0, i, 0)),
            pl.BlockSpec((seq_len, bb, GATE_COLS), lambda i: (0, i, 0)),
            pl.BlockSpec((1, 1, GATE_COLS), lambda i: (layer, 0, 0)),
            pl.BlockSpec((1, 1, D_MODEL), lambda i: (layer, 0, 0)),
            pl.BlockSpec((1, bb, N_HEADS, DK, DV), lambda i: (layer, i, 0, 0, 0)),
            pl.BlockSpec((1, bb, N_HEADS * DK), lambda i: (layer, i, 0)),
            pl.BlockSpec((1, bb, CHUNK), lambda i: (layer, i, 0)),
            pl.BlockSpec(memory_space=pl.ANY),
        ],
        out_specs=[
            pl.BlockSpec((seq_len, bb, D_MODEL), lambda i: (0, i, 0)),
            pl.BlockSpec((1, bb, N_HEADS, DK, DV), lambda i: (layer, i, 0, 0, 0)),
            pl.BlockSpec((bb, N_HEADS * DK), lambda i: (i, 0)),
            pl.BlockSpec((bb, CHUNK), lambda i: (i, 0)),
        ],
        out_shape=[
            jax.ShapeDtypeStruct((seq_len, n_seq, D_MODEL), _F32),
            jax.ShapeDtypeStruct((DEPTH, n_seq, N_HEADS, DK, DV), _F32),
            jax.ShapeDtypeStruct((n_seq, N_HEADS * DK), _F32),
            jax.ShapeDtypeStruct((n_seq, CHUNK), _F32),
        ],
        scratch_shapes=[
            pltpu.VMEM((seq_len, bb, N_HEADS * DK), _F32),
            pltpu.VMEM((seq_len, bb, N_HEADS * DK), _F32),
            pltpu.VMEM((seq_len, bb, N_HEADS * DV), _F32),
            pltpu.VMEM((seq_len, bb, N_HEADS * DV), _F32),
            pltpu.VMEM((bb, CHUNK), _F32),
            pltpu.VMEM((8, N_HEADS * DK), _F32),
            pltpu.VMEM((8, N_HEADS * DK), _F32),
            pltpu.VMEM((8, N_HEADS * DV), _F32),
        ],
        input_output_aliases=aliases,
        compiler_params=_params(1),
        name="mlstm_sample",
    )(mid3, gates3, bias, hng, c0, n0, m0, c_buf)


def _mlp_kernel(x_ref, mg_ref, gt1_ref, sc2_ref, sh2_ref, gt2_ref, g2_ref, wo_ref, wu_ref, wd_ref, gf_ref,
                o_ref, xmid, h2, acc, *, mod_reps, final_norm):
    f = pl.program_id(1)

    def mod(ref):
        return ref[0] if mod_reps is None else _rep_rows(ref[...], mod_reps)

    @pl.when(f == 0)
    def _():
        mix = _dot(mg_ref[...].astype(_BF), wo_ref[0])
        xm = x_ref[...] + mod(gt1_ref) * mix
        xmid[...] = xm
        h2[...] = _rms_mod(xm, g2_ref[0], mod(sc2_ref), mod(sh2_ref)).astype(_BF)
        acc[...] = jnp.zeros(acc.shape, _F32)

    a = jnp.maximum(_dot(h2[...], wu_ref[0]), 0.0)
    acc[...] += _dot((a * a).astype(_BF), wd_ref[0])

    @pl.when(f == pl.num_programs(1) - 1)
    def _():
        y = xmid[...] + mod(gt2_ref) * acc[...]
        if final_norm:
            ms = jnp.mean(y * y, axis=-1, keepdims=True)
            y = y * lax.rsqrt(ms + EPS) * gf_ref[...]
        o_ref[...] = y


def _mlp_call(layer, x, merged, gt1, sc2, sh2, gt2, g2, w_out, w_up, w_down, g_final, *,
              tm, tf, seq_len, final_norm):
    m_rows = x.shape[0]
    if seq_len >= tm:
        tiles_per_seq = seq_len // tm
        mod_reps = None
        mod_spec = pl.BlockSpec((1, 1, D_MODEL), lambda i, f: (i // tiles_per_seq, 0, 0))
    else:
        mod_reps = seq_len
        mod_spec = pl.BlockSpec((tm // seq_len, D_MODEL), lambda i, f: (0, 0))
    kern = functools.partial(_mlp_kernel, mod_reps=mod_reps, final_norm=final_norm)
    return pl.pallas_call(
        kern,
        grid=(m_rows // tm, D_FF // tf),
        in_specs=[
            pl.BlockSpec((tm, D_MODEL), lambda i, f: (i, 0)),
            pl.BlockSpec((tm, D_MODEL), lambda i, f: (i, 0)),
            mod_spec, mod_spec, mod_spec, mod_spec,
            pl.BlockSpec((1, 1, D_MODEL), lambda i, f: (layer, 0, 0)),
            pl.BlockSpec((1, D_MODEL, D_MODEL), lambda i, f: (layer, 0, 0)),
            pl.BlockSpec((1, D_MODEL, tf), lambda i, f: (layer, 0, f)),
            pl.BlockSpec((1, tf, D_MODEL), lambda i, f: (layer, f, 0)),
            pl.BlockSpec((1, D_MODEL), lambda i, f: (0, 0)),
        ],
        out_specs=pl.BlockSpec((tm, D_MODEL), lambda i, f: (i, 0)),
        out_shape=jax.ShapeDtypeStruct((m_rows, D_MODEL), _F32),
        scratch_shapes=[
            pltpu.VMEM((tm, D_MODEL), _F32),
            pltpu.VMEM((tm, D_MODEL), _BF),
            pltpu.VMEM((tm, D_MODEL), _F32),
        ],
        compiler_params=_params(2),
        name="outproj_mlp",
    )(x, merged, gt1, sc2, sh2, gt2, g2, w_out, w_up, w_down, g_final)


def kernel(x_prompt, x_sample, state_conv, state_C, state_n, state_m, c_prompt, c_sample,
           w_ada, b_ada, g_norm1, g_norm2, w_in, b_gate, conv_w, conv_b, hn_g, w_out, w_up, w_down, g_final):
    n_p, seq_p, _ = x_prompt.shape
    n_s, seq_s, _ = x_sample.shape
    rows_s = n_s * seq_s
    tm_p = 1024

    w_head = w_in[:, :, :_HEAD_W].astype(_BF)
    w_tail = w_in[:, :, _OFF_GA:].astype(_BF)
    w_ig = w_in[:, :, _OFF_IG:_OFF_IG + N_HEADS]
    w_fg = w_in[:, :, _OFF_IG + N_HEADS:_OFF_IG + 2 * N_HEADS]
    zpad = jnp.zeros((DEPTH, D_MODEL, GATE_F_COL - N_HEADS), _F32)
    w_gate = jnp.concatenate([w_ig, zpad, w_fg, zpad], axis=-1).astype(_BF)
    w_out_b = w_out.astype(_BF)
    w_up_b = w_up.astype(_BF)
    w_down_b = w_down.astype(_BF)
    zb = jnp.zeros((DEPTH, 4), _F32)
    bgate = jnp.broadcast_to(
        jnp.concatenate([b_gate[:, :N_HEADS], zb, b_gate[:, N_HEADS:], zb], axis=-1)[:, :, None],
        (DEPTH, BIAS_ROWS, CHUNK))
    zb = jnp.zeros((DEPTH, GATE_F_COL - N_HEADS), _F32)
    bias_row = jnp.concatenate([b_gate[:, :N_HEADS], zb, b_gate[:, N_HEADS:], zb], axis=-1)[:, None, :]
    g1 = g_norm1.reshape(DEPTH, 1, D_MODEL)
    g2 = g_norm2.reshape(DEPTH, 1, D_MODEL)
    hng = hn_g.reshape(DEPTH, 1, D_MODEL)
    cb = conv_b.reshape(DEPTH, 1, D_MODEL)
    gfin = g_final.reshape(1, D_MODEL)

    mod = _ada_call(jnp.concatenate([c_prompt, c_sample], axis=0), w_ada, b_ada)
    mod_p = mod[:, :n_p].reshape(DEPTH, n_p, 6, 1, D_MODEL)
    mod_s = mod[:, n_p:].reshape(DEPTH, n_s, 6, D_MODEL)

    xp = x_prompt.reshape(n_p * seq_p, D_MODEL)
    xs = x_sample.transpose(1, 0, 2).reshape(rows_s, D_MODEL)
    conv_s_in = state_conv.transpose(0, 2, 1, 3)
    n_s_in = state_n.reshape(DEPTH, n_s, N_HEADS * DK)
    m_s_in = jnp.pad(state_m, ((0, 0), (0, 0), (0, CHUNK - N_HEADS)))

    zeros_conv = jnp.zeros((n_p, CONV_W - 1, D_MODEL), _F32)
    zeros_c = jnp.zeros((n_p, N_HEADS, DK, DV), _F32)
    zeros_n = jnp.zeros((n_p, N_HEADS, DK), _F32)
    zeros_m = jnp.zeros((n_p, 8, CHUNK), _F32)

    p_conv, p_n, p_m, s_conv, s_n, s_m = [], [], [], [], [], []
    p_c = s_c = None
    for l in range(DEPTH):
        final = l == DEPTH - 1

        sh1, sc1, gt1, sh2, sc2, gt2 = (mod_p[l, :, j] for j in range(6))
        mid, gates_t, nconv = _inproj_prompt_call(
            l, xp, sc1, sh1, g1, w_head, w_tail, w_gate, bgate, conv_w, cb, zeros_conv, seq_len=seq_p, tm=tm_p)
        merged, p_c, n1, m1 = _mlstm_prompt_call(
            l, mid, gates_t, hng, zeros_c, zeros_n, zeros_m, p_c, seq_len=seq_p, tb=1024)
        xp = _mlp_call(l, xp, merged, gt1, sc2, sh2, gt2, g2, w_out_b, w_up_b, w_down_b, gfin,
                       tm=512, tf=2048, seq_len=seq_p, final_norm=final)
        tiles_per_seq = seq_p // tm_p
        p_conv.append(nconv[tiles_per_seq - 1::tiles_per_seq]); p_n.append(n1); p_m.append(m1[:, :N_HEADS, 0])

        sh1, sc1, gt1, sh2, sc2, gt2 = (mod_s[l, :, j] for j in range(6))
        mid, gates, nconv = _inproj_sample_call(
            l, xs, sc1, sh1, g1, w_head, w_tail, w_gate, conv_w, cb, conv_s_in, seq_len=seq_s)
        merged3, s_c, n1, m1 = _mlstm_sample_call(
            l, mid.reshape(seq_s, n_s, N_HEADS * HEAD_COLS), gates.reshape(seq_s, n_s, GATE_COLS), bias_row, hng,
            state_C, n_s_in, m_s_in, s_c, bb=16)
        xs = _mlp_call(l, xs, merged3.reshape(rows_s, D_MODEL), gt1, sc2, sh2, gt2, g2,
                       w_out_b, w_up_b, w_down_b, gfin,
                       tm=rows_s, tf=1024, seq_len=seq_s, final_norm=final)
        s_conv.append(nconv.transpose(1, 0, 2))
        s_n.append(n1.reshape(n_s, N_HEADS, DK)); s_m.append(m1[:, :N_HEADS])

    return (xp.reshape(n_p, seq_p, D_MODEL), xs.reshape(seq_s, n_s, D_MODEL).transpose(1, 0, 2),
            jnp.stack(p_conv), p_c, jnp.stack(p_n), jnp.stack(p_m),
            jnp.stack(s_conv), s_c, jnp.stack(s_n), jnp.stack(s_m))
```

```python
import functools

import jax
import jax.numpy as jnp
from jax import lax
from jax.experimental import pallas as pl
from jax.experimental.pallas import tpu as pltpu

D_MODEL = 1024
N_HEADS = 4
DK = 128
DV = 256
D_FF = 4096
DEPTH = 4
CONV_W = 3
GATE_CAP = 15.0
EPS = 1e-6
CHUNK = 128
HEAD_COLS = 1024
GATE_COLS = 256
GATE_F_COL = 128
BIAS_ROWS = 16
GATE_ROWS = 24
EXT_COLS = DV + CHUNK
VMEM_LIMIT = 56 * 1024 * 1024

_OFF_XC, _OFF_BG, _OFF_CG = 0, 1024, 2048
_OFF_Q, _OFF_K, _OFF_V, _OFF_O = 3072, 3584, 4096, 5120
_OFF_IG, _OFF_GA, _OFF_GB = 6144, 6152, 7176
_HEAD_W = _OFF_IG

_M_YA, _M_Q, _M_K, _M_V, _M_OG = 0, 256, 384, 512, 768

_BF = jnp.bfloat16
_F32 = jnp.float32


def _dot(a, b):
    return jnp.dot(a, b, preferred_element_type=_F32)


def _dot_nt(a, b):
    return lax.dot_general(a, b, (((1,), (1,)), ((), ())), preferred_element_type=_F32)


def _sigmoid(x):
    return 0.5 * jnp.tanh(0.5 * x) + 0.5


def _rms_mod(x, g, sc, sh):
    ms = jnp.mean(x * x, axis=-1, keepdims=True)
    return (x * lax.rsqrt(ms + EPS)) * (g * (1.0 + sc)) + sh


def _rep_rows(v, reps):
    return v if reps == 1 else jnp.concatenate([v] * reps, axis=0)


def _params(n_axes):
    return pltpu.CompilerParams(dimension_semantics=("arbitrary",) * n_axes, vmem_limit_bytes=VMEM_LIMIT)


def _ada_kernel(c_ref, w_ref, b_ref, o_ref):
    c = c_ref[...]
    a = (c * _sigmoid(c)).astype(_BF)
    o_ref[0] = _dot(a, w_ref[0].astype(_BF)) + b_ref[0]


def _ada_call(c_all, w_ada, b_ada):
    n_rows = c_all.shape[0]
    tn = 1024
    return pl.pallas_call(
        _ada_kernel,
        grid=(DEPTH, 6 * D_MODEL // tn),
        in_specs=[
            pl.BlockSpec((n_rows, D_MODEL), lambda l, j: (0, 0)),
            pl.BlockSpec((1, D_MODEL, tn), lambda l, j: (l, 0, j)),
            pl.BlockSpec((1, 1, tn), lambda l, j: (l, 0, j)),
        ],
        out_specs=pl.BlockSpec((1, n_rows, tn), lambda l, j: (l, 0, j)),
        out_shape=jax.ShapeDtypeStruct((DEPTH, n_rows, 6 * D_MODEL), _F32),
        compiler_params=_params(2),
        name="ada_mod",
    )(c_all, w_ada, b_ada.reshape(DEPTH, 1, 6 * D_MODEL))


def _cast_w_in_kernel(w_ref, head_ref, tail_ref, gate_ref):
    w = w_ref[0]
    head_ref[0] = w[:, :_HEAD_W].astype(_BF)
    tail_ref[0] = w[:, _OFF_GA:_OFF_GA + 2 * D_MODEL].astype(_BF)
    g = w[:, _OFF_IG:_OFF_IG + CHUNK]
    lane = lax.broadcasted_iota(jnp.int32, g.shape, 1)
    g_in = jnp.where(lane < N_HEADS, g, 0.0)
    g_fg = jnp.where(lane < N_HEADS, pltpu.roll(g, CHUNK - N_HEADS, 1), 0.0)
    gate_ref[0] = jnp.concatenate([g_in, g_fg], axis=1).astype(_BF)


def _cast_w_in_call(w_in):
    rb = 128
    n_in = w_in.shape[-1]
    return pl.pallas_call(
        _cast_w_in_kernel,
        grid=(DEPTH, D_MODEL // rb),
        in_specs=[pl.BlockSpec((1, rb, n_in), lambda l, r: (l, r, 0))],
        out_specs=[
            pl.BlockSpec((1, rb, _HEAD_W), lambda l, r: (l, r, 0)),
            pl.BlockSpec((1, rb, 2 * D_MODEL), lambda l, r: (l, r, 0)),
            pl.BlockSpec((1, rb, GATE_COLS), lambda l, r: (l, r, 0)),
        ],
        out_shape=[
            jax.ShapeDtypeStruct((DEPTH, D_MODEL, _HEAD_W), _BF),
            jax.ShapeDtypeStruct((DEPTH, D_MODEL, 2 * D_MODEL), _BF),
            jax.ShapeDtypeStruct((DEPTH, D_MODEL, GATE_COLS), _BF),
        ],
        compiler_params=_params(2),
        name="cast_w_in",
    )(w_in)


def _w_in_specs(layer, idx):
    def spec(width, first_block):
        return pl.BlockSpec((1, D_MODEL, width), lambda *g: (layer, 0, first_block + idx(*g)))
    head = [spec(256, _OFF_XC // 256), spec(256, _OFF_BG // 256), spec(256, _OFF_CG // 256),
            spec(128, _OFF_Q // 128), spec(128, _OFF_K // 128), spec(256, _OFF_V // 256),
            spec(256, _OFF_O // 256)]
    tail = [spec(256, 0), spec(256, (_OFF_GB - _OFF_GA) // 256)]
    return head, tail


def _scan_lanes(x, combine, identity, seg_len):
    pos = lax.broadcasted_iota(jnp.int32, x.shape, 1) % seg_len
    shift = 1
    while shift < seg_len:
        x = combine(x, jnp.where(pos >= shift, pltpu.roll(x, shift, 1), identity))
        shift *= 2
    return x


def _softcap(a):
    return GATE_CAP * jnp.tanh(a / GATE_CAP)


def _log_sigmoid(x):
    return -(jnp.maximum(-x, 0.0) + jnp.log(1.0 + jnp.exp(-jnp.abs(x))))


def _gate_rows_prompt(gates, bias, gt_ref, first_chunk):
    n_chunks = gates.shape[0] // CHUNK
    gates_t = gates.T

    def stack(r0):
        return jnp.concatenate([gates_t[r0:r0 + 8, c * CHUNK:(c + 1) * CHUNK] for c in range(n_chunks)], axis=0)

    i_pre = _softcap(stack(0) + jnp.concatenate([bias[0:8]] * n_chunks, axis=0))
    f_pre = _softcap(stack(GATE_F_COL) + jnp.concatenate([bias[8:16]] * n_chunks, axis=0))
    b = _scan_lanes(_log_sigmoid(f_pre), jnp.add, 0.0, CHUNK)
    g = i_pre - b
    cm = _scan_lanes(g, jnp.maximum, -3e38, CHUNK)
    for c in range(n_chunks):
        gt_ref[first_chunk + c, 0:8] = b[c * 8:(c + 1) * 8]
        gt_ref[first_chunk + c, 8:16] = g[c * 8:(c + 1) * 8]
        gt_ref[first_chunk + c, 16:24] = cm[c * 8:(c + 1) * 8]


def _inproj_project(h, w_refs):
    wxc, wbg, wcg, wq, wk, wv, wo, wga, wgb = w_refs
    xc = _dot(h, wxc[0])
    cg = _dot(h, wcg[0])
    qk = _dot(h, jnp.concatenate([wq[0], wk[0]], axis=1))
    return dict(u=cg * xc, bg=_dot(h, wbg[0]), ga=_dot(h, wga[0]), qk=qk,
                v=_dot(h, wv[0]), o=_dot(h, wo[0]), gb=_dot(h, wgb[0]))


def _inproj_epilogue(p, conv, mid_ref, out_dtype):
    mid_ref[:, _M_YA:_M_YA + 256] = (_sigmoid(p["ga"]) * (p["bg"] * conv)).astype(out_dtype)
    mid_ref[:, _M_Q:_M_Q + DK] = p["qk"][:, :DK].astype(out_dtype)
    mid_ref[:, _M_K:_M_K + DK] = (p["qk"][:, DK:] * (DK ** -0.5)).astype(out_dtype)
    mid_ref[:, _M_V:_M_V + 256] = p["v"].astype(out_dtype)
    mid_ref[:, _M_OG:_M_OG + 256] = (_sigmoid(p["o"]) * _sigmoid(p["gb"])).astype(out_dtype)


_RAW_COLS = 8 * 256


def _inproj_prompt_kernel(x_ref, sc_ref, sh_ref, g_ref, wxc, wbg, wcg, wq, wk, wv, wo, wga, wgb, wg_ref,
                          bg_ref, cw_ref, cb_ref, cprev_ref, mid_ref, gt_ref, nconv_ref,
                          h_scr, gate_scr, raw, ubuf, carry, *, tiles_per_seq, n_work):
    j = pl.program_id(0)
    tm = x_ref.shape[0]
    cur = jnp.minimum(j, n_work - 1)
    s = cur % N_HEADS
    prev = jnp.maximum(j - 1, 0)
    tile_p = prev // N_HEADS
    s_p = prev % N_HEADS
    chunks_per_step = (tm // CHUNK) // N_HEADS

    @pl.when(j == 0)
    def _():
        raw[1] = jnp.zeros(raw.shape[1:], _F32)
        carry[...] = jnp.zeros(carry.shape, _F32)

    @pl.when(jnp.logical_and(s == 0, j < n_work))
    def _():
        h = _rms_mod(x_ref[...], g_ref[0], sc_ref[0], sh_ref[0]).astype(_BF)
        h_scr[...] = h
        gate_scr[...] = _dot(h, wg_ref[0])

    def step(slot):
        h = h_scr[...]
        for idx, w in enumerate((wxc[0], wcg[0], wbg[0], wga[0], jnp.concatenate([wq[0], wk[0]], axis=1),
                                 wv[0], wo[0], wgb[0])):
            raw[slot, :, idx * 256:(idx + 1) * 256] = _dot(h, w)

        rows0 = pl.multiple_of(s * (chunks_per_step * CHUNK), CHUNK)
        _gate_rows_prompt(gate_scr[pl.ds(rows0, chunks_per_step * CHUNK), :], bg_ref[0], gt_ref,
                          s * chunks_per_step)

        def piece(idx):
            return raw[1 - slot, :, idx * 256:(idx + 1) * 256]

        u = piece(1) * piece(0)
        seq_start = (tile_p % tiles_per_seq) == 0
        ubuf[6:8, :] = jnp.where(seq_start, cprev_ref[0], carry[s_p])
        ubuf[8:8 + tm, :] = u
        cw = cw_ref[0]
        conv = (cb_ref[0] + cw[0:1] * ubuf[6:6 + tm, :] + cw[1:2] * ubuf[7:7 + tm, :] + cw[2:3] * u)
        last2 = u[tm - 2:tm, :]
        carry[s_p] = last2
        nconv_ref[0] = last2
        _inproj_epilogue(dict(bg=piece(2), ga=piece(3), qk=piece(4), v=piece(5), o=piece(6), gb=piece(7)),
                         conv, mid_ref, _BF)

    for parity in range(2):
        pl.when(j % 2 == parity)(functools.partial(step, parity))


def _inproj_prompt_call(layer, x, sc, sh, g, w_head, w_tail, w_gate, bgate, conv_w, conv_b, conv_prev,
                        *, seq_len, tm):
    m_rows = x.shape[0]
    n_tiles = m_rows // tm
    tiles_per_seq = seq_len // tm
    n_work = n_tiles * N_HEADS

    def cur(j):
        return jnp.minimum(j, n_work - 1)

    def prev(j):
        return jnp.maximum(j - 1, 0)

    kern = functools.partial(_inproj_prompt_kernel, tiles_per_seq=tiles_per_seq, n_work=n_work)
    head_specs, tail_specs = _w_in_specs(layer, lambda j: cur(j) % N_HEADS)
    seq_of_cur = lambda j: (cur(j) // N_HEADS // tiles_per_seq, 0, 0)
    return pl.pallas_call(
        kern,
        grid=(n_work + 1,),
        in_specs=[
            pl.BlockSpec((tm, D_MODEL), lambda j: (cur(j) // N_HEADS, 0)),
            pl.BlockSpec((1, 1, D_MODEL), seq_of_cur),
            pl.BlockSpec((1, 1, D_MODEL), seq_of_cur),
            pl.BlockSpec((1, 1, D_MODEL), lambda j: (layer, 0, 0)),
            *head_specs, *tail_specs,
            pl.BlockSpec((1, D_MODEL, GATE_COLS), lambda j: (layer, 0, 0)),
            pl.BlockSpec((1, BIAS_ROWS, CHUNK), lambda j: (layer, 0, 0)),
            pl.BlockSpec((1, CONV_W, 256), lambda j: (layer, 0, prev(j) % N_HEADS)),
            pl.BlockSpec((1, 1, 256), lambda j: (layer, 0, prev(j) % N_HEADS)),
            pl.BlockSpec((1, CONV_W - 1, 256),
                         lambda j: (prev(j) // N_HEADS // tiles_per_seq, 0, prev(j) % N_HEADS)),
        ],
        out_specs=[
            pl.BlockSpec((tm, HEAD_COLS), lambda j: (prev(j) // N_HEADS, prev(j) % N_HEADS)),
            pl.BlockSpec((tm // CHUNK, GATE_ROWS, CHUNK), lambda j: (cur(j) // N_HEADS, 0, 0)),
            pl.BlockSpec((1, CONV_W - 1, 256), lambda j: (prev(j) // N_HEADS, 0, prev(j) % N_HEADS)),
        ],
        out_shape=[
            jax.ShapeDtypeStruct((m_rows, N_HEADS * HEAD_COLS), _BF),
            jax.ShapeDtypeStruct((m_rows // CHUNK, GATE_ROWS, CHUNK), _F32),
            jax.ShapeDtypeStruct((n_tiles, CONV_W - 1, D_MODEL), _F32),
        ],
        scratch_shapes=[
            pltpu.VMEM((tm, D_MODEL), _BF),
            pltpu.VMEM((tm, GATE_COLS), _F32),
            pltpu.VMEM((2, tm, _RAW_COLS), _F32),
            pltpu.VMEM((tm + 8, 256), _F32),
            pltpu.VMEM((N_HEADS, CONV_W - 1, 256), _F32),
        ],
        compiler_params=_params(1),
        name="inproj_prompt",
    )(x, sc, sh, g, *([w_head] * 7), *([w_tail] * 2), w_gate, bgate, conv_w, conv_b, conv_prev)


def _inproj_sample_kernel(x_ref, sc_ref, sh_ref, g_ref, wxc, wbg, wcg, wq, wk, wv, wo, wga, wgb, wg_ref,
                          cw_ref, cb_ref, cprev_ref, mid_ref, gt_ref, nconv_ref, h_scr, *, seq_len):
    s = pl.program_id(0)
    n_b = sc_ref.shape[0]

    @pl.when(s == 0)
    def _():
        h = _rms_mod(x_ref[...], g_ref[0], _rep_rows(sc_ref[...], seq_len), _rep_rows(sh_ref[...], seq_len))
        h = h.astype(_BF)
        h_scr[...] = h
        gt_ref[...] = _dot(h, wg_ref[0])

    p = _inproj_project(h_scr[...], (wxc, wbg, wcg, wq, wk, wv, wo, wga, wgb))
    u = p["u"]
    prev0 = cprev_ref[0, 0]
    prev1 = cprev_ref[0, 1]
    p1 = jnp.concatenate([prev1, u[0:(seq_len - 1) * n_b]], axis=0)
    p2 = jnp.concatenate([prev0, prev1, u[0:(seq_len - 2) * n_b]], axis=0)
    cw = cw_ref[0]
    conv = cb_ref[0] + cw[0:1] * p2 + cw[1:2] * p1 + cw[2:3] * u
    nconv_ref[0] = u[(seq_len - 2) * n_b:(seq_len - 1) * n_b]
    nconv_ref[1] = u[(seq_len - 1) * n_b:seq_len * n_b]
    _inproj_epilogue(p, conv, mid_ref, _F32)


def _inproj_sample_call(layer, x, sc, sh, g, w_head, w_tail, w_gate, conv_w, conv_b, conv_prev, *, seq_len):
    tm = x.shape[0]
    n_b = tm // seq_len
    kern = functools.partial(_inproj_sample_kernel, seq_len=seq_len)
    head_specs, tail_specs = _w_in_specs(layer, lambda s: s)
    return pl.pallas_call(
        kern,
        grid=(N_HEADS,),
        in_specs=[
            pl.BlockSpec((tm, D_MODEL), lambda s: (0, 0)),
            pl.BlockSpec((n_b, D_MODEL), lambda s: (0, 0)),
            pl.BlockSpec((n_b, D_MODEL), lambda s: (0, 0)),
            pl.BlockSpec((1, 1, D_MODEL), lambda s: (layer, 0, 0)),
            *head_specs, *tail_specs,
            pl.BlockSpec((1, D_MODEL, GATE_COLS), lambda s: (layer, 0, 0)),
            pl.BlockSpec((1, CONV_W, 256), lambda s: (layer, 0, s)),
            pl.BlockSpec((1, 1, 256), lambda s: (layer, 0, s)),
            pl.BlockSpec((1, CONV_W - 1, n_b, 256), lambda s: (layer, 0, 0, s)),
        ],
        out_specs=[
            pl.BlockSpec((tm, HEAD_COLS), lambda s: (0, s)),
            pl.BlockSpec((tm, GATE_COLS), lambda s: (0, 0)),
            pl.BlockSpec((CONV_W - 1, n_b, 256), lambda s: (0, 0, s)),
        ],
        out_shape=[
            jax.ShapeDtypeStruct((tm, N_HEADS * HEAD_COLS), _F32),
            jax.ShapeDtypeStruct((tm, GATE_COLS), _F32),
            jax.ShapeDtypeStruct((CONV_W - 1, n_b, D_MODEL), _F32),
        ],
        scratch_shapes=[pltpu.VMEM((tm, D_MODEL), _BF)],
        compiler_params=_params(1),
        name="inproj_sample",
    )(x, sc, sh, g, *([w_head] * 7), *([w_tail] * 2), w_gate, conv_w, conv_b, conv_prev)


def _causal_mask(lq):
    row_id = lax.broadcasted_iota(jnp.int32, (lq, CHUNK), 0)
    col_id = lax.broadcasted_iota(jnp.int32, (lq, CHUNK), 1)
    return col_id <= row_id


def _chunk_local(a, causal):
    dmat = jnp.where(causal, jnp.exp(a["g_row"] - a["m_col"]), 0.0)
    s_bf = (_dot_nt(a["q"], a["k"]) * dmat).astype(_BF)
    kw = a["k"].astype(_F32) * a["ws_col"]
    return s_bf, _dot(kw.T.astype(_BF), a["v_ext"])


def _chunk_output(a, s_bf, d_state, cx_ref, h, hng_h):
    q_inter = (a["q"].astype(_F32) * a["inter_col"]).astype(_BF)
    cx = cx_ref[h]
    rhs = jnp.concatenate([a["v_ext"], cx.astype(_BF)], axis=0)
    res = _dot(jnp.concatenate([s_bf, q_inter], axis=1), rhs)
    cx_ref[h] = a["dec"] * cx + d_state
    num = res[:, :DV]
    rden = 1.0 / jnp.maximum(jnp.abs(res[:, DV:]), a["em_col"])
    sq = jnp.sum(num * num, axis=-1, keepdims=True) * (1.0 / DV)
    scale = rden * lax.rsqrt(rden * rden * sq + EPS)
    hn = num * jnp.concatenate([scale, scale], axis=1) * hng_h
    return a["ya"] + a["og"] * hn


def _mlstm_prompt_kernel(mid_ref, gt_ref, hng_ref, c0_ref, n0_ref, m0_ref, cbuf_ref,
                         mg_ref, c_ref, n_ref, m_ref, cx_ref):
    del cbuf_ref
    t = pl.program_id(1)

    @pl.when(t == 0)
    def _():
        m_ref[...] = m0_ref[...]
        for h in range(N_HEADS):
            cx_ref[h, :, 0:DV] = c0_ref[0, h]
            cx_ref[h, :, DV:EXT_COLS] = jnp.broadcast_to(n0_ref[0, h:h + 1, :], (CHUNK, DK)).T

    n_chunks = mid_ref.shape[0] // CHUNK
    rows = n_chunks * 8
    hng = hng_ref[0]
    gates = gt_ref[...]
    b = gates[:, 0:8, :].reshape(rows, CHUNK)
    g = gates[:, 8:16, :].reshape(rows, CHUNK)
    cm = gates[:, 16:24, :].reshape(rows, CHUNK)
    b_last = jnp.broadcast_to(b[:, CHUNK - 1:CHUNK], (rows, CHUNK))
    cm_last = jnp.broadcast_to(cm[:, CHUNK - 1:CHUNK], (rows, CHUNK))

    m_prev = m_ref[0]
    m_prevs = []
    for c in range(n_chunks):
        m_prevs.append(m_prev)
        m_prev = b_last[c * 8:(c + 1) * 8] + jnp.maximum(cm_last[c * 8:(c + 1) * 8], m_prev)
    m_ref[0] = m_prev
    m_prev_all = jnp.concatenate(m_prevs, axis=0)

    m_run = jnp.maximum(cm, m_prev_all)
    inter = jnp.exp(m_prev_all - m_run)
    em = jnp.exp(-(b + m_run))
    m_last = jnp.maximum(cm_last, m_prev_all)
    ws = jnp.exp(g - m_last)
    decay = jnp.exp(m_prev_all - m_last)

    causal = _causal_mask(CHUNK)
    ones = jnp.ones((CHUNK, CHUNK), _BF)
    group = CHUNK // 32
    cols_of = []
    for c0 in range(0, n_chunks, group):
        pieces = []
        for c in range(c0, min(c0 + group, n_chunks)):
            sl = slice(c * 8, (c + 1) * 8)
            pieces += [m_run[sl], inter[sl], em[sl], ws[sl]]
        if len(pieces) * 8 < CHUNK:
            pieces.append(jnp.zeros((CHUNK - len(pieces) * 8, CHUNK), _F32))
        cols_of.append(jnp.concatenate(pieces, axis=0).T)

    def chunk_inputs(c):
        cols = cols_of[c // group]
        base = (c % group) * 32
        r0 = c * CHUNK
        heads = []
        for h in range(N_HEADS):
            mb = h * HEAD_COLS
            row = c * 8 + h
            v = mid_ref[r0:r0 + CHUNK, mb + _M_V:mb + _M_V + 256]
            heads.append(dict(
                q=mid_ref[r0:r0 + CHUNK, mb + _M_Q:mb + _M_Q + 128],
                k=mid_ref[r0:r0 + CHUNK, mb + _M_K:mb + _M_K + 128],
                v_ext=jnp.concatenate([v, ones], axis=1),
                ya=mid_ref[r0:r0 + CHUNK, mb + _M_YA:mb + _M_YA + 256].astype(_F32),
                og=mid_ref[r0:r0 + CHUNK, mb + _M_OG:mb + _M_OG + 256].astype(_F32),
                g_row=g[row:row + 1, :],
                m_col=cols[:, base + h:base + h + 1],
                inter_col=cols[:, base + 8 + h:base + 9 + h],
                em_col=cols[:, base + 16 + h:base + 17 + h],
                ws_col=cols[:, base + 24 + h:base + 25 + h],
                dec=decay[row:row + 1, 0:1]))
        return heads

    for c in range(n_chunks):
        outs = []
        for h, a in enumerate(chunk_inputs(c)):
            s_bf, d_state = _chunk_local(a, causal)
            outs.append(_chunk_output(a, s_bf, d_state, cx_ref, h, hng[:, h * DV:(h + 1) * DV]))
        mg_ref[c * CHUNK:(c + 1) * CHUNK, :] = jnp.concatenate(outs, axis=1).astype(_BF)

    @pl.when(t == pl.num_programs(1) - 1)
    def _():
        for h in range(N_HEADS):
            c_ref[0, 0, h] = cx_ref[h, :, 0:DV]
            n_ref[0, h:h + 1, :] = cx_ref[h, :, DV:EXT_COLS].T[0:1, :]


def _mlstm_prompt_call(layer, mid, gates_t, hng, c0, n0, m0, c_buf, *, seq_len, tb):
    m_rows = mid.shape[0]
    n_seq = m_rows // seq_len
    steps = seq_len // tb
    aliases = {} if c_buf is None else {6: 1}
    if c_buf is None:
        c_buf = jnp.zeros((1,), _F32)
    return pl.pallas_call(
        _mlstm_prompt_kernel,
        grid=(n_seq, steps),
        in_specs=[
            pl.BlockSpec((tb, N_HEADS * HEAD_COLS), lambda b, t: (b * steps + t, 0)),
            pl.BlockSpec((tb // CHUNK, GATE_ROWS, CHUNK), lambda b, t: (b * steps + t, 0, 0)),
            pl.BlockSpec((1, 1, D_MODEL), lambda b, t: (layer, 0, 0)),
            pl.BlockSpec((1, N_HEADS, DK, DV), lambda b, t: (b, 0, 0, 0)),
            pl.BlockSpec((1, N_HEADS, DK), lambda b, t: (b, 0, 0)),
            pl.BlockSpec((1, 8, CHUNK), lambda b, t: (b, 0, 0)),
            pl.BlockSpec(memory_space=pl.ANY),
        ],
        out_specs=[
            pl.BlockSpec((tb, D_MODEL), lambda b, t: (b * steps + t, 0)),
            pl.BlockSpec((1, 1, N_HEADS, DK, DV), lambda b, t: (layer, b, 0, 0, 0)),
            pl.BlockSpec((1, N_HEADS, DK), lambda b, t: (b, 0, 0)),
            pl.BlockSpec((1, 8, CHUNK), lambda b, t: (b, 0, 0)),
        ],
        out_shape=[
            jax.ShapeDtypeStruct((m_rows, D_MODEL), _BF),
            jax.ShapeDtypeStruct((DEPTH, n_seq, N_HEADS, DK, DV), _F32),
            jax.ShapeDtypeStruct((n_seq, N_HEADS, DK), _F32),
            jax.ShapeDtypeStruct((n_seq, 8, CHUNK), _F32),
        ],
        scratch_shapes=[pltpu.VMEM((N_HEADS, DK, EXT_COLS), _F32)],
        input_output_aliases=aliases,
        compiler_params=_params(2),
        name="mlstm_prompt",
    )(mid, gates_t, hng, c0, n0, m0, c_buf)


def _mlstm_sample_kernel(mid_ref, gt_ref, bias_ref, hng_ref, c0_ref, n0_ref, m0_ref, cbuf_ref,
                         mg_ref, c_ref, n_ref, m_ref,
                         qs_scr, kk_scr, wv_scr, rs_scr, dec_scr, lhs_q, lhs_k, lhs_wv):
    del cbuf_ref
    seq_len, bb, _ = mid_ref.shape
    hng = hng_ref[0]
    bias = bias_ref[0]
    m_prev = m0_ref[0]

    b_t, g_t, cm_t = [], [], []
    for t in range(seq_len):
        pre = _softcap(gt_ref[t] + bias)
        logf = _log_sigmoid(pre[:, GATE_F_COL:GATE_F_COL + CHUNK])
        b_t.append(logf if t == 0 else b_t[-1] + logf)
        g_t.append(pre[:, 0:CHUNK] - b_t[-1])
        cm_t.append(g_t[-1] if t == 0 else jnp.maximum(cm_t[-1], g_t[-1]))
    m_run = [jnp.maximum(cm, m_prev) for cm in cm_t]
    inter = [jnp.exp(m_prev - mr) for mr in m_run]
    em = [jnp.exp(-(b + mr)) for b, mr in zip(b_t, m_run)]
    m_last = m_run[-1]
    ws = [jnp.exp(g - m_last) for g in g_t]
    decay = jnp.exp(m_prev - m_last)
    m_ref[...] = b_t[-1] + m_last
    dec_scr[...] = decay

    lhs_q[...] = jnp.zeros(lhs_q.shape, _F32)
    lhs_k[...] = jnp.zeros(lhs_k.shape, _F32)
    lhs_wv[...] = jnp.zeros(lhs_wv.shape, _F32)
    n_all = n0_ref[0]
    den_part = {}
    for h in range(N_HEADS):
        def col(x):
            return x[:, h:h + 1]
        mb = h * HEAD_COLS
        q = [mid_ref[t, :, mb + _M_Q:mb + _M_Q + DK] for t in range(seq_len)]
        k = [mid_ref[t, :, mb + _M_K:mb + _M_K + DK] for t in range(seq_len)]
        v = [mid_ref[t, :, mb + _M_V:mb + _M_V + DV] for t in range(seq_len)]
        n_h = n_all[:, h * DK:(h + 1) * DK]
        n_new = col(decay) * n_h
        for t in range(seq_len):
            qs = q[t] * col(inter[t])
            qs_scr[t, :, h * DK:(h + 1) * DK] = qs
            kk_scr[t, :, h * DK:(h + 1) * DK] = k[t]
            wv_scr[t, :, h * DV:(h + 1) * DV] = col(ws[t]) * v[t]
            n_new = n_new + col(ws[t]) * k[t]
            den = jnp.sum(qs * n_h, axis=-1, keepdims=True)
            num = None
            for s in range(t + 1):
                w = jnp.sum(q[t] * k[s], axis=-1, keepdims=True) * jnp.exp(col(g_t[s]) - col(m_run[t]))
                num = w * v[s] if num is None else num + w * v[s]
                den = den + w
            mg_ref[t, :, h * DV:(h + 1) * DV] = num
            den_part[h, t] = den
        n_ref[:, h * DK:(h + 1) * DK] = n_new

    def body(i, carry):
        for t in range(seq_len):
            lhs_q[t:t + 1, :] = qs_scr[t, pl.ds(i, 1), :]
            lhs_k[t:t + 1, :] = kk_scr[t, pl.ds(i, 1), :]
            lhs_wv[t:t + 1, :] = wv_scr[t, pl.ds(i, 1), :]
        for h in range(N_HEADS):
            c_old = c0_ref[0, i, h]
            r = _dot(lhs_q[:, h * DK:(h + 1) * DK].astype(_BF), c_old.astype(_BF))
            for t in range(seq_len):
                rs_scr[t, pl.ds(i, 1), h * DV:(h + 1) * DV] = r[t:t + 1, :]
            d_c = lax.dot_general(lhs_k[:, h * DK:(h + 1) * DK].astype(_BF),
                                  lhs_wv[:, h * DV:(h + 1) * DV].astype(_BF),
                                  (((0,), (0,)), ((), ())), preferred_element_type=_F32)
            c_ref[0, i, h] = dec_scr[pl.ds(i, 1), h:h + 1] * c_old + d_c
        return carry

    lax.fori_loop(0, bb, body, 0)

    for h in range(N_HEADS):
        mb = h * HEAD_COLS
        for t in range(seq_len):
            num = mg_ref[t, :, h * DV:(h + 1) * DV] + rs_scr[t, :, h * DV:(h + 1) * DV]
            hraw = num * (1.0 / jnp.maximum(jnp.abs(den_part[h, t]), em[t][:, h:h + 1]))
            ms = jnp.mean(hraw * hraw, axis=-1, keepdims=True)
            hn = hraw * lax.rsqrt(ms + EPS) * hng[:, h * DV:(h + 1) * DV]
            ya = mid_ref[t, :, mb + _M_YA:mb + _M_YA + DV]
            og = mid_ref[t, :, mb + _M_OG:mb + _M_OG + DV]
            mg_ref[t, :, h * DV:(h + 1) * DV] = ya + og * hn


def _mlstm_sample_call(layer, mid3, gates3, bias, hng, c0, n0, m0, c_buf, *, bb):
    seq_len, n_seq, _ = mid3.shape
    aliases = {} if c_buf is None else {7: 1}
    if c_buf is None:
        c_buf = jnp.zeros((1,), _F32)
    return pl.pallas_call(
        _mlstm_sample_kernel,
        grid=(n_seq // bb,),
        in_specs=[
            pl.BlockSpec((seq_len, bb, N_HEADS * HEAD_COLS), lambda i: (0, i, 0)),
            pl.BlockSpec((seq_len, bb, GATE_COLS), lambda i: (0, i, 0)),
            pl.BlockSpec((1, 1, GATE_COLS), lambda i: (layer, 0, 0)),
            pl.BlockSpec((1, 1, D_MODEL), lambda i: (layer, 0, 0)),
            pl.BlockSpec((1, bb, N_HEADS, DK, DV), lambda i: (layer, i, 0, 0, 0)),
            pl.BlockSpec((1, bb, N_HEADS * DK), lambda i: (layer, i, 0)),
            pl.BlockSpec((1, bb, CHUNK), lambda i: (layer, i, 0)),
            pl.BlockSpec(memory_space=pl.ANY),
        ],
        out_specs=[
            pl.BlockSpec((seq_len, bb, D_MODEL), lambda i: (0, i, 0)),
            pl.BlockSpec((1, bb, N_HEADS, DK, DV), lambda i: (layer, i, 0, 0, 0)),
            pl.BlockSpec((bb, N_HEADS * DK), lambda i: (i, 0)),
            pl.BlockSpec((bb, CHUNK), lambda i: (i, 0)),
        ],
        out_shape=[
            jax.ShapeDtypeStruct((seq_len, n_seq, D_MODEL), _F32),
            jax.ShapeDtypeStruct((DEPTH, n_seq, N_HEADS, DK, DV), _F32),
            jax.ShapeDtypeStruct((n_seq, N_HEADS * DK), _F32),
            jax.ShapeDtypeStruct((n_seq, CHUNK), _F32),
        ],
        scratch_shapes=[
            pltpu.VMEM((seq_len, bb, N_HEADS * DK), _F32),
            pltpu.VMEM((seq_len, bb, N_HEADS * DK), _F32),
            pltpu.VMEM((seq_len, bb, N_HEADS * DV), _F32),
            pltpu.VMEM((seq_len, bb, N_HEADS * DV), _F32),
            pltpu.VMEM((bb, CHUNK), _F32),
            pltpu.VMEM((8, N_HEADS * DK), _F32),
            pltpu.VMEM((8, N_HEADS * DK), _F32),
            pltpu.VMEM((8, N_HEADS * DV), _F32),
        ],
        input_output_aliases=aliases,
        compiler_params=_params(1),
        name="mlstm_sample",
    )(mid3, gates3, bias, hng, c0, n0, m0, c_buf)


def _mlp_kernel(x_ref, mg_ref, gt1_ref, sc2_ref, sh2_ref, gt2_ref, g2_ref, wo_ref, wu_ref, wd_ref, gf_ref,
                o_ref, xmid, h2, acc, *, mod_reps, final_norm):
    f = pl.program_id(1)

    def mod(ref):
        return ref[0] if mod_reps is None else _rep_rows(ref[...], mod_reps)

    @pl.when(f == 0)
    def _():
        mix = _dot(mg_ref[...].astype(_BF), wo_ref[0])
        xm = x_ref[...] + mod(gt1_ref) * mix
        xmid[...] = xm
        h2[...] = _rms_mod(xm, g2_ref[0], mod(sc2_ref), mod(sh2_ref)).astype(_BF)
        acc[...] = jnp.zeros(acc.shape, _F32)

    a = jnp.maximum(_dot(h2[...], wu_ref[0]), 0.0)
    acc[...] += _dot((a * a).astype(_BF), wd_ref[0])

    @pl.when(f == pl.num_programs(1) - 1)
    def _():
        y = xmid[...] + mod(gt2_ref) * acc[...]
        if final_norm:
            ms = jnp.mean(y * y, axis=-1, keepdims=True)
            y = y * lax.rsqrt(ms + EPS) * gf_ref[...]
        o_ref[...] = y


def _mlp_call(layer, x, merged, gt1, sc2, sh2, gt2, g2, w_out, w_up, w_down, g_final, *,
              tm, tf, seq_len, final_norm):
    m_rows = x.shape[0]
    if seq_len >= tm:
        tiles_per_seq = seq_len // tm
        mod_reps = None
        mod_spec = pl.BlockSpec((1, 1, D_MODEL), lambda i, f: (i // tiles_per_seq, 0, 0))
    else:
        mod_reps = seq_len
        mod_spec = pl.BlockSpec((tm // seq_len, D_MODEL), lambda i, f: (0, 0))
    kern = functools.partial(_mlp_kernel, mod_reps=mod_reps, final_norm=final_norm)
    return pl.pallas_call(
        kern,
        grid=(m_rows // tm, D_FF // tf),
        in_specs=[
            pl.BlockSpec((tm, D_MODEL), lambda i, f: (i, 0)),
            pl.BlockSpec((tm, D_MODEL), lambda i, f: (i, 0)),
            mod_spec, mod_spec, mod_spec, mod_spec,
            pl.BlockSpec((1, 1, D_MODEL), lambda i, f: (layer, 0, 0)),
            pl.BlockSpec((1, D_MODEL, D_MODEL), lambda i, f: (layer, 0, 0)),
            pl.BlockSpec((1, D_MODEL, tf), lambda i, f: (layer, 0, f)),
            pl.BlockSpec((1, tf, D_MODEL), lambda i, f: (layer, f, 0)),
            pl.BlockSpec((1, D_MODEL), lambda i, f: (0, 0)),
        ],
        out_specs=pl.BlockSpec((tm, D_MODEL), lambda i, f: (i, 0)),
        out_shape=jax.ShapeDtypeStruct((m_rows, D_MODEL), _F32),
        scratch_shapes=[
            pltpu.VMEM((tm, D_MODEL), _F32),
            pltpu.VMEM((tm, D_MODEL), _BF),
            pltpu.VMEM((tm, D_MODEL), _F32),
        ],
        compiler_params=_params(2),
        name="outproj_mlp",
    )(x, merged, gt1, sc2, sh2, gt2, g2, w_out, w_up, w_down, g_final)


def kernel(x_prompt, x_sample, state_conv, state_C, state_n, state_m, c_prompt, c_sample,
           w_ada, b_ada, g_norm1, g_norm2, w_in, b_gate, conv_w, conv_b, hn_g, w_out, w_up, w_down, g_final):
    n_p, seq_p, _ = x_prompt.shape
    n_s, seq_s, _ = x_sample.shape
    rows_s = n_s * seq_s
    tm_p = 1024

    w_head, w_tail, w_gate = _cast_w_in_call(w_in)
    w_out_b = w_out.astype(_BF)
    w_up_b = w_up.astype(_BF)
    w_down_b = w_down.astype(_BF)
    zb = jnp.zeros((DEPTH, 4), _F32)
    bgate = jnp.broadcast_to(
        jnp.concatenate([b_gate[:, :N_HEADS], zb, b_gate[:, N_HEADS:], zb], axis=-1)[:, :, None],
        (DEPTH, BIAS_ROWS, CHUNK))
    zb = jnp.zeros((DEPTH, GATE_F_COL - N_HEADS), _F32)
    bias_row = jnp.concatenate([b_gate[:, :N_HEADS], zb, b_gate[:, N_HEADS:], zb], axis=-1)[:, None, :]
    g1 = g_norm1.reshape(DEPTH, 1, D_MODEL)
    g2 = g_norm2.reshape(DEPTH, 1, D_MODEL)
    hng = hn_g.reshape(DEPTH, 1, D_MODEL)
    cb = conv_b.reshape(DEPTH, 1, D_MODEL)
    gfin = g_final.reshape(1, D_MODEL)

    mod = _ada_call(jnp.concatenate([c_prompt, c_sample], axis=0), w_ada, b_ada)
    mod_p = mod[:, :n_p].reshape(DEPTH, n_p, 6, 1, D_MODEL)
    mod_s = mod[:, n_p:].reshape(DEPTH, n_s, 6, D_MODEL)

    xp = x_prompt.reshape(n_p * seq_p, D_MODEL)
    xs = x_sample.transpose(1, 0, 2).reshape(rows_s, D_MODEL)
    conv_s_in = state_conv.transpose(0, 2, 1, 3)
    n_s_in = state_n.reshape(DEPTH, n_s, N_HEADS * DK)
    m_s_in = jnp.pad(state_m, ((0, 0), (0, 0), (0, CHUNK - N_HEADS)))

    zeros_conv = jnp.zeros((n_p, CONV_W - 1, D_MODEL), _F32)
    zeros_c = jnp.zeros((n_p, N_HEADS, DK, DV), _F32)
    zeros_n = jnp.zeros((n_p, N_HEADS, DK), _F32)
    zeros_m = jnp.zeros((n_p, 8, CHUNK), _F32)

    p_conv, p_n, p_m, s_conv, s_n, s_m = [], [], [], [], [], []
    p_c = s_c = None
    for l in range(DEPTH):
        final = l == DEPTH - 1

        sh1, sc1, gt1, sh2, sc2, gt2 = (mod_p[l, :, j] for j in range(6))
        mid, gates_t, nconv = _inproj_prompt_call(
            l, xp, sc1, sh1, g1, w_head, w_tail, w_gate, bgate, conv_w, cb, zeros_conv, seq_len=seq_p, tm=tm_p)
        merged, p_c, n1, m1 = _mlstm_prompt_call(
            l, mid, gates_t, hng, zeros_c, zeros_n, zeros_m, p_c, seq_len=seq_p, tb=1024)
        xp = _mlp_call(l, xp, merged, gt1, sc2, sh2, gt2, g2, w_out_b, w_up_b, w_down_b, gfin,
                       tm=512, tf=2048, seq_len=seq_p, final_norm=final)
        tiles_per_seq = seq_p // tm_p
        p_conv.append(nconv[tiles_per_seq - 1::tiles_per_seq]); p_n.append(n1); p_m.append(m1[:, :N_HEADS, 0])

        sh1, sc1, gt1, sh2, sc2, gt2 = (mod_s[l, :, j] for j in range(6))
        mid, gates, nconv = _inproj_sample_call(
            l, xs, sc1, sh1, g1, w_head, w_tail, w_gate, conv_w, cb, conv_s_in, seq_len=seq_s)
        merged3, s_c, n1, m1 = _mlstm_sample_call(
            l, mid.reshape(seq_s, n_s, N_HEADS * HEAD_COLS), gates.reshape(seq_s, n_s, GATE_COLS), bias_row, hng,
            state_C, n_s_in, m_s_in, s_c, bb=16)
        xs = _mlp_call(l, xs, merged3.reshape(rows_s, D_MODEL), gt1, sc2, sh2, gt2, g2,
                       w_out_b, w_up_b, w_down_b, gfin,
                       tm=rows_s, tf=1024, seq_len=seq_s, final_norm=final)
        s_conv.append(nconv.transpose(1, 0, 2))
        s_n.append(n1.reshape(n_s, N_HEADS, DK)); s_m.append(m1[:, :N_HEADS])

    return (xp.reshape(n_p, seq_p, D_MODEL), xs.reshape(seq_s, n_s, D_MODEL).transpose(1, 0, 2),
            jnp.stack(p_conv), p_c, jnp.stack(p_n), jnp.stack(p_m),
            jnp.stack(s_conv), s_c, jnp.stack(s_n), jnp.stack(s_m))
```

```python
import functools

import jax
import jax.numpy as jnp
from jax import lax
from jax.experimental import pallas as pl
from jax.experimental.pallas import tpu as pltpu

D_MODEL = 1024
N_HEADS = 4
DK = 128
DV = 256
D_FF = 4096
DEPTH = 4
CONV_W = 3
GATE_CAP = 15.0
EPS = 1e-6
CHUNK = 128
HEAD_COLS = 1024
GATE_COLS = 256
GATE_F_COL = 128
BIAS_ROWS = 16
GATE_ROWS = 24
EXT_COLS = DV + CHUNK
VMEM_LIMIT = 56 * 1024 * 1024

_OFF_XC, _OFF_BG, _OFF_CG = 0, 1024, 2048
_OFF_Q, _OFF_K, _OFF_V, _OFF_O = 3072, 3584, 4096, 5120
_OFF_IG, _OFF_GA, _OFF_GB = 6144, 6152, 7176

_M_YA, _M_Q, _M_K, _M_V, _M_OG = 0, 256, 384, 512, 768

_BF = jnp.bfloat16
_F32 = jnp.float32


def _dot(a, b):
    return jnp.dot(a, b, preferred_element_type=_F32)


def _dot_nt(a, b):
    return lax.dot_general(a, b, (((1,), (1,)), ((), ())), preferred_element_type=_F32)


def _sigmoid(x):
    return 0.5 * jnp.tanh(0.5 * x) + 0.5


def _rms_mod(x, g, sc, sh):
    ms = jnp.mean(x * x, axis=-1, keepdims=True)
    return (x * lax.rsqrt(ms + EPS)) * (g * (1.0 + sc)) + sh


def _rep_rows(v, reps):
    return v if reps == 1 else jnp.concatenate([v] * reps, axis=0)


def _params(n_axes):
    return pltpu.CompilerParams(dimension_semantics=("arbitrary",) * n_axes, vmem_limit_bytes=VMEM_LIMIT)


def _ada_kernel(c_ref, w_ref, b_ref, o_ref):
    c = c_ref[...]
    a = (c * _sigmoid(c)).astype(_BF)
    o_ref[0] = _dot(a, w_ref[0].astype(_BF)) + b_ref[0]


def _ada_call(c_all, w_ada, b_ada):
    n_rows = c_all.shape[0]
    tn = 1024
    return pl.pallas_call(
        _ada_kernel,
        grid=(DEPTH, 6 * D_MODEL // tn),
        in_specs=[
            pl.BlockSpec((n_rows, D_MODEL), lambda l, j: (0, 0)),
            pl.BlockSpec((1, D_MODEL, tn), lambda l, j: (l, 0, j)),
            pl.BlockSpec((1, 1, tn), lambda l, j: (l, 0, j)),
        ],
        out_specs=pl.BlockSpec((1, n_rows, tn), lambda l, j: (l, 0, j)),
        out_shape=jax.ShapeDtypeStruct((DEPTH, n_rows, 6 * D_MODEL), _F32),
        compiler_params=_params(2),
        name="ada_mod",
    )(c_all, w_ada, b_ada.reshape(DEPTH, 1, 6 * D_MODEL))


def _w_in_specs(layer, idx):
    def spec(width, offset):
        return pl.BlockSpec((1, width, D_MODEL), lambda *g: (layer, offset // width + idx(*g), 0))
    return [spec(256, _OFF_XC), spec(256, _OFF_BG), spec(256, _OFF_CG), spec(128, _OFF_Q), spec(128, _OFF_K),
            spec(256, _OFF_V), spec(256, _OFF_O), spec(256, 0), spec(256, _OFF_GB - _OFF_GA)]


def _scan_lanes(x, combine, identity, seg_len):
    pos = lax.broadcasted_iota(jnp.int32, x.shape, 1) % seg_len
    shift = 1
    while shift < seg_len:
        x = combine(x, jnp.where(pos >= shift, pltpu.roll(x, shift, 1), identity))
        shift *= 2
    return x


def _softcap(a):
    return GATE_CAP * jnp.tanh(a / GATE_CAP)


def _log_sigmoid(x):
    return -(jnp.maximum(-x, 0.0) + jnp.log(1.0 + jnp.exp(-jnp.abs(x))))


def _gate_rows_prompt(gates, bias, gt_ref, first_chunk):
    n_chunks = gates.shape[0] // CHUNK
    gates_t = gates.T

    def stack(r0):
        return jnp.concatenate([gates_t[r0:r0 + 8, c * CHUNK:(c + 1) * CHUNK] for c in range(n_chunks)], axis=0)

    i_pre = _softcap(stack(0) + jnp.concatenate([bias[0:8]] * n_chunks, axis=0))
    f_pre = _softcap(stack(GATE_F_COL) + jnp.concatenate([bias[8:16]] * n_chunks, axis=0))
    b = _scan_lanes(_log_sigmoid(f_pre), jnp.add, 0.0, CHUNK)
    g = i_pre - b
    cm = _scan_lanes(g, jnp.maximum, -3e38, CHUNK)
    for c in range(n_chunks):
        gt_ref[first_chunk + c, 0:8] = b[c * 8:(c + 1) * 8]
        gt_ref[first_chunk + c, 8:16] = g[c * 8:(c + 1) * 8]
        gt_ref[first_chunk + c, 16:24] = cm[c * 8:(c + 1) * 8]


def _inproj_project(h, w_refs):
    wxc, wbg, wcg, wq, wk, wv, wo, wga, wgb = w_refs
    xc = _dot_nt(h, wxc[0])
    cg = _dot_nt(h, wcg[0])
    qk = _dot_nt(h, jnp.concatenate([wq[0], wk[0]], axis=0))
    return dict(u=cg * xc, bg=_dot_nt(h, wbg[0]), ga=_dot_nt(h, wga[0]), qk=qk,
                v=_dot_nt(h, wv[0]), o=_dot_nt(h, wo[0]), gb=_dot_nt(h, wgb[0]))


def _inproj_epilogue(p, conv, mid_ref, out_dtype, r0=0):
    rows = slice(r0, r0 + conv.shape[0])
    mid_ref[rows, _M_YA:_M_YA + 256] = (_sigmoid(p["ga"]) * (p["bg"] * conv)).astype(out_dtype)
    mid_ref[rows, _M_Q:_M_Q + DK] = p["qk"][:, :DK].astype(out_dtype)
    mid_ref[rows, _M_K:_M_K + DK] = (p["qk"][:, DK:] * (DK ** -0.5)).astype(out_dtype)
    mid_ref[rows, _M_V:_M_V + 256] = p["v"].astype(out_dtype)
    mid_ref[rows, _M_OG:_M_OG + 256] = (_sigmoid(p["o"]) * _sigmoid(p["gb"])).astype(out_dtype)


_RAW_COLS = 8 * 256


def _inproj_prompt_kernel(x_ref, sc_ref, sh_ref, g_ref, wxc, wbg, wcg, wq, wk, wv, wo, wga, wgb, wg_ref,
                          bg_ref, cw_ref, cb_ref, cprev_ref, mid_ref, gt_ref, nconv_ref,
                          h_scr, gate_scr, raw_even, raw_odd, ubuf, carry, *, tiles_per_seq, n_work):
    j = pl.program_id(0)
    tm = x_ref.shape[0]
    cur = jnp.minimum(j, n_work - 1)
    s = cur % N_HEADS
    prev = jnp.maximum(j - 1, 0)
    tile_p = prev // N_HEADS
    s_p = prev % N_HEADS
    chunks_per_step = (tm // CHUNK) // N_HEADS
    raw = (raw_even, raw_odd)

    @pl.when(j == 0)
    def _():
        raw_odd[...] = jnp.zeros(raw_odd.shape, _F32)
        carry[...] = jnp.zeros(carry.shape, _F32)

    @pl.when(jnp.logical_and(s == 0, j < n_work))
    def _():
        h = _rms_mod(x_ref[...], g_ref[0], sc_ref[0], sh_ref[0]).astype(_BF)
        h_scr[...] = h
        gate_scr[...] = _dot_nt(h, wg_ref[0])

    def step(slot):
        rows0 = pl.multiple_of(s * (chunks_per_step * CHUNK), CHUNK)
        _gate_rows_prompt(gate_scr[pl.ds(rows0, chunks_per_step * CHUNK), :], bg_ref[0], gt_ref,
                          s * chunks_per_step)

        h = h_scr[...]
        weights = (wxc[0], wcg[0], wbg[0], wga[0], jnp.concatenate([wq[0], wk[0]], axis=0),
                   wv[0], wo[0], wgb[0])
        seq_start = (tile_p % tiles_per_seq) == 0
        ubuf[6:8, :] = jnp.where(seq_start, cprev_ref[0], carry[s_p])
        cw = cw_ref[0]
        conv_bias = cb_ref[0]
        rows_per_part = tm // len(weights)
        for idx, w in enumerate(weights):
            raw[slot][:, idx * 256:(idx + 1) * 256] = _dot_nt(h, w)
            for r0 in range(idx * rows_per_part, (idx + 1) * rows_per_part, CHUNK):
                def piece(k):
                    return raw[1 - slot][r0:r0 + CHUNK, k * 256:(k + 1) * 256]

                u = piece(1) * piece(0)
                ubuf[8 + r0:8 + r0 + CHUNK, :] = u
                conv = (conv_bias + cw[0:1] * ubuf[6 + r0:6 + r0 + CHUNK, :]
                        + cw[1:2] * ubuf[7 + r0:7 + r0 + CHUNK, :] + cw[2:3] * u)
                _inproj_epilogue(dict(bg=piece(2), ga=piece(3), qk=piece(4), v=piece(5), o=piece(6),
                                      gb=piece(7)), conv, mid_ref, _BF, r0)
        last2 = ubuf[6 + tm:8 + tm, :]
        carry[s_p] = last2
        nconv_ref[0] = last2

    for parity in range(2):
        pl.when(j % 2 == parity)(functools.partial(step, parity))


def _inproj_prompt_call(layer, x, sc, sh, g, w_t, w_tail, w_gate, bgate, conv_w, conv_b, conv_prev,
                        *, seq_len, tm):
    m_rows = x.shape[0]
    n_tiles = m_rows // tm
    tiles_per_seq = seq_len // tm
    n_work = n_tiles * N_HEADS

    def cur(j):
        return jnp.minimum(j, n_work - 1)

    def prev(j):
        return jnp.maximum(j - 1, 0)

    kern = functools.partial(_inproj_prompt_kernel, tiles_per_seq=tiles_per_seq, n_work=n_work)
    w_specs = _w_in_specs(layer, lambda j: cur(j) % N_HEADS)
    seq_of_cur = lambda j: (cur(j) // N_HEADS // tiles_per_seq, 0, 0)
    return pl.pallas_call(
        kern,
        grid=(n_work + 1,),
        in_specs=[
            pl.BlockSpec((tm, D_MODEL), lambda j: (cur(j) // N_HEADS, 0)),
            pl.BlockSpec((1, 1, D_MODEL), seq_of_cur),
            pl.BlockSpec((1, 1, D_MODEL), seq_of_cur),
            pl.BlockSpec((1, 1, D_MODEL), lambda j: (layer, 0, 0)),
            *w_specs,
            pl.BlockSpec((1, GATE_COLS, D_MODEL), lambda j: (layer, 0, 0)),
            pl.BlockSpec((1, BIAS_ROWS, CHUNK), lambda j: (layer, 0, 0)),
            pl.BlockSpec((1, CONV_W, 256), lambda j: (layer, 0, prev(j) % N_HEADS)),
            pl.BlockSpec((1, 1, 256), lambda j: (layer, 0, prev(j) % N_HEADS)),
            pl.BlockSpec((1, CONV_W - 1, 256),
                         lambda j: (prev(j) // N_HEADS // tiles_per_seq, 0, prev(j) % N_HEADS)),
        ],
        out_specs=[
            pl.BlockSpec((tm, HEAD_COLS), lambda j: (prev(j) // N_HEADS, prev(j) % N_HEADS)),
            pl.BlockSpec((tm // CHUNK, GATE_ROWS, CHUNK), lambda j: (cur(j) // N_HEADS, 0, 0)),
            pl.BlockSpec((1, CONV_W - 1, 256), lambda j: (prev(j) // N_HEADS, 0, prev(j) % N_HEADS)),
        ],
        out_shape=[
            jax.ShapeDtypeStruct((m_rows, N_HEADS * HEAD_COLS), _BF),
            jax.ShapeDtypeStruct((m_rows // CHUNK, GATE_ROWS, CHUNK), _F32),
            jax.ShapeDtypeStruct((n_tiles, CONV_W - 1, D_MODEL), _F32),
        ],
        scratch_shapes=[
            pltpu.VMEM((tm, D_MODEL), _BF),
            pltpu.VMEM((tm, GATE_COLS), _F32),
            pltpu.VMEM((tm, _RAW_COLS), _F32),
            pltpu.VMEM((tm, _RAW_COLS), _F32),
            pltpu.VMEM((tm + 8, 256), _F32),
            pltpu.VMEM((N_HEADS, CONV_W - 1, 256), _F32),
        ],
        compiler_params=_params(1),
        name="inproj_prompt",
    )(x, sc, sh, g, *([w_t] * 7), *([w_tail] * 2), w_gate, bgate, conv_w, conv_b, conv_prev)


def _inproj_sample_kernel(x_ref, sc_ref, sh_ref, g_ref, wxc, wbg, wcg, wq, wk, wv, wo, wga, wgb, wg_ref,
                          cw_ref, cb_ref, cprev_ref, mid_ref, gt_ref, nconv_ref, h_scr, *, seq_len):
    s = pl.program_id(0)
    n_b = sc_ref.shape[0]

    @pl.when(s == 0)
    def _():
        h = _rms_mod(x_ref[...], g_ref[0], _rep_rows(sc_ref[...], seq_len), _rep_rows(sh_ref[...], seq_len))
        h = h.astype(_BF)
        h_scr[...] = h
        gt_ref[...] = _dot_nt(h, wg_ref[0])

    p = _inproj_project(h_scr[...], (wxc, wbg, wcg, wq, wk, wv, wo, wga, wgb))
    u = p["u"]
    prev0 = cprev_ref[0, 0]
    prev1 = cprev_ref[0, 1]
    p1 = jnp.concatenate([prev1, u[0:(seq_len - 1) * n_b]], axis=0)
    p2 = jnp.concatenate([prev0, prev1, u[0:(seq_len - 2) * n_b]], axis=0)
    cw = cw_ref[0]
    conv = cb_ref[0] + cw[0:1] * p2 + cw[1:2] * p1 + cw[2:3] * u
    nconv_ref[0] = u[(seq_len - 2) * n_b:(seq_len - 1) * n_b]
    nconv_ref[1] = u[(seq_len - 1) * n_b:seq_len * n_b]
    _inproj_epilogue(p, conv, mid_ref, _F32)


def _inproj_sample_call(layer, x, sc, sh, g, w_t, w_tail, w_gate, conv_w, conv_b, conv_prev, *, seq_len):
    tm = x.shape[0]
    n_b = tm // seq_len
    kern = functools.partial(_inproj_sample_kernel, seq_len=seq_len)
    w_specs = _w_in_specs(layer, lambda s: s)
    return pl.pallas_call(
        kern,
        grid=(N_HEADS,),
        in_specs=[
            pl.BlockSpec((tm, D_MODEL), lambda s: (0, 0)),
            pl.BlockSpec((n_b, D_MODEL), lambda s: (0, 0)),
            pl.BlockSpec((n_b, D_MODEL), lambda s: (0, 0)),
            pl.BlockSpec((1, 1, D_MODEL), lambda s: (layer, 0, 0)),
            *w_specs,
            pl.BlockSpec((1, GATE_COLS, D_MODEL), lambda s: (layer, 0, 0)),
            pl.BlockSpec((1, CONV_W, 256), lambda s: (layer, 0, s)),
            pl.BlockSpec((1, 1, 256), lambda s: (layer, 0, s)),
            pl.BlockSpec((1, CONV_W - 1, n_b, 256), lambda s: (layer, 0, 0, s)),
        ],
        out_specs=[
            pl.BlockSpec((tm, HEAD_COLS), lambda s: (0, s)),
            pl.BlockSpec((tm, GATE_COLS), lambda s: (0, 0)),
            pl.BlockSpec((CONV_W - 1, n_b, 256), lambda s: (0, 0, s)),
        ],
        out_shape=[
            jax.ShapeDtypeStruct((tm, N_HEADS * HEAD_COLS), _F32),
            jax.ShapeDtypeStruct((tm, GATE_COLS), _F32),
            jax.ShapeDtypeStruct((CONV_W - 1, n_b, D_MODEL), _F32),
        ],
        scratch_shapes=[pltpu.VMEM((tm, D_MODEL), _BF)],
        compiler_params=_params(1),
        name="inproj_sample",
    )(x, sc, sh, g, *([w_t] * 7), *([w_tail] * 2), w_gate, conv_w, conv_b, conv_prev)


def _causal_mask(lq):
    row_id = lax.broadcasted_iota(jnp.int32, (lq, CHUNK), 0)
    col_id = lax.broadcasted_iota(jnp.int32, (lq, CHUNK), 1)
    return col_id <= row_id


def _chunk_local(a, causal):
    dmat = jnp.where(causal, jnp.exp(a["g_row"] - a["m_col"]), 0.0)
    s_bf = (_dot_nt(a["q"], a["k"]) * dmat).astype(_BF)
    kw = a["k"].astype(_F32) * a["ws_col"]
    return s_bf, _dot(kw.T.astype(_BF), a["v_ext"])


def _chunk_output(a, s_bf, d_state, cx_ref, h, hng_h):
    q_inter = (a["q"].astype(_F32) * a["inter_col"]).astype(_BF)
    cx = cx_ref[h]
    rhs = jnp.concatenate([a["v_ext"], cx.astype(_BF)], axis=0)
    res = _dot(jnp.concatenate([s_bf, q_inter], axis=1), rhs)
    cx_ref[h] = a["dec"] * cx + d_state
    num = res[:, :DV]
    rden = 1.0 / jnp.maximum(jnp.abs(res[:, DV:]), a["em_col"])
    sq = jnp.sum(num * num, axis=-1, keepdims=True) * (1.0 / DV)
    scale = rden * lax.rsqrt(rden * rden * sq + EPS)
    hn = num * jnp.concatenate([scale, scale], axis=1) * hng_h
    return a["ya"] + a["og"] * hn


def _mlstm_prompt_kernel(mid_ref, gt_ref, hng_ref, c0_ref, n0_ref, m0_ref, cbuf_ref,
                         mg_ref, c_ref, n_ref, m_ref, cx_ref):
    del cbuf_ref
    t = pl.program_id(1)

    @pl.when(t == 0)
    def _():
        m_ref[...] = m0_ref[...]
        for h in range(N_HEADS):
            cx_ref[h, :, 0:DV] = c0_ref[0, h]
            cx_ref[h, :, DV:EXT_COLS] = jnp.broadcast_to(n0_ref[0, h:h + 1, :], (CHUNK, DK)).T

    n_chunks = mid_ref.shape[0] // CHUNK
    rows = n_chunks * 8
    hng = hng_ref[0]
    gates = gt_ref[...]
    b = gates[:, 0:8, :].reshape(rows, CHUNK)
    g = gates[:, 8:16, :].reshape(rows, CHUNK)
    cm = gates[:, 16:24, :].reshape(rows, CHUNK)
    b_last = jnp.broadcast_to(b[:, CHUNK - 1:CHUNK], (rows, CHUNK))
    cm_last = jnp.broadcast_to(cm[:, CHUNK - 1:CHUNK], (rows, CHUNK))

    m_prev = m_ref[0]
    m_prevs = []
    for c in range(n_chunks):
        m_prevs.append(m_prev)
        m_prev = b_last[c * 8:(c + 1) * 8] + jnp.maximum(cm_last[c * 8:(c + 1) * 8], m_prev)
    m_ref[0] = m_prev
    m_prev_all = jnp.concatenate(m_prevs, axis=0)

    m_run = jnp.maximum(cm, m_prev_all)
    inter = jnp.exp(m_prev_all - m_run)
    em = jnp.exp(-(b + m_run))
    m_last = jnp.maximum(cm_last, m_prev_all)
    ws = jnp.exp(g - m_last)
    decay = jnp.exp(m_prev_all - m_last)

    causal = _causal_mask(CHUNK)
    ones = jnp.ones((CHUNK, CHUNK), _BF)
    group = CHUNK // 32
    cols_of = []
    for c0 in range(0, n_chunks, group):
        pieces = []
        for c in range(c0, min(c0 + group, n_chunks)):
            sl = slice(c * 8, (c + 1) * 8)
            pieces += [m_run[sl], inter[sl], em[sl], ws[sl]]
        if len(pieces) * 8 < CHUNK:
            pieces.append(jnp.zeros((CHUNK - len(pieces) * 8, CHUNK), _F32))
        cols_of.append(jnp.concatenate(pieces, axis=0).T)

    def chunk_inputs(c):
        cols = cols_of[c // group]
        base = (c % group) * 32
        r0 = c * CHUNK
        heads = []
        for h in range(N_HEADS):
            mb = h * HEAD_COLS
            row = c * 8 + h
            v = mid_ref[r0:r0 + CHUNK, mb + _M_V:mb + _M_V + 256]
            heads.append(dict(
                q=mid_ref[r0:r0 + CHUNK, mb + _M_Q:mb + _M_Q + 128],
                k=mid_ref[r0:r0 + CHUNK, mb + _M_K:mb + _M_K + 128],
                v_ext=jnp.concatenate([v, ones], axis=1),
                ya=mid_ref[r0:r0 + CHUNK, mb + _M_YA:mb + _M_YA + 256].astype(_F32),
                og=mid_ref[r0:r0 + CHUNK, mb + _M_OG:mb + _M_OG + 256].astype(_F32),
                g_row=g[row:row + 1, :],
                m_col=cols[:, base + h:base + h + 1],
                inter_col=cols[:, base + 8 + h:base + 9 + h],
                em_col=cols[:, base + 16 + h:base + 17 + h],
                ws_col=cols[:, base + 24 + h:base + 25 + h],
                dec=decay[row:row + 1, 0:1]))
        return heads

    for c in range(n_chunks):
        outs = []
        for h, a in enumerate(chunk_inputs(c)):
            s_bf, d_state = _chunk_local(a, causal)
            outs.append(_chunk_output(a, s_bf, d_state, cx_ref, h, hng[:, h * DV:(h + 1) * DV]))
        mg_ref[c * CHUNK:(c + 1) * CHUNK, :] = jnp.concatenate(outs, axis=1).astype(_BF)

    @pl.when(t == pl.num_programs(1) - 1)
    def _():
        for h in range(N_HEADS):
            c_ref[0, 0, h] = cx_ref[h, :, 0:DV]
            n_ref[0, h:h + 1, :] = cx_ref[h, :, DV:EXT_COLS].T[0:1, :]


def _mlstm_prompt_call(layer, mid, gates_t, hng, c0, n0, m0, c_buf, *, seq_len, tb):
    m_rows = mid.shape[0]
    n_seq = m_rows // seq_len
    steps = seq_len // tb
    aliases = {} if c_buf is None else {6: 1}
    if c_buf is None:
        c_buf = jnp.zeros((1,), _F32)
    return pl.pallas_call(
        _mlstm_prompt_kernel,
        grid=(n_seq, steps),
        in_specs=[
            pl.BlockSpec((tb, N_HEADS * HEAD_COLS), lambda b, t: (b * steps + t, 0)),
            pl.BlockSpec((tb // CHUNK, GATE_ROWS, CHUNK), lambda b, t: (b * steps + t, 0, 0)),
            pl.BlockSpec((1, 1, D_MODEL), lambda b, t: (layer, 0, 0)),
            pl.BlockSpec((1, N_HEADS, DK, DV), lambda b, t: (b, 0, 0, 0)),
            pl.BlockSpec((1, N_HEADS, DK), lambda b, t: (b, 0, 0)),
            pl.BlockSpec((1, 8, CHUNK), lambda b, t: (b, 0, 0)),
            pl.BlockSpec(memory_space=pl.ANY),
        ],
        out_specs=[
            pl.BlockSpec((tb, D_MODEL), lambda b, t: (b * steps + t, 0)),
            pl.BlockSpec((1, 1, N_HEADS, DK, DV), lambda b, t: (layer, b, 0, 0, 0)),
            pl.BlockSpec((1, N_HEADS, DK), lambda b, t: (b, 0, 0)),
            pl.BlockSpec((1, 8, CHUNK), lambda b, t: (b, 0, 0)),
        ],
        out_shape=[
            jax.ShapeDtypeStruct((m_rows, D_MODEL), _BF),
            jax.ShapeDtypeStruct((DEPTH, n_seq, N_HEADS, DK, DV), _F32),
            jax.ShapeDtypeStruct((n_seq, N_HEADS, DK), _F32),
            jax.ShapeDtypeStruct((n_seq, 8, CHUNK), _F32),
        ],
        scratch_shapes=[pltpu.VMEM((N_HEADS, DK, EXT_COLS), _F32)],
        input_output_aliases=aliases,
        compiler_params=_params(2),
        name="mlstm_prompt",
    )(mid, gates_t, hng, c0, n0, m0, c_buf)


def _mlstm_sample_kernel(mid_ref, gt_ref, bias_ref, hng_ref, c0_ref, n0_ref, m0_ref, cbuf_ref,
                         mg_ref, c_ref, n_ref, m_ref,
                         qs_scr, kk_scr, wv_scr, rs_scr, dec_scr, lhs_q, lhs_k, lhs_wv):
    del cbuf_ref
    seq_len, bb, _ = mid_ref.shape
    hng = hng_ref[0]
    bias = bias_ref[0]
    m_prev = m0_ref[0]

    b_t, g_t, cm_t = [], [], []
    for t in range(seq_len):
        pre = _softcap(gt_ref[t] + bias)
        logf = _log_sigmoid(pre[:, GATE_F_COL:GATE_F_COL + CHUNK])
        b_t.append(logf if t == 0 else b_t[-1] + logf)
        g_t.append(pre[:, 0:CHUNK] - b_t[-1])
        cm_t.append(g_t[-1] if t == 0 else jnp.maximum(cm_t[-1], g_t[-1]))
    m_run = [jnp.maximum(cm, m_prev) for cm in cm_t]
    inter = [jnp.exp(m_prev - mr) for mr in m_run]
    em = [jnp.exp(-(b + mr)) for b, mr in zip(b_t, m_run)]
    m_last = m_run[-1]
    ws = [jnp.exp(g - m_last) for g in g_t]
    decay = jnp.exp(m_prev - m_last)
    m_ref[...] = b_t[-1] + m_last
    dec_scr[...] = decay

    lhs_q[...] = jnp.zeros(lhs_q.shape, _F32)
    lhs_k[...] = jnp.zeros(lhs_k.shape, _F32)
    lhs_wv[...] = jnp.zeros(lhs_wv.shape, _F32)
    n_all = n0_ref[0]
    den_part = {}
    for h in range(N_HEADS):
        def col(x):
            return x[:, h:h + 1]
        mb = h * HEAD_COLS
        q = [mid_ref[t, :, mb + _M_Q:mb + _M_Q + DK] for t in range(seq_len)]
        k = [mid_ref[t, :, mb + _M_K:mb + _M_K + DK] for t in range(seq_len)]
        v = [mid_ref[t, :, mb + _M_V:mb + _M_V + DV] for t in range(seq_len)]
        n_h = n_all[:, h * DK:(h + 1) * DK]
        n_new = col(decay) * n_h
        for t in range(seq_len):
            qs = q[t] * col(inter[t])
            qs_scr[t, :, h * DK:(h + 1) * DK] = qs
            kk_scr[t, :, h * DK:(h + 1) * DK] = k[t]
            wv_scr[t, :, h * DV:(h + 1) * DV] = col(ws[t]) * v[t]
            n_new = n_new + col(ws[t]) * k[t]
            den = jnp.sum(qs * n_h, axis=-1, keepdims=True)
            num = None
            for s in range(t + 1):
                w = jnp.sum(q[t] * k[s], axis=-1, keepdims=True) * jnp.exp(col(g_t[s]) - col(m_run[t]))
                num = w * v[s] if num is None else num + w * v[s]
                den = den + w
            mg_ref[t, :, h * DV:(h + 1) * DV] = num
            den_part[h, t] = den
        n_ref[:, h * DK:(h + 1) * DK] = n_new

    def body(i, carry):
        for t in range(seq_len):
            lhs_q[t:t + 1, :] = qs_scr[t, pl.ds(i, 1), :]
            lhs_k[t:t + 1, :] = kk_scr[t, pl.ds(i, 1), :]
            lhs_wv[t:t + 1, :] = wv_scr[t, pl.ds(i, 1), :]
        for h in range(N_HEADS):
            c_old = c0_ref[0, i, h]
            r = _dot(lhs_q[:, h * DK:(h + 1) * DK].astype(_BF), c_old.astype(_BF))
            for t in range(seq_len):
                rs_scr[t, pl.ds(i, 1), h * DV:(h + 1) * DV] = r[t:t + 1, :]
            d_c = lax.dot_general(lhs_k[:, h * DK:(h + 1) * DK].astype(_BF),
                                  lhs_wv[:, h * DV:(h + 1) * DV].astype(_BF),
                                  (((0,), (0,)), ((), ())), preferred_element_type=_F32)
            c_ref[0, i, h] = dec_scr[pl.ds(i, 1), h:h + 1] * c_old + d_c
        return carry

    lax.fori_loop(0, bb, body, 0)

    for h in range(N_HEADS):
        mb = h * HEAD_COLS
        for t in range(seq_len):
            num = mg_ref[t, :, h * DV:(h + 1) * DV] + rs_scr[t, :, h * DV:(h + 1) * DV]
            hraw = num * (1.0 / jnp.maximum(jnp.abs(den_part[h, t]), em[t][:, h:h + 1]))
            ms = jnp.mean(hraw * hraw, axis=-1, keepdims=True)
            hn = hraw * lax.rsqrt(ms + EPS) * hng[:, h * DV:(h + 1) * DV]
            ya = mid_ref[t, :, mb + _M_YA:mb + _M_YA + DV]
            og = mid_ref[t, :, mb + _M_OG:mb + _M_OG + DV]
            mg_ref[t, :, h * DV:(h + 1) * DV] = ya + og * hn


def _mlstm_sample_call(layer, mid3, gates3, bias, hng, c0, n0, m0, c_buf, *, bb):
    seq_len, n_seq, _ = mid3.shape
    aliases = {} if c_buf is None else {7: 1}
    if c_buf is None:
        c_buf = jnp.zeros((1,), _F32)
    return pl.pallas_call(
        _mlstm_sample_kernel,
        grid=(n_seq // bb,),
        in_specs=[
            pl.BlockSpec((seq_len, bb, N_HEADS * HEAD_COLS), lambda i: (0, i, 0)),
            pl.BlockSpec((seq_len, bb, GATE_COLS), lambda i: (0, i, 0)),
            pl.BlockSpec((1, 1, GATE_COLS), lambda i: (layer, 0, 0)),
            pl.BlockSpec((1, 1, D_MODEL), lambda i: (layer, 0, 0)),
            pl.BlockSpec((1, bb, N_HEADS, DK, DV), lambda i: (layer, i, 0, 0, 0)),
            pl.BlockSpec((1, bb, N_HEADS * DK), lambda i: (layer, i, 0)),
            pl.BlockSpec((1, bb, CHUNK), lambda i: (layer, i, 0)),
            pl.BlockSpec(memory_space=pl.ANY),
        ],
        out_specs=[
            pl.BlockSpec((seq_len, bb, D_MODEL), lambda i: (0, i, 0)),
            pl.BlockSpec((1, bb, N_HEADS, DK, DV), lambda i: (layer, i, 0, 0, 0)),
            pl.BlockSpec((bb, N_HEADS * DK), lambda i: (i, 0)),
            pl.BlockSpec((bb, CHUNK), lambda i: (i, 0)),
        ],
        out_shape=[
            jax.ShapeDtypeStruct((seq_len, n_seq, D_MODEL), _F32),
            jax.ShapeDtypeStruct((DEPTH, n_seq, N_HEADS, DK, DV), _F32),
            jax.ShapeDtypeStruct((n_seq, N_HEADS * DK), _F32),
            jax.ShapeDtypeStruct((n_seq, CHUNK), _F32),
        ],
        scratch_shapes=[
            pltpu.VMEM((seq_len, bb, N_HEADS * DK), _F32),
            pltpu.VMEM((seq_len, bb, N_HEADS * DK), _F32),
            pltpu.VMEM((seq_len, bb, N_HEADS * DV), _F32),
            pltpu.VMEM((seq_len, bb, N_HEADS * DV), _F32),
            pltpu.VMEM((bb, CHUNK), _F32),
            pltpu.VMEM((8, N_HEADS * DK), _F32),
            pltpu.VMEM((8, N_HEADS * DK), _F32),
            pltpu.VMEM((8, N_HEADS * DV), _F32),
        ],
        input_output_aliases=aliases,
        compiler_params=_params(1),
        name="mlstm_sample",
    )(mid3, gates3, bias, hng, c0, n0, m0, c_buf)


def _mlp_kernel(x_ref, mg_ref, gt1_ref, sc2_ref, sh2_ref, gt2_ref, g2_ref, wo_ref, wu_ref, wd_ref, gf_ref,
                o_ref, xmid, h2, acc, *, mod_reps, final_norm):
    f = pl.program_id(1)

    def mod(ref):
        return ref[0] if mod_reps is None else _rep_rows(ref[...], mod_reps)

    @pl.when(f == 0)
    def _():
        mix = _dot(mg_ref[...].astype(_BF), wo_ref[0])
        xm = x_ref[...] + mod(gt1_ref) * mix
        xmid[...] = xm
        h2[...] = _rms_mod(xm, g2_ref[0], mod(sc2_ref), mod(sh2_ref)).astype(_BF)
        acc[...] = jnp.zeros(acc.shape, _F32)

    a = jnp.maximum(_dot(h2[...], wu_ref[0]), 0.0)
    acc[...] += _dot((a * a).astype(_BF), wd_ref[0])

    @pl.when(f == pl.num_programs(1) - 1)
    def _():
        y = xmid[...] + mod(gt2_ref) * acc[...]
        if final_norm:
            ms = jnp.mean(y * y, axis=-1, keepdims=True)
            y = y * lax.rsqrt(ms + EPS) * gf_ref[...]
        o_ref[...] = y


def _mlp_call(layer, x, merged, gt1, sc2, sh2, gt2, g2, w_out, w_up, w_down, g_final, *,
              tm, tf, seq_len, final_norm):
    m_rows = x.shape[0]
    if seq_len >= tm:
        tiles_per_seq = seq_len // tm
        mod_reps = None
        mod_spec = pl.BlockSpec((1, 1, D_MODEL), lambda i, f: (i // tiles_per_seq, 0, 0))
    else:
        mod_reps = seq_len
        mod_spec = pl.BlockSpec((tm // seq_len, D_MODEL), lambda i, f: (0, 0))
    kern = functools.partial(_mlp_kernel, mod_reps=mod_reps, final_norm=final_norm)
    return pl.pallas_call(
        kern,
        grid=(m_rows // tm, D_FF // tf),
        in_specs=[
            pl.BlockSpec((tm, D_MODEL), lambda i, f: (i, 0)),
            pl.BlockSpec((tm, D_MODEL), lambda i, f: (i, 0)),
            mod_spec, mod_spec, mod_spec, mod_spec,
            pl.BlockSpec((1, 1, D_MODEL), lambda i, f: (layer, 0, 0)),
            pl.BlockSpec((1, D_MODEL, D_MODEL), lambda i, f: (layer, 0, 0)),
            pl.BlockSpec((1, D_MODEL, tf), lambda i, f: (layer, 0, f)),
            pl.BlockSpec((1, tf, D_MODEL), lambda i, f: (layer, f, 0)),
            pl.BlockSpec((1, D_MODEL), lambda i, f: (0, 0)),
        ],
        out_specs=pl.BlockSpec((tm, D_MODEL), lambda i, f: (i, 0)),
        out_shape=jax.ShapeDtypeStruct((m_rows, D_MODEL), _F32),
        scratch_shapes=[
            pltpu.VMEM((tm, D_MODEL), _F32),
            pltpu.VMEM((tm, D_MODEL), _BF),
            pltpu.VMEM((tm, D_MODEL), _F32),
        ],
        compiler_params=_params(2),
        name="outproj_mlp",
    )(x, merged, gt1, sc2, sh2, gt2, g2, w_out, w_up, w_down, g_final)


def kernel(x_prompt, x_sample, state_conv, state_C, state_n, state_m, c_prompt, c_sample,
           w_ada, b_ada, g_norm1, g_norm2, w_in, b_gate, conv_w, conv_b, hn_g, w_out, w_up, w_down, g_final):
    n_p, seq_p, _ = x_prompt.shape
    n_s, seq_s, _ = x_sample.shape
    rows_s = n_s * seq_s
    tm_p = 1024

    w_t = jnp.swapaxes(w_in, 1, 2).astype(_BF)
    w_tail = w_t[:, _OFF_GA:]
    zrows = jnp.zeros((DEPTH, GATE_F_COL - N_HEADS, D_MODEL), _BF)
    w_gate = jnp.concatenate([w_t[:, _OFF_IG:_OFF_IG + N_HEADS], zrows,
                              w_t[:, _OFF_IG + N_HEADS:_OFF_GA], zrows], axis=1)
    w_out_b = w_out.astype(_BF)
    w_up_b = w_up.astype(_BF)
    w_down_b = w_down.astype(_BF)
    zb = jnp.zeros((DEPTH, 4), _F32)
    bgate = jnp.broadcast_to(
        jnp.concatenate([b_gate[:, :N_HEADS], zb, b_gate[:, N_HEADS:], zb], axis=-1)[:, :, None],
        (DEPTH, BIAS_ROWS, CHUNK))
    zb = jnp.zeros((DEPTH, GATE_F_COL - N_HEADS), _F32)
    bias_row = jnp.concatenate([b_gate[:, :N_HEADS], zb, b_gate[:, N_HEADS:], zb], axis=-1)[:, None, :]
    g1 = g_norm1.reshape(DEPTH, 1, D_MODEL)
    g2 = g_norm2.reshape(DEPTH, 1, D_MODEL)
    hng = hn_g.reshape(DEPTH, 1, D_MODEL)
    cb = conv_b.reshape(DEPTH, 1, D_MODEL)
    gfin = g_final.reshape(1, D_MODEL)

    mod = _ada_call(jnp.concatenate([c_prompt, c_sample], axis=0), w_ada, b_ada)
    mod_p = mod[:, :n_p].reshape(DEPTH, n_p, 6, 1, D_MODEL)
    mod_s = mod[:, n_p:].reshape(DEPTH, n_s, 6, D_MODEL)

    xp = x_prompt.reshape(n_p * seq_p, D_MODEL)
    xs = x_sample.transpose(1, 0, 2).reshape(rows_s, D_MODEL)
    conv_s_in = state_conv.transpose(0, 2, 1, 3)
    n_s_in = state_n.reshape(DEPTH, n_s, N_HEADS * DK)
    m_s_in = jnp.pad(state_m, ((0, 0), (0, 0), (0, CHUNK - N_HEADS)))

    zeros_conv = jnp.zeros((n_p, CONV_W - 1, D_MODEL), _F32)
    zeros_c = jnp.zeros((n_p, N_HEADS, DK, DV), _F32)
    zeros_n = jnp.zeros((n_p, N_HEADS, DK), _F32)
    zeros_m = jnp.zeros((n_p, 8, CHUNK), _F32)

    p_conv, p_n, p_m, s_conv, s_n, s_m = [], [], [], [], [], []
    p_c = s_c = None
    for l in range(DEPTH):
        final = l == DEPTH - 1

        sh1, sc1, gt1, sh2, sc2, gt2 = (mod_p[l, :, j] for j in range(6))
        mid, gates_t, nconv = _inproj_prompt_call(
            l, xp, sc1, sh1, g1, w_t, w_tail, w_gate, bgate, conv_w, cb, zeros_conv, seq_len=seq_p, tm=tm_p)
        merged, p_c, n1, m1 = _mlstm_prompt_call(
            l, mid, gates_t, hng, zeros_c, zeros_n, zeros_m, p_c, seq_len=seq_p, tb=1024)
        xp = _mlp_call(l, xp, merged, gt1, sc2, sh2, gt2, g2, w_out_b, w_up_b, w_down_b, gfin,
                       tm=512, tf=2048, seq_len=seq_p, final_norm=final)
        tiles_per_seq = seq_p // tm_p
        p_conv.append(nconv[tiles_per_seq - 1::tiles_per_seq]); p_n.append(n1); p_m.append(m1[:, :N_HEADS, 0])

        sh1, sc1, gt1, sh2, sc2, gt2 = (mod_s[l, :, j] for j in range(6))
        mid, gates, nconv = _inproj_sample_call(
            l, xs, sc1, sh1, g1, w_t, w_tail, w_gate, conv_w, cb, conv_s_in, seq_len=seq_s)
        merged3, s_c, n1, m1 = _mlstm_sample_call(
            l, mid.reshape(seq_s, n_s, N_HEADS * HEAD_COLS), gates.reshape(seq_s, n_s, GATE_COLS), bias_row, hng,
            state_C, n_s_in, m_s_in, s_c, bb=16)
        xs = _mlp_call(l, xs, merged3.reshape(rows_s, D_MODEL), gt1, sc2, sh2, gt2, g2,
                       w_out_b, w_up_b, w_down_b, gfin,
                       tm=rows_s, tf=1024, seq_len=seq_s, final_norm=final)
        s_conv.append(nconv.transpose(1, 0, 2))
        s_n.append(n1.reshape(n_s, N_HEADS, DK)); s_m.append(m1[:, :N_HEADS])

    return (xp.reshape(n_p, seq_p, D_MODEL), xs.reshape(seq_s, n_s, D_MODEL).transpose(1, 0, 2),
            jnp.stack(p_conv), p_c, jnp.stack(p_n), jnp.stack(p_m),
            jnp.stack(s_conv), s_c, jnp.stack(s_n), jnp.stack(s_m))
```

```python
import functools

import jax
import jax.numpy as jnp
from jax import lax
from jax.experimental import pallas as pl
from jax.experimental.pallas import tpu as pltpu

D_MODEL = 1024
N_HEADS = 4
DK = 128
DV = 256
D_FF = 4096
DEPTH = 4
CONV_W = 3
GATE_CAP = 15.0
EPS = 1e-6
CHUNK = 128
HEAD_COLS = 1024
GATE_COLS = 256
GATE_F_COL = 128
BIAS_ROWS = 16
GATE_ROWS = 24
EXT_COLS = DV + CHUNK
VMEM_LIMIT = 56 * 1024 * 1024

_OFF_XC, _OFF_BG, _OFF_CG = 0, 1024, 2048
_OFF_Q, _OFF_K, _OFF_V, _OFF_O = 3072, 3584, 4096, 5120
_OFF_IG, _OFF_GA, _OFF_GB = 6144, 6152, 7176

_M_YA, _M_Q, _M_K, _M_V, _M_OG = 0, 256, 384, 512, 768

_BF = jnp.bfloat16
_F32 = jnp.float32


def _dot(a, b):
    return jnp.dot(a, b, preferred_element_type=_F32)


def _dot_nt(a, b):
    return lax.dot_general(a, b, (((1,), (1,)), ((), ())), preferred_element_type=_F32)


def _sigmoid(x):
    return 0.5 * jnp.tanh(0.5 * x) + 0.5


def _rms_mod(x, g, sc, sh):
    ms = jnp.mean(x * x, axis=-1, keepdims=True)
    return (x * lax.rsqrt(ms + EPS)) * (g * (1.0 + sc)) + sh


def _rep_rows(v, reps):
    return v if reps == 1 else jnp.concatenate([v] * reps, axis=0)


def _params(n_axes):
    return pltpu.CompilerParams(dimension_semantics=("arbitrary",) * n_axes, vmem_limit_bytes=VMEM_LIMIT)


N_MOD = 6
_SH1, _SC1, _GT1, _SH2, _SC2, _GT2 = range(N_MOD)


def _ada_kernel(c_ref, w_ref, b_ref, o_ref):
    c = c_ref[...]
    a = (c * _sigmoid(c)).astype(_BF)
    o_ref[0, 0] = _dot(a, w_ref[0].astype(_BF)) + b_ref[0]


def _ada_call(c_all, w_ada, b_ada):
    n_rows = c_all.shape[0]
    return pl.pallas_call(
        _ada_kernel,
        grid=(DEPTH, N_MOD),
        in_specs=[
            pl.BlockSpec((n_rows, D_MODEL), lambda l, j: (0, 0)),
            pl.BlockSpec((1, D_MODEL, D_MODEL), lambda l, j: (l, 0, j)),
            pl.BlockSpec((1, 1, D_MODEL), lambda l, j: (l, 0, j)),
        ],
        out_specs=pl.BlockSpec((1, 1, n_rows, D_MODEL), lambda l, j: (l, j, 0, 0)),
        out_shape=jax.ShapeDtypeStruct((DEPTH, N_MOD, n_rows, D_MODEL), _F32),
        compiler_params=_params(2),
        name="ada_mod",
    )(c_all, w_ada, b_ada.reshape(DEPTH, 1, N_MOD * D_MODEL))


def _w_in_specs(layer, idx):
    def spec(width, offset):
        return pl.BlockSpec((1, width, D_MODEL), lambda *g: (layer, offset // width + idx(*g), 0))
    return [spec(256, _OFF_XC), spec(256, _OFF_BG), spec(256, _OFF_CG), spec(128, _OFF_Q), spec(128, _OFF_K),
            spec(256, _OFF_V), spec(256, _OFF_O), spec(256, 0), spec(256, _OFF_GB - _OFF_GA)]


def _scan_lanes(x, combine, identity, seg_len):
    pos = lax.broadcasted_iota(jnp.int32, x.shape, 1) % seg_len
    shift = 1
    while shift < seg_len:
        x = combine(x, jnp.where(pos >= shift, pltpu.roll(x, shift, 1), identity))
        shift *= 2
    return x


def _softcap(a):
    return GATE_CAP * jnp.tanh(a / GATE_CAP)


def _log_sigmoid(x):
    return -(jnp.maximum(-x, 0.0) + jnp.log(1.0 + jnp.exp(-jnp.abs(x))))


def _gate_rows_prompt(gates, bias, gt_ref, first_chunk):
    n_chunks = gates.shape[0] // CHUNK
    gates_t = gates.T

    def stack(r0):
        return jnp.concatenate([gates_t[r0:r0 + 8, c * CHUNK:(c + 1) * CHUNK] for c in range(n_chunks)], axis=0)

    i_pre = _softcap(stack(0) + jnp.concatenate([bias[0:8]] * n_chunks, axis=0))
    f_pre = _softcap(stack(GATE_F_COL) + jnp.concatenate([bias[8:16]] * n_chunks, axis=0))
    b = _scan_lanes(_log_sigmoid(f_pre), jnp.add, 0.0, CHUNK)
    g = i_pre - b
    cm = _scan_lanes(g, jnp.maximum, -3e38, CHUNK)
    for c in range(n_chunks):
        gt_ref[first_chunk + c, 0:8] = b[c * 8:(c + 1) * 8]
        gt_ref[first_chunk + c, 8:16] = g[c * 8:(c + 1) * 8]
        gt_ref[first_chunk + c, 16:24] = cm[c * 8:(c + 1) * 8]


def _inproj_project(h, w_refs):
    wxc, wbg, wcg, wq, wk, wv, wo, wga, wgb = w_refs
    xc = _dot_nt(h, wxc[0])
    cg = _dot_nt(h, wcg[0])
    qk = _dot_nt(h, jnp.concatenate([wq[0], wk[0]], axis=0))
    return dict(u=cg * xc, bg=_dot_nt(h, wbg[0]), ga=_dot_nt(h, wga[0]), qk=qk,
                v=_dot_nt(h, wv[0]), o=_dot_nt(h, wo[0]), gb=_dot_nt(h, wgb[0]))


def _inproj_epilogue(p, conv, mid_ref, out_dtype, r0=0):
    rows = slice(r0, r0 + conv.shape[0])
    mid_ref[rows, _M_YA:_M_YA + 256] = (_sigmoid(p["ga"]) * (p["bg"] * conv)).astype(out_dtype)
    mid_ref[rows, _M_Q:_M_Q + DK] = p["qk"][:, :DK].astype(out_dtype)
    mid_ref[rows, _M_K:_M_K + DK] = (p["qk"][:, DK:] * (DK ** -0.5)).astype(out_dtype)
    mid_ref[rows, _M_V:_M_V + 256] = p["v"].astype(out_dtype)
    mid_ref[rows, _M_OG:_M_OG + 256] = (_sigmoid(p["o"]) * _sigmoid(p["gb"])).astype(out_dtype)


_RAW_COLS = 8 * 256


def _inproj_prompt_kernel(x_ref, sc_ref, sh_ref, g_ref, wxc, wbg, wcg, wq, wk, wv, wo, wga, wgb, wg_ref,
                          bg_ref, cw_ref, cb_ref, cprev_ref, mid_ref, gt_ref, nconv_ref,
                          h_scr, gate_scr, raw_even, raw_odd, ubuf, carry, *, tiles_per_seq, n_work):
    j = pl.program_id(0)
    tm = x_ref.shape[0]
    cur = jnp.minimum(j, n_work - 1)
    s = cur % N_HEADS
    prev = jnp.maximum(j - 1, 0)
    tile_p = prev // N_HEADS
    s_p = prev % N_HEADS
    chunks_per_step = (tm // CHUNK) // N_HEADS
    raw = (raw_even, raw_odd)

    @pl.when(j == 0)
    def _():
        raw_odd[...] = jnp.zeros(raw_odd.shape, _F32)
        carry[...] = jnp.zeros(carry.shape, _F32)

    @pl.when(jnp.logical_and(s == 0, j < n_work))
    def _():
        h = _rms_mod(x_ref[...], g_ref[0], sc_ref[0, 0, 0], sh_ref[0, 0, 0]).astype(_BF)
        h_scr[...] = h
        gate_scr[...] = _dot_nt(h, wg_ref[0])

    def step(slot):
        rows0 = pl.multiple_of(s * (chunks_per_step * CHUNK), CHUNK)
        _gate_rows_prompt(gate_scr[pl.ds(rows0, chunks_per_step * CHUNK), :], bg_ref[0], gt_ref,
                          s * chunks_per_step)

        h = h_scr[...]
        weights = (wxc[0], wcg[0], wbg[0], wga[0], jnp.concatenate([wq[0], wk[0]], axis=0),
                   wv[0], wo[0], wgb[0])
        seq_start = (tile_p % tiles_per_seq) == 0
        ubuf[6:8, :] = jnp.where(seq_start, cprev_ref[0], carry[s_p])
        cw = cw_ref[0]
        conv_bias = cb_ref[0]
        rows_per_part = tm // len(weights)
        for idx, w in enumerate(weights):
            raw[slot][:, idx * 256:(idx + 1) * 256] = _dot_nt(h, w)
            for r0 in range(idx * rows_per_part, (idx + 1) * rows_per_part, CHUNK):
                def piece(k):
                    return raw[1 - slot][r0:r0 + CHUNK, k * 256:(k + 1) * 256]

                u = piece(1) * piece(0)
                ubuf[8 + r0:8 + r0 + CHUNK, :] = u
                conv = (conv_bias + cw[0:1] * ubuf[6 + r0:6 + r0 + CHUNK, :]
                        + cw[1:2] * ubuf[7 + r0:7 + r0 + CHUNK, :] + cw[2:3] * u)
                _inproj_epilogue(dict(bg=piece(2), ga=piece(3), qk=piece(4), v=piece(5), o=piece(6),
                                      gb=piece(7)), conv, mid_ref, _BF, r0)
        last2 = ubuf[6 + tm:8 + tm, :]
        carry[s_p] = last2
        nconv_ref[0] = last2

    for parity in range(2):
        pl.when(j % 2 == parity)(functools.partial(step, parity))


def _mod_row_spec(layer, piece, row0, seq_of):
    return pl.BlockSpec((1, 1, 1, 1, D_MODEL), lambda *g: (layer, piece, row0 + seq_of(*g), 0, 0))


def _inproj_prompt_call(layer, x, mod_rows, row0, g, w_t, w_tail, w_gate, bgate, conv_w, conv_b, conv_prev,
                        *, seq_len, tm):
    m_rows = x.shape[0]
    n_tiles = m_rows // tm
    tiles_per_seq = seq_len // tm
    n_work = n_tiles * N_HEADS

    def cur(j):
        return jnp.minimum(j, n_work - 1)

    def prev(j):
        return jnp.maximum(j - 1, 0)

    kern = functools.partial(_inproj_prompt_kernel, tiles_per_seq=tiles_per_seq, n_work=n_work)
    w_specs = _w_in_specs(layer, lambda j: cur(j) % N_HEADS)
    seq_of_cur = lambda j: cur(j) // N_HEADS // tiles_per_seq
    return pl.pallas_call(
        kern,
        grid=(n_work + 1,),
        in_specs=[
            pl.BlockSpec((tm, D_MODEL), lambda j: (cur(j) // N_HEADS, 0)),
            _mod_row_spec(layer, _SC1, row0, seq_of_cur),
            _mod_row_spec(layer, _SH1, row0, seq_of_cur),
            pl.BlockSpec((1, 1, D_MODEL), lambda j: (layer, 0, 0)),
            *w_specs,
            pl.BlockSpec((1, GATE_COLS, D_MODEL), lambda j: (layer, 0, 0)),
            pl.BlockSpec((1, BIAS_ROWS, CHUNK), lambda j: (layer, 0, 0)),
            pl.BlockSpec((1, CONV_W, 256), lambda j: (layer, 0, prev(j) % N_HEADS)),
            pl.BlockSpec((1, 1, 256), lambda j: (layer, 0, prev(j) % N_HEADS)),
            pl.BlockSpec((1, CONV_W - 1, 256),
                         lambda j: (prev(j) // N_HEADS // tiles_per_seq, 0, prev(j) % N_HEADS)),
        ],
        out_specs=[
            pl.BlockSpec((tm, HEAD_COLS), lambda j: (prev(j) // N_HEADS, prev(j) % N_HEADS)),
            pl.BlockSpec((tm // CHUNK, GATE_ROWS, CHUNK), lambda j: (cur(j) // N_HEADS, 0, 0)),
            pl.BlockSpec((1, CONV_W - 1, 256), lambda j: (prev(j) // N_HEADS, 0, prev(j) % N_HEADS)),
        ],
        out_shape=[
            jax.ShapeDtypeStruct((m_rows, N_HEADS * HEAD_COLS), _BF),
            jax.ShapeDtypeStruct((m_rows // CHUNK, GATE_ROWS, CHUNK), _F32),
            jax.ShapeDtypeStruct((n_tiles, CONV_W - 1, D_MODEL), _F32),
        ],
        scratch_shapes=[
            pltpu.VMEM((tm, D_MODEL), _BF),
            pltpu.VMEM((tm, GATE_COLS), _F32),
            pltpu.VMEM((tm, _RAW_COLS), _F32),
            pltpu.VMEM((tm, _RAW_COLS), _F32),
            pltpu.VMEM((tm + 8, 256), _F32),
            pltpu.VMEM((N_HEADS, CONV_W - 1, 256), _F32),
        ],
        compiler_params=_params(1),
        name="inproj_prompt",
    )(x, mod_rows, mod_rows, g, *([w_t] * 7), *([w_tail] * 2), w_gate, bgate, conv_w, conv_b, conv_prev)


def _inproj_sample_kernel(x_ref, sc_ref, sh_ref, g_ref, wxc, wbg, wcg, wq, wk, wv, wo, wga, wgb, wg_ref,
                          cw_ref, cb_ref, cprev_ref, mid_ref, gt_ref, nconv_ref, h_scr, *, seq_len):
    s = pl.program_id(0)
    n_b = sc_ref.shape[2]

    @pl.when(s == 0)
    def _():
        h = _rms_mod(x_ref[...], g_ref[0], _rep_rows(sc_ref[0, 0], seq_len), _rep_rows(sh_ref[0, 0], seq_len))
        h = h.astype(_BF)
        h_scr[...] = h
        gt_ref[...] = _dot_nt(h, wg_ref[0])

    p = _inproj_project(h_scr[...], (wxc, wbg, wcg, wq, wk, wv, wo, wga, wgb))
    u = p["u"]
    prev0 = cprev_ref[0, 0]
    prev1 = cprev_ref[0, 1]
    p1 = jnp.concatenate([prev1, u[0:(seq_len - 1) * n_b]], axis=0)
    p2 = jnp.concatenate([prev0, prev1, u[0:(seq_len - 2) * n_b]], axis=0)
    cw = cw_ref[0]
    conv = cb_ref[0] + cw[0:1] * p2 + cw[1:2] * p1 + cw[2:3] * u
    nconv_ref[0] = u[(seq_len - 2) * n_b:(seq_len - 1) * n_b]
    nconv_ref[1] = u[(seq_len - 1) * n_b:seq_len * n_b]
    _inproj_epilogue(p, conv, mid_ref, _F32)


def _inproj_sample_call(layer, x, mod, g, w_t, w_tail, w_gate, conv_w, conv_b, conv_prev, *, seq_len):
    tm = x.shape[0]
    n_b = tm // seq_len
    kern = functools.partial(_inproj_sample_kernel, seq_len=seq_len)
    w_specs = _w_in_specs(layer, lambda s: s)
    return pl.pallas_call(
        kern,
        grid=(N_HEADS,),
        in_specs=[
            pl.BlockSpec((tm, D_MODEL), lambda s: (0, 0)),
            pl.BlockSpec((1, 1, n_b, D_MODEL), lambda s: (layer, _SC1, 0, 0)),
            pl.BlockSpec((1, 1, n_b, D_MODEL), lambda s: (layer, _SH1, 0, 0)),
            pl.BlockSpec((1, 1, D_MODEL), lambda s: (layer, 0, 0)),
            *w_specs,
            pl.BlockSpec((1, GATE_COLS, D_MODEL), lambda s: (layer, 0, 0)),
            pl.BlockSpec((1, CONV_W, 256), lambda s: (layer, 0, s)),
            pl.BlockSpec((1, 1, 256), lambda s: (layer, 0, s)),
            pl.BlockSpec((1, CONV_W - 1, n_b, 256), lambda s: (layer, 0, 0, s)),
        ],
        out_specs=[
            pl.BlockSpec((tm, HEAD_COLS), lambda s: (0, s)),
            pl.BlockSpec((tm, GATE_COLS), lambda s: (0, 0)),
            pl.BlockSpec((CONV_W - 1, n_b, 256), lambda s: (0, 0, s)),
        ],
        out_shape=[
            jax.ShapeDtypeStruct((tm, N_HEADS * HEAD_COLS), _F32),
            jax.ShapeDtypeStruct((tm, GATE_COLS), _F32),
            jax.ShapeDtypeStruct((CONV_W - 1, n_b, D_MODEL), _F32),
        ],
        scratch_shapes=[pltpu.VMEM((tm, D_MODEL), _BF)],
        compiler_params=_params(1),
        name="inproj_sample",
    )(x, mod, mod, g, *([w_t] * 7), *([w_tail] * 2), w_gate, conv_w, conv_b, conv_prev)


def _causal_mask(lq):
    row_id = lax.broadcasted_iota(jnp.int32, (lq, CHUNK), 0)
    col_id = lax.broadcasted_iota(jnp.int32, (lq, CHUNK), 1)
    return col_id <= row_id


def _chunk_local(a, causal):
    dmat = jnp.where(causal, jnp.exp(a["g_row"] - a["m_col"]), 0.0)
    s_bf = (_dot_nt(a["q"], a["k"]) * dmat).astype(_BF)
    kw = a["k"].astype(_F32) * a["ws_col"]
    return s_bf, _dot(kw.T.astype(_BF), a["v_ext"])


def _chunk_output(a, s_bf, d_state, cx_ref, h, hng_h):
    q_inter = (a["q"].astype(_F32) * a["inter_col"]).astype(_BF)
    cx = cx_ref[h]
    rhs = jnp.concatenate([a["v_ext"], cx.astype(_BF)], axis=0)
    res = _dot(jnp.concatenate([s_bf, q_inter], axis=1), rhs)
    cx_ref[h] = a["dec"] * cx + d_state
    num = res[:, :DV]
    rden = 1.0 / jnp.maximum(jnp.abs(res[:, DV:]), a["em_col"])
    sq = jnp.sum(num * num, axis=-1, keepdims=True) * (1.0 / DV)
    scale = rden * lax.rsqrt(rden * rden * sq + EPS)
    hn = num * jnp.concatenate([scale, scale], axis=1) * hng_h
    return a["ya"] + a["og"] * hn


def _mlstm_prompt_kernel(mid_ref, gt_ref, hng_ref, c0_ref, n0_ref, m0_ref, cbuf_ref,
                         mg_ref, c_ref, n_ref, m_ref, cx_ref, cols_scr, rows_scr):
    del cbuf_ref
    t = pl.program_id(1)

    @pl.when(t == 0)
    def _():
        m_ref[...] = m0_ref[...]
        for h in range(N_HEADS):
            cx_ref[h, :, 0:DV] = c0_ref[0, h]
            cx_ref[h, :, DV:EXT_COLS] = jnp.broadcast_to(n0_ref[0, h:h + 1, :], (CHUNK, DK)).T

    n_chunks = mid_ref.shape[0] // CHUNK
    rows = n_chunks * 8
    hng = hng_ref[0]
    gates = gt_ref[...]
    b = gates[:, 0:8, :].reshape(rows, CHUNK)
    g = gates[:, 8:16, :].reshape(rows, CHUNK)
    cm = gates[:, 16:24, :].reshape(rows, CHUNK)
    b_last = jnp.broadcast_to(b[:, CHUNK - 1:CHUNK], (rows, CHUNK))
    cm_last = jnp.broadcast_to(cm[:, CHUNK - 1:CHUNK], (rows, CHUNK))

    m_prev = m_ref[0]
    m_prevs = []
    for c in range(n_chunks):
        m_prevs.append(m_prev)
        m_prev = b_last[c * 8:(c + 1) * 8] + jnp.maximum(cm_last[c * 8:(c + 1) * 8], m_prev)
    m_ref[0] = m_prev
    m_prev_all = jnp.concatenate(m_prevs, axis=0)

    m_run = jnp.maximum(cm, m_prev_all)
    inter = jnp.exp(m_prev_all - m_run)
    em = jnp.exp(-(b + m_run))
    m_last = jnp.maximum(cm_last, m_prev_all)
    ws = jnp.exp(g - m_last)
    decay = jnp.exp(m_prev_all - m_last)

    pad = jnp.zeros((CHUNK - 32, CHUNK), _F32)
    for c in range(n_chunks):
        sl = slice(c * 8, (c + 1) * 8)
        cols_scr[c] = jnp.concatenate([m_run[sl], inter[sl], em[sl], ws[sl], pad], axis=0).T
        rows_scr[c, 0:8] = g[sl]
        rows_scr[c, 8:16] = decay[sl]

    causal = _causal_mask(CHUNK)
    ones = jnp.ones((CHUNK, CHUNK), _BF)

    def chunk_body(c, carry):
        cols = cols_scr[c]
        vec = rows_scr[c]
        rows_c = pl.ds(pl.multiple_of(c * CHUNK, CHUNK), CHUNK)
        heads = []
        for h in range(N_HEADS):
            mb = h * HEAD_COLS
            v = mid_ref[rows_c, mb + _M_V:mb + _M_V + 256]
            heads.append(dict(
                q=mid_ref[rows_c, mb + _M_Q:mb + _M_Q + 128],
                k=mid_ref[rows_c, mb + _M_K:mb + _M_K + 128],
                v_ext=jnp.concatenate([v, ones], axis=1),
                ya=mid_ref[rows_c, mb + _M_YA:mb + _M_YA + 256].astype(_F32),
                og=mid_ref[rows_c, mb + _M_OG:mb + _M_OG + 256].astype(_F32),
                g_row=vec[h:h + 1, :],
                m_col=cols[:, h:h + 1],
                inter_col=cols[:, 8 + h:9 + h],
                em_col=cols[:, 16 + h:17 + h],
                ws_col=cols[:, 24 + h:25 + h],
                dec=vec[8 + h:9 + h, 0:1]))
        outs = []
        for h, a in enumerate(heads):
            s_bf, d_state = _chunk_local(a, causal)
            outs.append(_chunk_output(a, s_bf, d_state, cx_ref, h, hng[:, h * DV:(h + 1) * DV]))
        mg_ref[rows_c, :] = jnp.concatenate(outs, axis=1).astype(_BF)
        return carry

    lax.fori_loop(0, n_chunks, chunk_body, 0)

    @pl.when(t == pl.num_programs(1) - 1)
    def _():
        for h in range(N_HEADS):
            c_ref[0, 0, h] = cx_ref[h, :, 0:DV]
            n_ref[0, h:h + 1, :] = cx_ref[h, :, DV:EXT_COLS].T[0:1, :]


def _mlstm_prompt_call(layer, mid, gates_t, hng, c0, n0, m0, c_buf, *, seq_len, tb):
    m_rows = mid.shape[0]
    n_seq = m_rows // seq_len
    steps = seq_len // tb
    aliases = {} if c_buf is None else {6: 1}
    if c_buf is None:
        c_buf = jnp.zeros((1,), _F32)
    return pl.pallas_call(
        _mlstm_prompt_kernel,
        grid=(n_seq, steps),
        in_specs=[
            pl.BlockSpec((tb, N_HEADS * HEAD_COLS), lambda b, t: (b * steps + t, 0)),
            pl.BlockSpec((tb // CHUNK, GATE_ROWS, CHUNK), lambda b, t: (b * steps + t, 0, 0)),
            pl.BlockSpec((1, 1, D_MODEL), lambda b, t: (layer, 0, 0)),
            pl.BlockSpec((1, N_HEADS, DK, DV), lambda b, t: (b, 0, 0, 0)),
            pl.BlockSpec((1, N_HEADS, DK), lambda b, t: (b, 0, 0)),
            pl.BlockSpec((1, 8, CHUNK), lambda b, t: (b, 0, 0)),
            pl.BlockSpec(memory_space=pl.ANY),
        ],
        out_specs=[
            pl.BlockSpec((tb, D_MODEL), lambda b, t: (b * steps + t, 0)),
            pl.BlockSpec((1, 1, N_HEADS, DK, DV), lambda b, t: (layer, b, 0, 0, 0)),
            pl.BlockSpec((1, N_HEADS, DK), lambda b, t: (b, 0, 0)),
            pl.BlockSpec((1, 8, CHUNK), lambda b, t: (b, 0, 0)),
        ],
        out_shape=[
            jax.ShapeDtypeStruct((m_rows, D_MODEL), _BF),
            jax.ShapeDtypeStruct((DEPTH, n_seq, N_HEADS, DK, DV), _F32),
            jax.ShapeDtypeStruct((n_seq, N_HEADS, DK), _F32),
            jax.ShapeDtypeStruct((n_seq, 8, CHUNK), _F32),
        ],
        scratch_shapes=[
            pltpu.VMEM((N_HEADS, DK, EXT_COLS), _F32),
            pltpu.VMEM((tb // CHUNK, CHUNK, CHUNK), _F32),
            pltpu.VMEM((tb // CHUNK, 16, CHUNK), _F32),
        ],
        input_output_aliases=aliases,
        compiler_params=_params(2),
        name="mlstm_prompt",
    )(mid, gates_t, hng, c0, n0, m0, c_buf)


def _mlstm_sample_kernel(mid_ref, gt_ref, bias_ref, hng_ref, c0_ref, n0_ref, m0_ref, cbuf_ref,
                         mg_ref, c_ref, n_ref, m_ref,
                         qs_scr, kk_scr, wv_scr, rs_scr, dec_scr, lhs_q, lhs_k, lhs_wv):
    del cbuf_ref
    seq_len, bb, _ = mid_ref.shape
    hng = hng_ref[0]
    bias = bias_ref[0]
    m_prev = m0_ref[0]

    b_t, g_t, cm_t = [], [], []
    for t in range(seq_len):
        pre = _softcap(gt_ref[t] + bias)
        logf = _log_sigmoid(pre[:, GATE_F_COL:GATE_F_COL + CHUNK])
        b_t.append(logf if t == 0 else b_t[-1] + logf)
        g_t.append(pre[:, 0:CHUNK] - b_t[-1])
        cm_t.append(g_t[-1] if t == 0 else jnp.maximum(cm_t[-1], g_t[-1]))
    m_run = [jnp.maximum(cm, m_prev) for cm in cm_t]
    inter = [jnp.exp(m_prev - mr) for mr in m_run]
    em = [jnp.exp(-(b + mr)) for b, mr in zip(b_t, m_run)]
    m_last = m_run[-1]
    ws = [jnp.exp(g - m_last) for g in g_t]
    decay = jnp.exp(m_prev - m_last)
    m_ref[...] = b_t[-1] + m_last
    dec_scr[...] = decay

    lhs_q[...] = jnp.zeros(lhs_q.shape, _F32)
    lhs_k[...] = jnp.zeros(lhs_k.shape, _F32)
    lhs_wv[...] = jnp.zeros(lhs_wv.shape, _F32)
    n_all = n0_ref[0]
    den_part = {}
    for h in range(N_HEADS):
        def col(x):
            return x[:, h:h + 1]
        mb = h * HEAD_COLS
        q = [mid_ref[t, :, mb + _M_Q:mb + _M_Q + DK] for t in range(seq_len)]
        k = [mid_ref[t, :, mb + _M_K:mb + _M_K + DK] for t in range(seq_len)]
        v = [mid_ref[t, :, mb + _M_V:mb + _M_V + DV] for t in range(seq_len)]
        n_h = n_all[:, h * DK:(h + 1) * DK]
        n_new = col(decay) * n_h
        for t in range(seq_len):
            qs = q[t] * col(inter[t])
            qs_scr[t, :, h * DK:(h + 1) * DK] = qs
            kk_scr[t, :, h * DK:(h + 1) * DK] = k[t]
            wv_scr[t, :, h * DV:(h + 1) * DV] = col(ws[t]) * v[t]
            n_new = n_new + col(ws[t]) * k[t]
            den = jnp.sum(qs * n_h, axis=-1, keepdims=True)
            num = None
            for s in range(t + 1):
                w = jnp.sum(q[t] * k[s], axis=-1, keepdims=True) * jnp.exp(col(g_t[s]) - col(m_run[t]))
                num = w * v[s] if num is None else num + w * v[s]
                den = den + w
            mg_ref[t, :, h * DV:(h + 1) * DV] = num
            den_part[h, t] = den
        n_ref[:, h * DK:(h + 1) * DK] = n_new

    def body(i, carry):
        for t in range(seq_len):
            lhs_q[t:t + 1, :] = qs_scr[t, pl.ds(i, 1), :]
            lhs_k[t:t + 1, :] = kk_scr[t, pl.ds(i, 1), :]
            lhs_wv[t:t + 1, :] = wv_scr[t, pl.ds(i, 1), :]
        for h in range(N_HEADS):
            c_old = c0_ref[0, i, h]
            r = _dot(lhs_q[:, h * DK:(h + 1) * DK].astype(_BF), c_old.astype(_BF))
            for t in range(seq_len):
                rs_scr[t, pl.ds(i, 1), h * DV:(h + 1) * DV] = r[t:t + 1, :]
            d_c = lax.dot_general(lhs_k[:, h * DK:(h + 1) * DK].astype(_BF),
                                  lhs_wv[:, h * DV:(h + 1) * DV].astype(_BF),
                                  (((0,), (0,)), ((), ())), preferred_element_type=_F32)
            c_ref[0, i, h] = dec_scr[pl.ds(i, 1), h:h + 1] * c_old + d_c
        return carry

    lax.fori_loop(0, bb, body, 0)

    for h in range(N_HEADS):
        mb = h * HEAD_COLS
        for t in range(seq_len):
            num = mg_ref[t, :, h * DV:(h + 1) * DV] + rs_scr[t, :, h * DV:(h + 1) * DV]
            hraw = num * (1.0 / jnp.maximum(jnp.abs(den_part[h, t]), em[t][:, h:h + 1]))
            ms = jnp.mean(hraw * hraw, axis=-1, keepdims=True)
            hn = hraw * lax.rsqrt(ms + EPS) * hng[:, h * DV:(h + 1) * DV]
            ya = mid_ref[t, :, mb + _M_YA:mb + _M_YA + DV]
            og = mid_ref[t, :, mb + _M_OG:mb + _M_OG + DV]
            mg_ref[t, :, h * DV:(h + 1) * DV] = ya + og * hn


def _mlstm_sample_call(layer, mid3, gates3, bias, hng, c0, n0, m0, c_buf, *, bb):
    seq_len, n_seq, _ = mid3.shape
    aliases = {} if c_buf is None else {7: 1}
    if c_buf is None:
        c_buf = jnp.zeros((1,), _F32)
    return pl.pallas_call(
        _mlstm_sample_kernel,
        grid=(n_seq // bb,),
        in_specs=[
            pl.BlockSpec((seq_len, bb, N_HEADS * HEAD_COLS), lambda i: (0, i, 0)),
            pl.BlockSpec((seq_len, bb, GATE_COLS), lambda i: (0, i, 0)),
            pl.BlockSpec((1, 1, GATE_COLS), lambda i: (layer, 0, 0)),
            pl.BlockSpec((1, 1, D_MODEL), lambda i: (layer, 0, 0)),
            pl.BlockSpec((1, bb, N_HEADS, DK, DV), lambda i: (layer, i, 0, 0, 0)),
            pl.BlockSpec((1, bb, N_HEADS * DK), lambda i: (layer, i, 0)),
            pl.BlockSpec((1, bb, CHUNK), lambda i: (layer, i, 0)),
            pl.BlockSpec(memory_space=pl.ANY),
        ],
        out_specs=[
            pl.BlockSpec((seq_len, bb, D_MODEL), lambda i: (0, i, 0)),
            pl.BlockSpec((1, bb, N_HEADS, DK, DV), lambda i: (layer, i, 0, 0, 0)),
            pl.BlockSpec((bb, N_HEADS * DK), lambda i: (i, 0)),
            pl.BlockSpec((bb, CHUNK), lambda i: (i, 0)),
        ],
        out_shape=[
            jax.ShapeDtypeStruct((seq_len, n_seq, D_MODEL), _F32),
            jax.ShapeDtypeStruct((DEPTH, n_seq, N_HEADS, DK, DV), _F32),
            jax.ShapeDtypeStruct((n_seq, N_HEADS * DK), _F32),
            jax.ShapeDtypeStruct((n_seq, CHUNK), _F32),
        ],
        scratch_shapes=[
            pltpu.VMEM((seq_len, bb, N_HEADS * DK), _F32),
            pltpu.VMEM((seq_len, bb, N_HEADS * DK), _F32),
            pltpu.VMEM((seq_len, bb, N_HEADS * DV), _F32),
            pltpu.VMEM((seq_len, bb, N_HEADS * DV), _F32),
            pltpu.VMEM((bb, CHUNK), _F32),
            pltpu.VMEM((8, N_HEADS * DK), _F32),
            pltpu.VMEM((8, N_HEADS * DK), _F32),
            pltpu.VMEM((8, N_HEADS * DV), _F32),
        ],
        input_output_aliases=aliases,
        compiler_params=_params(1),
        name="mlstm_sample",
    )(mid3, gates3, bias, hng, c0, n0, m0, c_buf)


def _mlp_kernel(x_ref, mg_ref, gt1_ref, sc2_ref, sh2_ref, gt2_ref, g2_ref, wo_ref, wu_ref, wd_ref, gf_ref,
                o_ref, xmid, h2, acc, *, mod_reps, final_norm):
    f = pl.program_id(1)

    def mod(ref):
        return ref[0, 0, 0] if mod_reps is None else _rep_rows(ref[0, 0], mod_reps)

    @pl.when(f == 0)
    def _():
        mix = _dot(mg_ref[...].astype(_BF), wo_ref[0])
        xm = x_ref[...] + mod(gt1_ref) * mix
        xmid[...] = xm
        h2[...] = _rms_mod(xm, g2_ref[0], mod(sc2_ref), mod(sh2_ref)).astype(_BF)
        acc[...] = jnp.zeros(acc.shape, _F32)

    a = jnp.maximum(_dot(h2[...], wu_ref[0]), 0.0)
    acc[...] += _dot((a * a).astype(_BF), wd_ref[0])

    @pl.when(f == pl.num_programs(1) - 1)
    def _():
        y = xmid[...] + mod(gt2_ref) * acc[...]
        if final_norm:
            ms = jnp.mean(y * y, axis=-1, keepdims=True)
            y = y * lax.rsqrt(ms + EPS) * gf_ref[...]
        o_ref[...] = y


def _mlp_call(layer, x, merged, mod, row0, g2, w_out, w_up, w_down, g_final, *, tm, tf, seq_len, final_norm):
    m_rows = x.shape[0]
    pieces = (_GT1, _SC2, _SH2, _GT2)
    if seq_len >= tm:
        tiles_per_seq = seq_len // tm
        mod_reps = None
        mod_specs = [_mod_row_spec(layer, p, row0, lambda i, f: i // tiles_per_seq) for p in pieces]
    else:
        mod_reps = seq_len
        mod_specs = [pl.BlockSpec((1, 1, tm // seq_len, D_MODEL), functools.partial(lambda p, i, f: (layer, p, 0, 0), p))
                     for p in pieces]
    kern = functools.partial(_mlp_kernel, mod_reps=mod_reps, final_norm=final_norm)
    return pl.pallas_call(
        kern,
        grid=(m_rows // tm, D_FF // tf),
        in_specs=[
            pl.BlockSpec((tm, D_MODEL), lambda i, f: (i, 0)),
            pl.BlockSpec((tm, D_MODEL), lambda i, f: (i, 0)),
            *mod_specs,
            pl.BlockSpec((1, 1, D_MODEL), lambda i, f: (layer, 0, 0)),
            pl.BlockSpec((1, D_MODEL, D_MODEL), lambda i, f: (layer, 0, 0)),
            pl.BlockSpec((1, D_MODEL, tf), lambda i, f: (layer, 0, f)),
            pl.BlockSpec((1, tf, D_MODEL), lambda i, f: (layer, f, 0)),
            pl.BlockSpec((1, D_MODEL), lambda i, f: (0, 0)),
        ],
        out_specs=pl.BlockSpec((tm, D_MODEL), lambda i, f: (i, 0)),
        out_shape=jax.ShapeDtypeStruct((m_rows, D_MODEL), _F32),
        scratch_shapes=[
            pltpu.VMEM((tm, D_MODEL), _F32),
            pltpu.VMEM((tm, D_MODEL), _BF),
            pltpu.VMEM((tm, D_MODEL), _F32),
        ],
        compiler_params=_params(2),
        name="outproj_mlp",
    )(x, merged, mod, mod, mod, mod, g2, w_out, w_up, w_down, g_final)


def kernel(x_prompt, x_sample, state_conv, state_C, state_n, state_m, c_prompt, c_sample,
           w_ada, b_ada, g_norm1, g_norm2, w_in, b_gate, conv_w, conv_b, hn_g, w_out, w_up, w_down, g_final):
    n_p, seq_p, _ = x_prompt.shape
    n_s, seq_s, _ = x_sample.shape
    rows_s = n_s * seq_s
    tm_p = 1024

    w_t = jnp.swapaxes(w_in, 1, 2).astype(_BF)
    w_tail = w_t[:, _OFF_GA:]
    zrows = jnp.zeros((DEPTH, GATE_F_COL - N_HEADS, D_MODEL), _BF)
    w_gate = jnp.concatenate([w_t[:, _OFF_IG:_OFF_IG + N_HEADS], zrows,
                              w_t[:, _OFF_IG + N_HEADS:_OFF_GA], zrows], axis=1)
    w_out_b = w_out.astype(_BF)
    w_up_b = w_up.astype(_BF)
    w_down_b = w_down.astype(_BF)
    zb = jnp.zeros((DEPTH, 4), _F32)
    bgate = jnp.broadcast_to(
        jnp.concatenate([b_gate[:, :N_HEADS], zb, b_gate[:, N_HEADS:], zb], axis=-1)[:, :, None],
        (DEPTH, BIAS_ROWS, CHUNK))
    zb = jnp.zeros((DEPTH, GATE_F_COL - N_HEADS), _F32)
    bias_row = jnp.concatenate([b_gate[:, :N_HEADS], zb, b_gate[:, N_HEADS:], zb], axis=-1)[:, None, :]
    g1 = g_norm1.reshape(DEPTH, 1, D_MODEL)
    g2 = g_norm2.reshape(DEPTH, 1, D_MODEL)
    hng = hn_g.reshape(DEPTH, 1, D_MODEL)
    cb = conv_b.reshape(DEPTH, 1, D_MODEL)
    gfin = g_final.reshape(1, D_MODEL)

    mod = _ada_call(jnp.concatenate([c_sample, c_prompt], axis=0), w_ada, b_ada)
    mod_rows = mod.reshape(DEPTH, N_MOD, n_s + n_p, 1, D_MODEL)

    xp = x_prompt.reshape(n_p * seq_p, D_MODEL)
    xs = x_sample.transpose(1, 0, 2).reshape(rows_s, D_MODEL)
    conv_s_in = state_conv.transpose(0, 2, 1, 3)
    n_s_in = state_n.reshape(DEPTH, n_s, N_HEADS * DK)
    m_s_in = jnp.pad(state_m, ((0, 0), (0, 0), (0, CHUNK - N_HEADS)))

    zeros_conv = jnp.zeros((n_p, CONV_W - 1, D_MODEL), _F32)
    zeros_c = jnp.zeros((n_p, N_HEADS, DK, DV), _F32)
    zeros_n = jnp.zeros((n_p, N_HEADS, DK), _F32)
    zeros_m = jnp.zeros((n_p, 8, CHUNK), _F32)

    p_conv, p_n, p_m, s_conv, s_n, s_m = [], [], [], [], [], []
    p_c = s_c = None
    for l in range(DEPTH):
        final = l == DEPTH - 1

        mid, gates_t, nconv = _inproj_prompt_call(
            l, xp, mod_rows, n_s, g1, w_t, w_tail, w_gate, bgate, conv_w, cb, zeros_conv, seq_len=seq_p, tm=tm_p)
        merged, p_c, n1, m1 = _mlstm_prompt_call(
            l, mid, gates_t, hng, zeros_c, zeros_n, zeros_m, p_c, seq_len=seq_p, tb=1024)
        xp = _mlp_call(l, xp, merged, mod_rows, n_s, g2, w_out_b, w_up_b, w_down_b, gfin,
                       tm=512, tf=2048, seq_len=seq_p, final_norm=final)
        tiles_per_seq = seq_p // tm_p
        p_conv.append(nconv[tiles_per_seq - 1::tiles_per_seq]); p_n.append(n1); p_m.append(m1[:, :N_HEADS, 0])

        mid, gates, nconv = _inproj_sample_call(
            l, xs, mod, g1, w_t, w_tail, w_gate, conv_w, cb, conv_s_in, seq_len=seq_s)
        merged3, s_c, n1, m1 = _mlstm_sample_call(
            l, mid.reshape(seq_s, n_s, N_HEADS * HEAD_COLS), gates.reshape(seq_s, n_s, GATE_COLS), bias_row, hng,
            state_C, n_s_in, m_s_in, s_c, bb=16)
        xs = _mlp_call(l, xs, merged3.reshape(rows_s, D_MODEL), mod, 0, g2, w_out_b, w_up_b, w_down_b, gfin,
                       tm=rows_s, tf=1024, seq_len=seq_s, final_norm=final)
        s_conv.append(nconv.transpose(1, 0, 2))
        s_n.append(n1.reshape(n_s, N_HEADS, DK)); s_m.append(m1[:, :N_HEADS])

    return (xp.reshape(n_p, seq_p, D_MODEL), xs.reshape(seq_s, n_s, D_MODEL).transpose(1, 0, 2),
            jnp.stack(p_conv), p_c, jnp.stack(p_n), jnp.stack(p_m),
            jnp.stack(s_conv), s_c, jnp.stack(s_n), jnp.stack(s_m))
```

```python
import functools

import jax
import jax.numpy as jnp
from jax import lax
from jax.experimental import pallas as pl
from jax.experimental.pallas import tpu as pltpu

D_MODEL = 1024
N_HEADS = 4
DK = 128
DV = 256
D_FF = 4096
DEPTH = 4
CONV_W = 3
GATE_CAP = 15.0
EPS = 1e-6
CHUNK = 128
HEAD_COLS = 1024
GATE_COLS = 256
GATE_F_COL = 128
BIAS_ROWS = 16
GATE_ROWS = 24
EXT_COLS = DV + CHUNK
VMEM_LIMIT = 56 * 1024 * 1024

_OFF_XC, _OFF_BG, _OFF_CG = 0, 1024, 2048
_OFF_Q, _OFF_K, _OFF_V, _OFF_O = 3072, 3584, 4096, 5120
_OFF_IG, _OFF_GA, _OFF_GB = 6144, 6152, 7176

_M_YA, _M_Q, _M_K, _M_V, _M_OG = 0, 256, 384, 512, 768

_BF = jnp.bfloat16
_F32 = jnp.float32


def _dot(a, b):
    return jnp.dot(a, b, preferred_element_type=_F32)


def _dot_nt(a, b):
    return lax.dot_general(a, b, (((1,), (1,)), ((), ())), preferred_element_type=_F32)


def _sigmoid(x):
    return 0.5 * jnp.tanh(0.5 * x) + 0.5


def _rms_mod(x, g, sc, sh):
    ms = jnp.mean(x * x, axis=-1, keepdims=True)
    return (x * lax.rsqrt(ms + EPS)) * (g * (1.0 + sc)) + sh


def _rep_rows(v, reps):
    return v if reps == 1 else jnp.concatenate([v] * reps, axis=0)


def _params(n_axes):
    return pltpu.CompilerParams(dimension_semantics=("arbitrary",) * n_axes, vmem_limit_bytes=VMEM_LIMIT)


N_MOD = 6
_SH1, _SC1, _GT1, _SH2, _SC2, _GT2 = range(N_MOD)


def _ada_kernel(c_ref, w_ref, b_ref, o_ref):
    c = c_ref[...]
    a = (c * _sigmoid(c)).astype(_BF)
    o_ref[0, 0] = _dot(a, w_ref[0].astype(_BF)) + b_ref[0]


def _ada_call(c_all, w_ada, b_ada):
    n_rows = c_all.shape[0]
    return pl.pallas_call(
        _ada_kernel,
        grid=(DEPTH, N_MOD),
        in_specs=[
            pl.BlockSpec((n_rows, D_MODEL), lambda l, j: (0, 0)),
            pl.BlockSpec((1, D_MODEL, D_MODEL), lambda l, j: (l, 0, j)),
            pl.BlockSpec((1, 1, D_MODEL), lambda l, j: (l, 0, j)),
        ],
        out_specs=pl.BlockSpec((1, 1, n_rows, D_MODEL), lambda l, j: (l, j, 0, 0)),
        out_shape=jax.ShapeDtypeStruct((DEPTH, N_MOD, n_rows, D_MODEL), _F32),
        compiler_params=_params(2),
        name="ada_mod",
    )(c_all, w_ada, b_ada.reshape(DEPTH, 1, N_MOD * D_MODEL))


def _w_in_specs(layer, idx):
    def spec(width, offset):
        return pl.BlockSpec((1, width, D_MODEL), lambda *g: (layer, offset // width + idx(*g), 0))
    return [spec(256, _OFF_XC), spec(256, _OFF_BG), spec(256, _OFF_CG), spec(128, _OFF_Q), spec(128, _OFF_K),
            spec(256, _OFF_V), spec(256, _OFF_O), spec(256, 0), spec(256, _OFF_GB - _OFF_GA)]


def _scan_lanes(x, combine, identity, seg_len):
    pos = lax.broadcasted_iota(jnp.int32, x.shape, 1) % seg_len
    shift = 1
    while shift < seg_len:
        x = combine(x, jnp.where(pos >= shift, pltpu.roll(x, shift, 1), identity))
        shift *= 2
    return x


def _softcap(a):
    return GATE_CAP * jnp.tanh(a / GATE_CAP)


def _log_sigmoid(x):
    return -(jnp.maximum(-x, 0.0) + jnp.log(1.0 + jnp.exp(-jnp.abs(x))))


def _gate_rows_prompt(gates, bias, gt_ref, first_chunk):
    n_chunks = gates.shape[0] // CHUNK
    gates_t = gates.T

    def stack(r0):
        return jnp.concatenate([gates_t[r0:r0 + 8, c * CHUNK:(c + 1) * CHUNK] for c in range(n_chunks)], axis=0)

    i_pre = _softcap(stack(0) + jnp.concatenate([bias[0:8]] * n_chunks, axis=0))
    f_pre = _softcap(stack(GATE_F_COL) + jnp.concatenate([bias[8:16]] * n_chunks, axis=0))
    b = _scan_lanes(_log_sigmoid(f_pre), jnp.add, 0.0, CHUNK)
    g = i_pre - b
    cm = _scan_lanes(g, jnp.maximum, -3e38, CHUNK)
    for c in range(n_chunks):
        gt_ref[first_chunk + c, 0:8] = b[c * 8:(c + 1) * 8]
        gt_ref[first_chunk + c, 8:16] = g[c * 8:(c + 1) * 8]
        gt_ref[first_chunk + c, 16:24] = cm[c * 8:(c + 1) * 8]


def _inproj_project(h, w_refs):
    wxc, wbg, wcg, wq, wk, wv, wo, wga, wgb = w_refs
    xc = _dot_nt(h, wxc[0])
    cg = _dot_nt(h, wcg[0])
    qk = _dot_nt(h, jnp.concatenate([wq[0], wk[0]], axis=0))
    return dict(u=cg * xc, bg=_dot_nt(h, wbg[0]), ga=_dot_nt(h, wga[0]), qk=qk,
                v=_dot_nt(h, wv[0]), o=_dot_nt(h, wo[0]), gb=_dot_nt(h, wgb[0]))


def _inproj_epilogue(p, conv, mid_ref, out_dtype, r0=0):
    rows = slice(r0, r0 + conv.shape[0])
    mid_ref[rows, _M_YA:_M_YA + 256] = (_sigmoid(p["ga"]) * (p["bg"] * conv)).astype(out_dtype)
    mid_ref[rows, _M_Q:_M_Q + DK] = p["qk"][:, :DK].astype(out_dtype)
    mid_ref[rows, _M_K:_M_K + DK] = (p["qk"][:, DK:] * (DK ** -0.5)).astype(out_dtype)
    mid_ref[rows, _M_V:_M_V + 256] = p["v"].astype(out_dtype)
    mid_ref[rows, _M_OG:_M_OG + 256] = (_sigmoid(p["o"]) * _sigmoid(p["gb"])).astype(out_dtype)


_RAW_COLS = 8 * 256


def _inproj_prompt_kernel(x_ref, sc_ref, sh_ref, g_ref, wxc, wbg, wcg, wq, wk, wv, wo, wga, wgb, wg_ref,
                          bg_ref, cw_ref, cb_ref, cprev_ref, mid_ref, gt_ref, nconv_ref,
                          h_scr, gate_scr, raw_even, raw_odd, ubuf, carry, *, tiles_per_seq, n_work):
    j = pl.program_id(0)
    tm = x_ref.shape[0]
    cur = jnp.minimum(j, n_work - 1)
    s = cur % N_HEADS
    prev = jnp.maximum(j - 1, 0)
    tile_p = prev // N_HEADS
    s_p = prev % N_HEADS
    chunks_per_step = (tm // CHUNK) // N_HEADS
    raw = (raw_even, raw_odd)

    @pl.when(j == 0)
    def _():
        raw_odd[...] = jnp.zeros(raw_odd.shape, _F32)
        carry[...] = jnp.zeros(carry.shape, _F32)

    @pl.when(jnp.logical_and(s == 0, j < n_work))
    def _():
        h = _rms_mod(x_ref[...], g_ref[0], sc_ref[0, 0, 0], sh_ref[0, 0, 0]).astype(_BF)
        h_scr[...] = h
        gate_scr[...] = _dot_nt(h, wg_ref[0])

    def step(slot):
        rows0 = pl.multiple_of(s * (chunks_per_step * CHUNK), CHUNK)
        _gate_rows_prompt(gate_scr[pl.ds(rows0, chunks_per_step * CHUNK), :], bg_ref[0], gt_ref,
                          s * chunks_per_step)

        h = h_scr[...]
        weights = (wxc[0], wcg[0], wbg[0], wga[0], jnp.concatenate([wq[0], wk[0]], axis=0),
                   wv[0], wo[0], wgb[0])
        seq_start = (tile_p % tiles_per_seq) == 0
        ubuf[6:8, :] = jnp.where(seq_start, cprev_ref[0], carry[s_p])
        cw = cw_ref[0]
        conv_bias = cb_ref[0]
        rows_per_part = tm // len(weights)
        rc = min(CHUNK // 2, rows_per_part)
        for idx, w in enumerate(weights):
            raw[slot][:, idx * 256:(idx + 1) * 256] = _dot_nt(h, w)
            for r0 in range(idx * rows_per_part, (idx + 1) * rows_per_part, rc):
                def piece(k):
                    return raw[1 - slot][r0:r0 + rc, k * 256:(k + 1) * 256]

                u = piece(1) * piece(0)
                ubuf[8 + r0:8 + r0 + rc, :] = u
                conv = (conv_bias + cw[0:1] * ubuf[6 + r0:6 + r0 + rc, :]
                        + cw[1:2] * ubuf[7 + r0:7 + r0 + rc, :] + cw[2:3] * u)
                _inproj_epilogue(dict(bg=piece(2), ga=piece(3), qk=piece(4), v=piece(5), o=piece(6),
                                      gb=piece(7)), conv, mid_ref, _BF, r0)
        last2 = ubuf[6 + tm:8 + tm, :]
        carry[s_p] = last2
        nconv_ref[0] = last2

    for parity in range(2):
        pl.when(j % 2 == parity)(functools.partial(step, parity))


def _mod_row_spec(layer, piece, row0, seq_of):
    return pl.BlockSpec((1, 1, 1, 1, D_MODEL), lambda *g: (layer, piece, row0 + seq_of(*g), 0, 0))


def _inproj_prompt_call(layer, x, mod_rows, row0, g, w_t, w_tail, w_gate, bgate, conv_w, conv_b, conv_prev,
                        *, seq_len, tm):
    m_rows = x.shape[0]
    n_tiles = m_rows // tm
    tiles_per_seq = seq_len // tm
    n_work = n_tiles * N_HEADS

    def cur(j):
        return jnp.minimum(j, n_work - 1)

    def prev(j):
        return jnp.maximum(j - 1, 0)

    kern = functools.partial(_inproj_prompt_kernel, tiles_per_seq=tiles_per_seq, n_work=n_work)
    w_specs = _w_in_specs(layer, lambda j: cur(j) % N_HEADS)
    seq_of_cur = lambda j: cur(j) // N_HEADS // tiles_per_seq
    return pl.pallas_call(
        kern,
        grid=(n_work + 1,),
        in_specs=[
            pl.BlockSpec((tm, D_MODEL), lambda j: (cur(j) // N_HEADS, 0)),
            _mod_row_spec(layer, _SC1, row0, seq_of_cur),
            _mod_row_spec(layer, _SH1, row0, seq_of_cur),
            pl.BlockSpec((1, 1, D_MODEL), lambda j: (layer, 0, 0)),
            *w_specs,
            pl.BlockSpec((1, GATE_COLS, D_MODEL), lambda j: (layer, 0, 0)),
            pl.BlockSpec((1, BIAS_ROWS, CHUNK), lambda j: (layer, 0, 0)),
            pl.BlockSpec((1, CONV_W, 256), lambda j: (layer, 0, prev(j) % N_HEADS)),
            pl.BlockSpec((1, 1, 256), lambda j: (layer, 0, prev(j) % N_HEADS)),
            pl.BlockSpec((1, CONV_W - 1, 256),
                         lambda j: (prev(j) // N_HEADS // tiles_per_seq, 0, prev(j) % N_HEADS)),
        ],
        out_specs=[
            pl.BlockSpec((tm, HEAD_COLS), lambda j: (prev(j) // N_HEADS, prev(j) % N_HEADS)),
            pl.BlockSpec((tm // CHUNK, GATE_ROWS, CHUNK), lambda j: (cur(j) // N_HEADS, 0, 0)),
            pl.BlockSpec((1, CONV_W - 1, 256), lambda j: (prev(j) // N_HEADS, 0, prev(j) % N_HEADS)),
        ],
        out_shape=[
            jax.ShapeDtypeStruct((m_rows, N_HEADS * HEAD_COLS), _BF),
            jax.ShapeDtypeStruct((m_rows // CHUNK, GATE_ROWS, CHUNK), _F32),
            jax.ShapeDtypeStruct((n_tiles, CONV_W - 1, D_MODEL), _F32),
        ],
        scratch_shapes=[
            pltpu.VMEM((tm, D_MODEL), _BF),
            pltpu.VMEM((tm, GATE_COLS), _F32),
            pltpu.VMEM((tm, _RAW_COLS), _F32),
            pltpu.VMEM((tm, _RAW_COLS), _F32),
            pltpu.VMEM((tm + 8, 256), _F32),
            pltpu.VMEM((N_HEADS, CONV_W - 1, 256), _F32),
        ],
        compiler_params=_params(1),
        name="inproj_prompt",
    )(x, mod_rows, mod_rows, g, *([w_t] * 7), *([w_tail] * 2), w_gate, bgate, conv_w, conv_b, conv_prev)


def _inproj_sample_kernel(x_ref, sc_ref, sh_ref, g_ref, wxc, wbg, wcg, wq, wk, wv, wo, wga, wgb, wg_ref,
                          cw_ref, cb_ref, cprev_ref, mid_ref, gt_ref, nconv_ref, h_scr, *, seq_len):
    s = pl.program_id(0)
    n_b = sc_ref.shape[2]

    @pl.when(s == 0)
    def _():
        h = _rms_mod(x_ref[...], g_ref[0], _rep_rows(sc_ref[0, 0], seq_len), _rep_rows(sh_ref[0, 0], seq_len))
        h = h.astype(_BF)
        h_scr[...] = h
        gt_ref[...] = _dot_nt(h, wg_ref[0])

    p = _inproj_project(h_scr[...], (wxc, wbg, wcg, wq, wk, wv, wo, wga, wgb))
    u = p["u"]
    prev0 = cprev_ref[0, 0]
    prev1 = cprev_ref[0, 1]
    p1 = jnp.concatenate([prev1, u[0:(seq_len - 1) * n_b]], axis=0)
    p2 = jnp.concatenate([prev0, prev1, u[0:(seq_len - 2) * n_b]], axis=0)
    cw = cw_ref[0]
    conv = cb_ref[0] + cw[0:1] * p2 + cw[1:2] * p1 + cw[2:3] * u
    nconv_ref[0] = u[(seq_len - 2) * n_b:(seq_len - 1) * n_b]
    nconv_ref[1] = u[(seq_len - 1) * n_b:seq_len * n_b]
    _inproj_epilogue(p, conv, mid_ref, _F32)


def _inproj_sample_call(layer, x, mod, g, w_t, w_tail, w_gate, conv_w, conv_b, conv_prev, *, seq_len):
    tm = x.shape[0]
    n_b = tm // seq_len
    kern = functools.partial(_inproj_sample_kernel, seq_len=seq_len)
    w_specs = _w_in_specs(layer, lambda s: s)
    return pl.pallas_call(
        kern,
        grid=(N_HEADS,),
        in_specs=[
            pl.BlockSpec((tm, D_MODEL), lambda s: (0, 0)),
            pl.BlockSpec((1, 1, n_b, D_MODEL), lambda s: (layer, _SC1, 0, 0)),
            pl.BlockSpec((1, 1, n_b, D_MODEL), lambda s: (layer, _SH1, 0, 0)),
            pl.BlockSpec((1, 1, D_MODEL), lambda s: (layer, 0, 0)),
            *w_specs,
            pl.BlockSpec((1, GATE_COLS, D_MODEL), lambda s: (layer, 0, 0)),
            pl.BlockSpec((1, CONV_W, 256), lambda s: (layer, 0, s)),
            pl.BlockSpec((1, 1, 256), lambda s: (layer, 0, s)),
            pl.BlockSpec((1, CONV_W - 1, n_b, 256), lambda s: (layer, 0, 0, s)),
        ],
        out_specs=[
            pl.BlockSpec((tm, HEAD_COLS), lambda s: (0, s)),
            pl.BlockSpec((tm, GATE_COLS), lambda s: (0, 0)),
            pl.BlockSpec((CONV_W - 1, n_b, 256), lambda s: (0, 0, s)),
        ],
        out_shape=[
            jax.ShapeDtypeStruct((tm, N_HEADS * HEAD_COLS), _F32),
            jax.ShapeDtypeStruct((tm, GATE_COLS), _F32),
            jax.ShapeDtypeStruct((CONV_W - 1, n_b, D_MODEL), _F32),
        ],
        scratch_shapes=[pltpu.VMEM((tm, D_MODEL), _BF)],
        compiler_params=_params(1),
        name="inproj_sample",
    )(x, mod, mod, g, *([w_t] * 7), *([w_tail] * 2), w_gate, conv_w, conv_b, conv_prev)


def _causal_mask(lq):
    row_id = lax.broadcasted_iota(jnp.int32, (lq, CHUNK), 0)
    col_id = lax.broadcasted_iota(jnp.int32, (lq, CHUNK), 1)
    return col_id <= row_id


def _chunk_local(a, causal):
    dmat = jnp.where(causal, jnp.exp(a["g_row"] - a["m_col"]), 0.0)
    s_bf = (_dot_nt(a["q"], a["k"]) * dmat).astype(_BF)
    kw = a["k"].astype(_F32) * a["ws_col"]
    return s_bf, _dot(kw.T.astype(_BF), a["v_ext"])


def _chunk_output(a, s_bf, d_state, cx_ref, h, hng_h):
    q_inter = (a["q"].astype(_F32) * a["inter_col"]).astype(_BF)
    cx = cx_ref[h]
    rhs = jnp.concatenate([a["v_ext"], cx.astype(_BF)], axis=0)
    res = _dot(jnp.concatenate([s_bf, q_inter], axis=1), rhs)
    cx_ref[h] = a["dec"] * cx + d_state
    num = res[:, :DV]
    rden = 1.0 / jnp.maximum(jnp.abs(res[:, DV:]), a["em_col"])
    sq = jnp.sum(num * num, axis=-1, keepdims=True) * (1.0 / DV)
    scale = rden * lax.rsqrt(rden * rden * sq + EPS)
    hn = num * jnp.concatenate([scale, scale], axis=1) * hng_h
    return a["ya"] + a["og"] * hn


def _mlstm_prompt_kernel(mid_ref, gt_ref, hng_ref, c0_ref, n0_ref, m0_ref, cbuf_ref,
                         mg_ref, c_ref, n_ref, m_ref, cx_ref, cols_scr, rows_scr):
    del cbuf_ref
    t = pl.program_id(1)

    @pl.when(t == 0)
    def _():
        m_ref[...] = m0_ref[...]
        for h in range(N_HEADS):
            cx_ref[h, :, 0:DV] = c0_ref[0, h]
            cx_ref[h, :, DV:EXT_COLS] = jnp.broadcast_to(n0_ref[0, h:h + 1, :], (CHUNK, DK)).T

    n_chunks = mid_ref.shape[0] // CHUNK
    rows = n_chunks * 8
    hng = hng_ref[0]
    gates = gt_ref[...]
    b = gates[:, 0:8, :].reshape(rows, CHUNK)
    g = gates[:, 8:16, :].reshape(rows, CHUNK)
    cm = gates[:, 16:24, :].reshape(rows, CHUNK)
    b_last = jnp.broadcast_to(b[:, CHUNK - 1:CHUNK], (rows, CHUNK))
    cm_last = jnp.broadcast_to(cm[:, CHUNK - 1:CHUNK], (rows, CHUNK))

    m_prev = m_ref[0]
    m_prevs = []
    for c in range(n_chunks):
        m_prevs.append(m_prev)
        m_prev = b_last[c * 8:(c + 1) * 8] + jnp.maximum(cm_last[c * 8:(c + 1) * 8], m_prev)
    m_ref[0] = m_prev
    m_prev_all = jnp.concatenate(m_prevs, axis=0)

    m_run = jnp.maximum(cm, m_prev_all)
    inter = jnp.exp(m_prev_all - m_run)
    em = jnp.exp(-(b + m_run))
    m_last = jnp.maximum(cm_last, m_prev_all)
    ws = jnp.exp(g - m_last)
    decay = jnp.exp(m_prev_all - m_last)

    pad = jnp.zeros((CHUNK - 32, CHUNK), _F32)
    for c in range(n_chunks):
        sl = slice(c * 8, (c + 1) * 8)
        cols_scr[c] = jnp.concatenate([m_run[sl], inter[sl], em[sl], ws[sl], pad], axis=0).T
        rows_scr[c, 0:8] = g[sl]
        rows_scr[c, 8:16] = decay[sl]

    causal = _causal_mask(CHUNK)
    ones = jnp.ones((CHUNK, CHUNK), _BF)

    def chunk_body(c, carry):
        cols = cols_scr[c]
        vec = rows_scr[c]
        rows_c = pl.ds(pl.multiple_of(c * CHUNK, CHUNK), CHUNK)
        heads = []
        for h in range(N_HEADS):
            mb = h * HEAD_COLS
            v = mid_ref[rows_c, mb + _M_V:mb + _M_V + 256]
            heads.append(dict(
                q=mid_ref[rows_c, mb + _M_Q:mb + _M_Q + 128],
                k=mid_ref[rows_c, mb + _M_K:mb + _M_K + 128],
                v_ext=jnp.concatenate([v, ones], axis=1),
                ya=mid_ref[rows_c, mb + _M_YA:mb + _M_YA + 256].astype(_F32),
                og=mid_ref[rows_c, mb + _M_OG:mb + _M_OG + 256].astype(_F32),
                g_row=vec[h:h + 1, :],
                m_col=cols[:, h:h + 1],
                inter_col=cols[:, 8 + h:9 + h],
                em_col=cols[:, 16 + h:17 + h],
                ws_col=cols[:, 24 + h:25 + h],
                dec=vec[8 + h:9 + h, 0:1]))
        outs = []
        for h, a in enumerate(heads):
            s_bf, d_state = _chunk_local(a, causal)
            outs.append(_chunk_output(a, s_bf, d_state, cx_ref, h, hng[:, h * DV:(h + 1) * DV]))
        mg_ref[rows_c, :] = jnp.concatenate(outs, axis=1).astype(_BF)
        return carry

    lax.fori_loop(0, n_chunks, chunk_body, 0)

    @pl.when(t == pl.num_programs(1) - 1)
    def _():
        for h in range(N_HEADS):
            c_ref[0, 0, h] = cx_ref[h, :, 0:DV]
            n_ref[0, h:h + 1, :] = cx_ref[h, :, DV:EXT_COLS].T[0:1, :]


def _mlstm_prompt_call(layer, mid, gates_t, hng, c0, n0, m0, c_buf, *, seq_len, tb):
    m_rows = mid.shape[0]
    n_seq = m_rows // seq_len
    steps = seq_len // tb
    aliases = {} if c_buf is None else {6: 1}
    if c_buf is None:
        c_buf = jnp.zeros((1,), _F32)
    return pl.pallas_call(
        _mlstm_prompt_kernel,
        grid=(n_seq, steps),
        in_specs=[
            pl.BlockSpec((tb, N_HEADS * HEAD_COLS), lambda b, t: (b * steps + t, 0)),
            pl.BlockSpec((tb // CHUNK, GATE_ROWS, CHUNK), lambda b, t: (b * steps + t, 0, 0)),
            pl.BlockSpec((1, 1, D_MODEL), lambda b, t: (layer, 0, 0)),
            pl.BlockSpec((1, N_HEADS, DK, DV), lambda b, t: (b, 0, 0, 0)),
            pl.BlockSpec((1, N_HEADS, DK), lambda b, t: (b, 0, 0)),
            pl.BlockSpec((1, 8, CHUNK), lambda b, t: (b, 0, 0)),
            pl.BlockSpec(memory_space=pl.ANY),
        ],
        out_specs=[
            pl.BlockSpec((tb, D_MODEL), lambda b, t: (b * steps + t, 0)),
            pl.BlockSpec((1, 1, N_HEADS, DK, DV), lambda b, t: (layer, b, 0, 0, 0)),
            pl.BlockSpec((1, N_HEADS, DK), lambda b, t: (b, 0, 0)),
            pl.BlockSpec((1, 8, CHUNK), lambda b, t: (b, 0, 0)),
        ],
        out_shape=[
            jax.ShapeDtypeStruct((m_rows, D_MODEL), _BF),
            jax.ShapeDtypeStruct((DEPTH, n_seq, N_HEADS, DK, DV), _F32),
            jax.ShapeDtypeStruct((n_seq, N_HEADS, DK), _F32),
            jax.ShapeDtypeStruct((n_seq, 8, CHUNK), _F32),
        ],
        scratch_shapes=[
            pltpu.VMEM((N_HEADS, DK, EXT_COLS), _F32),
            pltpu.VMEM((tb // CHUNK, CHUNK, CHUNK), _F32),
            pltpu.VMEM((tb // CHUNK, 16, CHUNK), _F32),
        ],
        input_output_aliases=aliases,
        compiler_params=_params(2),
        name="mlstm_prompt",
    )(mid, gates_t, hng, c0, n0, m0, c_buf)


def _mlstm_sample_kernel(mid_ref, gt_ref, bias_ref, hng_ref, c0_ref, n0_ref, m0_ref, cbuf_ref,
                         mg_ref, c_ref, n_ref, m_ref,
                         qs_scr, kk_scr, wv_scr, rs_scr, dec_scr, lhs_q, lhs_k, lhs_wv):
    del cbuf_ref
    seq_len, bb, _ = mid_ref.shape
    hng = hng_ref[0]
    bias = bias_ref[0]
    m_prev = m0_ref[0]

    b_t, g_t, cm_t = [], [], []
    for t in range(seq_len):
        pre = _softcap(gt_ref[t] + bias)
        logf = _log_sigmoid(pre[:, GATE_F_COL:GATE_F_COL + CHUNK])
        b_t.append(logf if t == 0 else b_t[-1] + logf)
        g_t.append(pre[:, 0:CHUNK] - b_t[-1])
        cm_t.append(g_t[-1] if t == 0 else jnp.maximum(cm_t[-1], g_t[-1]))
    m_run = [jnp.maximum(cm, m_prev) for cm in cm_t]
    inter = [jnp.exp(m_prev - mr) for mr in m_run]
    em = [jnp.exp(-(b + mr)) for b, mr in zip(b_t, m_run)]
    m_last = m_run[-1]
    ws = [jnp.exp(g - m_last) for g in g_t]
    decay = jnp.exp(m_prev - m_last)
    m_ref[...] = b_t[-1] + m_last
    dec_scr[...] = decay

    lhs_q[...] = jnp.zeros(lhs_q.shape, _F32)
    lhs_k[...] = jnp.zeros(lhs_k.shape, _F32)
    lhs_wv[...] = jnp.zeros(lhs_wv.shape, _F32)
    n_all = n0_ref[0]
    den_part = {}
    for h in range(N_HEADS):
        def col(x):
            return x[:, h:h + 1]
        mb = h * HEAD_COLS
        q = [mid_ref[t, :, mb + _M_Q:mb + _M_Q + DK] for t in range(seq_len)]
        k = [mid_ref[t, :, mb + _M_K:mb + _M_K + DK] for t in range(seq_len)]
        v = [mid_ref[t, :, mb + _M_V:mb + _M_V + DV] for t in range(seq_len)]
        n_h = n_all[:, h * DK:(h + 1) * DK]
        n_new = col(decay) * n_h
        for t in range(seq_len):
            qs = q[t] * col(inter[t])
            qs_scr[t, :, h * DK:(h + 1) * DK] = qs
            kk_scr[t, :, h * DK:(h + 1) * DK] = k[t]
            wv_scr[t, :, h * DV:(h + 1) * DV] = col(ws[t]) * v[t]
            n_new = n_new + col(ws[t]) * k[t]
            den = jnp.sum(qs * n_h, axis=-1, keepdims=True)
            num = None
            for s in range(t + 1):
                w = jnp.sum(q[t] * k[s], axis=-1, keepdims=True) * jnp.exp(col(g_t[s]) - col(m_run[t]))
                num = w * v[s] if num is None else num + w * v[s]
                den = den + w
            mg_ref[t, :, h * DV:(h + 1) * DV] = num
            den_part[h, t] = den
        n_ref[:, h * DK:(h + 1) * DK] = n_new

    n_par = lhs_q.shape[0] // 8

    def body(ii, carry):
        for u in range(n_par):
            i = ii * n_par + u
            rows = slice(8 * u, 8 * u + 8)
            for t in range(seq_len):
                lhs_q[8 * u + t:8 * u + t + 1, :] = qs_scr[t, pl.ds(i, 1), :]
                lhs_k[8 * u + t:8 * u + t + 1, :] = kk_scr[t, pl.ds(i, 1), :]
                lhs_wv[8 * u + t:8 * u + t + 1, :] = wv_scr[t, pl.ds(i, 1), :]
            for h in range(N_HEADS):
                c_old = c0_ref[0, i, h]
                r = _dot(lhs_q[rows, h * DK:(h + 1) * DK].astype(_BF), c_old.astype(_BF))
                for t in range(seq_len):
                    rs_scr[t, pl.ds(i, 1), h * DV:(h + 1) * DV] = r[t:t + 1, :]
                d_c = lax.dot_general(lhs_k[rows, h * DK:(h + 1) * DK].astype(_BF),
                                      lhs_wv[rows, h * DV:(h + 1) * DV].astype(_BF),
                                      (((0,), (0,)), ((), ())), preferred_element_type=_F32)
                c_ref[0, i, h] = dec_scr[pl.ds(i, 1), h:h + 1] * c_old + d_c
        return carry

    lax.fori_loop(0, bb // n_par, body, 0)

    for h in range(N_HEADS):
        mb = h * HEAD_COLS
        for t in range(seq_len):
            num = mg_ref[t, :, h * DV:(h + 1) * DV] + rs_scr[t, :, h * DV:(h + 1) * DV]
            hraw = num * (1.0 / jnp.maximum(jnp.abs(den_part[h, t]), em[t][:, h:h + 1]))
            ms = jnp.mean(hraw * hraw, axis=-1, keepdims=True)
            hn = hraw * lax.rsqrt(ms + EPS) * hng[:, h * DV:(h + 1) * DV]
            ya = mid_ref[t, :, mb + _M_YA:mb + _M_YA + DV]
            og = mid_ref[t, :, mb + _M_OG:mb + _M_OG + DV]
            mg_ref[t, :, h * DV:(h + 1) * DV] = ya + og * hn


def _mlstm_sample_call(layer, mid3, gates3, bias, hng, c0, n0, m0, c_buf, *, bb):
    seq_len, n_seq, _ = mid3.shape
    aliases = {} if c_buf is None else {7: 1}
    if c_buf is None:
        c_buf = jnp.zeros((1,), _F32)
    return pl.pallas_call(
        _mlstm_sample_kernel,
        grid=(n_seq // bb,),
        in_specs=[
            pl.BlockSpec((seq_len, bb, N_HEADS * HEAD_COLS), lambda i: (0, i, 0)),
            pl.BlockSpec((seq_len, bb, GATE_COLS), lambda i: (0, i, 0)),
            pl.BlockSpec((1, 1, GATE_COLS), lambda i: (layer, 0, 0)),
            pl.BlockSpec((1, 1, D_MODEL), lambda i: (layer, 0, 0)),
            pl.BlockSpec((1, bb, N_HEADS, DK, DV), lambda i: (layer, i, 0, 0, 0)),
            pl.BlockSpec((1, bb, N_HEADS * DK), lambda i: (layer, i, 0)),
            pl.BlockSpec((1, bb, CHUNK), lambda i: (layer, i, 0)),
            pl.BlockSpec(memory_space=pl.ANY),
        ],
        out_specs=[
            pl.BlockSpec((seq_len, bb, D_MODEL), lambda i: (0, i, 0)),
            pl.BlockSpec((1, bb, N_HEADS, DK, DV), lambda i: (layer, i, 0, 0, 0)),
            pl.BlockSpec((bb, N_HEADS * DK), lambda i: (i, 0)),
            pl.BlockSpec((bb, CHUNK), lambda i: (i, 0)),
        ],
        out_shape=[
            jax.ShapeDtypeStruct((seq_len, n_seq, D_MODEL), _F32),
            jax.ShapeDtypeStruct((DEPTH, n_seq, N_HEADS, DK, DV), _F32),
            jax.ShapeDtypeStruct((n_seq, N_HEADS * DK), _F32),
            jax.ShapeDtypeStruct((n_seq, CHUNK), _F32),
        ],
        scratch_shapes=[
            pltpu.VMEM((seq_len, bb, N_HEADS * DK), _F32),
            pltpu.VMEM((seq_len, bb, N_HEADS * DK), _F32),
            pltpu.VMEM((seq_len, bb, N_HEADS * DV), _F32),
            pltpu.VMEM((seq_len, bb, N_HEADS * DV), _F32),
            pltpu.VMEM((bb, CHUNK), _F32),
            pltpu.VMEM((32, N_HEADS * DK), _F32),
            pltpu.VMEM((32, N_HEADS * DK), _F32),
            pltpu.VMEM((32, N_HEADS * DV), _F32),
        ],
        input_output_aliases=aliases,
        compiler_params=_params(1),
        name="mlstm_sample",
    )(mid3, gates3, bias, hng, c0, n0, m0, c_buf)


def _mlp_kernel(x_ref, mg_ref, gt1_ref, sc2_ref, sh2_ref, gt2_ref, g2_ref, wo_ref, wu_ref, wd_ref, gf_ref,
                o_ref, xmid, h2, acc, *, mod_reps, final_norm):
    f = pl.program_id(1)

    def mod(ref):
        return ref[0, 0, 0] if mod_reps is None else _rep_rows(ref[0, 0], mod_reps)

    @pl.when(f == 0)
    def _():
        mix = _dot(mg_ref[...].astype(_BF), wo_ref[0])
        xm = x_ref[...] + mod(gt1_ref) * mix
        xmid[...] = xm
        h2[...] = _rms_mod(xm, g2_ref[0], mod(sc2_ref), mod(sh2_ref)).astype(_BF)
        acc[...] = jnp.zeros(acc.shape, _F32)

    a = jnp.maximum(_dot(h2[...], wu_ref[0]), 0.0)
    acc[...] += _dot((a * a).astype(_BF), wd_ref[0])

    @pl.when(f == pl.num_programs(1) - 1)
    def _():
        y = xmid[...] + mod(gt2_ref) * acc[...]
        if final_norm:
            ms = jnp.mean(y * y, axis=-1, keepdims=True)
            y = y * lax.rsqrt(ms + EPS) * gf_ref[...]
        o_ref[...] = y


def _mlp_call(layer, x, merged, mod, row0, g2, w_out, w_up, w_down, g_final, *, tm, tf, seq_len, final_norm):
    m_rows = x.shape[0]
    pieces = (_GT1, _SC2, _SH2, _GT2)
    if seq_len >= tm:
        tiles_per_seq = seq_len // tm
        mod_reps = None
        mod_specs = [_mod_row_spec(layer, p, row0, lambda i, f: i // tiles_per_seq) for p in pieces]
    else:
        mod_reps = seq_len
        mod_specs = [pl.BlockSpec((1, 1, tm // seq_len, D_MODEL), functools.partial(lambda p, i, f: (layer, p, 0, 0), p))
                     for p in pieces]
    kern = functools.partial(_mlp_kernel, mod_reps=mod_reps, final_norm=final_norm)
    return pl.pallas_call(
        kern,
        grid=(m_rows // tm, D_FF // tf),
        in_specs=[
            pl.BlockSpec((tm, D_MODEL), lambda i, f: (i, 0)),
            pl.BlockSpec((tm, D_MODEL), lambda i, f: (i, 0)),
            *mod_specs,
            pl.BlockSpec((1, 1, D_MODEL), lambda i, f: (layer, 0, 0)),
            pl.BlockSpec((1, D_MODEL, D_MODEL), lambda i, f: (layer, 0, 0)),
            pl.BlockSpec((1, D_MODEL, tf), lambda i, f: (layer, 0, f)),
            pl.BlockSpec((1, tf, D_MODEL), lambda i, f: (layer, f, 0)),
            pl.BlockSpec((1, D_MODEL), lambda i, f: (0, 0)),
        ],
        out_specs=pl.BlockSpec((tm, D_MODEL), lambda i, f: (i, 0)),
        out_shape=jax.ShapeDtypeStruct((m_rows, D_MODEL), _F32),
        scratch_shapes=[
            pltpu.VMEM((tm, D_MODEL), _F32),
            pltpu.VMEM((tm, D_MODEL), _BF),
            pltpu.VMEM((tm, D_MODEL), _F32),
        ],
        compiler_params=_params(2),
        name="outproj_mlp",
    )(x, merged, mod, mod, mod, mod, g2, w_out, w_up, w_down, g_final)


def kernel(x_prompt, x_sample, state_conv, state_C, state_n, state_m, c_prompt, c_sample,
           w_ada, b_ada, g_norm1, g_norm2, w_in, b_gate, conv_w, conv_b, hn_g, w_out, w_up, w_down, g_final):
    n_p, seq_p, _ = x_prompt.shape
    n_s, seq_s, _ = x_sample.shape
    rows_s = n_s * seq_s
    tm_p = 1024

    w_t = jnp.swapaxes(w_in, 1, 2).astype(_BF)
    w_tail = w_t[:, _OFF_GA:]
    zrows = jnp.zeros((DEPTH, GATE_F_COL - N_HEADS, D_MODEL), _BF)
    w_gate = jnp.concatenate([w_t[:, _OFF_IG:_OFF_IG + N_HEADS], zrows,
                              w_t[:, _OFF_IG + N_HEADS:_OFF_GA], zrows], axis=1)
    w_out_b = w_out.astype(_BF)
    w_up_b = w_up.astype(_BF)
    w_down_b = w_down.astype(_BF)
    zb = jnp.zeros((DEPTH, 4), _F32)
    bgate = jnp.broadcast_to(
        jnp.concatenate([b_gate[:, :N_HEADS], zb, b_gate[:, N_HEADS:], zb], axis=-1)[:, :, None],
        (DEPTH, BIAS_ROWS, CHUNK))
    zb = jnp.zeros((DEPTH, GATE_F_COL - N_HEADS), _F32)
    bias_row = jnp.concatenate([b_gate[:, :N_HEADS], zb, b_gate[:, N_HEADS:], zb], axis=-1)[:, None, :]
    g1 = g_norm1.reshape(DEPTH, 1, D_MODEL)
    g2 = g_norm2.reshape(DEPTH, 1, D_MODEL)
    hng = hn_g.reshape(DEPTH, 1, D_MODEL)
    cb = conv_b.reshape(DEPTH, 1, D_MODEL)
    gfin = g_final.reshape(1, D_MODEL)

    mod = _ada_call(jnp.concatenate([c_sample, c_prompt], axis=0), w_ada, b_ada)
    mod_rows = mod.reshape(DEPTH, N_MOD, n_s + n_p, 1, D_MODEL)

    xp = x_prompt.reshape(n_p * seq_p, D_MODEL)
    xs = x_sample.transpose(1, 0, 2).reshape(rows_s, D_MODEL)
    conv_s_in = state_conv.transpose(0, 2, 1, 3)
    n_s_in = state_n.reshape(DEPTH, n_s, N_HEADS * DK)
    m_s_in = jnp.pad(state_m, ((0, 0), (0, 0), (0, CHUNK - N_HEADS)))

    zeros_conv = jnp.zeros((n_p, CONV_W - 1, D_MODEL), _F32)
    zeros_c = jnp.zeros((n_p, N_HEADS, DK, DV), _F32)
    zeros_n = jnp.zeros((n_p, N_HEADS, DK), _F32)
    zeros_m = jnp.zeros((n_p, 8, CHUNK), _F32)

    p_conv, p_n, p_m, s_conv, s_n, s_m = [], [], [], [], [], []
    p_c = s_c = None
    for l in range(DEPTH):
        final = l == DEPTH - 1

        mid, gates_t, nconv = _inproj_prompt_call(
            l, xp, mod_rows, n_s, g1, w_t, w_tail, w_gate, bgate, conv_w, cb, zeros_conv, seq_len=seq_p, tm=tm_p)
        merged, p_c, n1, m1 = _mlstm_prompt_call(
            l, mid, gates_t, hng, zeros_c, zeros_n, zeros_m, p_c, seq_len=seq_p, tb=1024)
        xp = _mlp_call(l, xp, merged, mod_rows, n_s, g2, w_out_b, w_up_b, w_down_b, gfin,
                       tm=512, tf=2048, seq_len=seq_p, final_norm=final)
        tiles_per_seq = seq_p // tm_p
        p_conv.append(nconv[tiles_per_seq - 1::tiles_per_seq]); p_n.append(n1); p_m.append(m1[:, :N_HEADS, 0])

        mid, gates, nconv = _inproj_sample_call(
            l, xs, mod, g1, w_t, w_tail, w_gate, conv_w, cb, conv_s_in, seq_len=seq_s)
        merged3, s_c, n1, m1 = _mlstm_sample_call(
            l, mid.reshape(seq_s, n_s, N_HEADS * HEAD_COLS), gates.reshape(seq_s, n_s, GATE_COLS), bias_row, hng,
            state_C, n_s_in, m_s_in, s_c, bb=16)
        xs = _mlp_call(l, xs, merged3.reshape(rows_s, D_MODEL), mod, 0, g2, w_out_b, w_up_b, w_down_b, gfin,
                       tm=rows_s, tf=1024, seq_len=seq_s, final_norm=final)
        s_conv.append(nconv.transpose(1, 0, 2))
        s_n.append(n1.reshape(n_s, N_HEADS, DK)); s_m.append(m1[:, :N_HEADS])

    return (xp.reshape(n_p, seq_p, D_MODEL), xs.reshape(seq_s, n_s, D_MODEL).transpose(1, 0, 2),
            jnp.stack(p_conv), p_c, jnp.stack(p_n), jnp.stack(p_m),
            jnp.stack(s_conv), s_c, jnp.stack(s_n), jnp.stack(s_m))
```

```python
import functools

import jax
import jax.numpy as jnp
from jax import lax
from jax.experimental import pallas as pl
from jax.experimental.pallas import tpu as pltpu

D_MODEL = 1024
N_HEADS = 4
DK = 128
DV = 256
D_FF = 4096
DEPTH = 4
CONV_W = 3
GATE_CAP = 15.0
EPS = 1e-6
CHUNK = 128
HEAD_COLS = 1024
GATE_COLS = 256
GATE_F_COL = 128
BIAS_ROWS = 16
GATE_ROWS = 24
EXT_COLS = DV + CHUNK
VMEM_LIMIT = 56 * 1024 * 1024

_OFF_XC, _OFF_BG, _OFF_CG = 0, 1024, 2048
_OFF_Q, _OFF_K, _OFF_V, _OFF_O = 3072, 3584, 4096, 5120
_OFF_IG, _OFF_GA, _OFF_GB = 6144, 6152, 7176

_M_YA, _M_Q, _M_K, _M_V, _M_OG = 0, 256, 384, 512, 768

_BF = jnp.bfloat16
_F32 = jnp.float32


def _dot(a, b):
    return jnp.dot(a, b, preferred_element_type=_F32)


def _dot_nt(a, b):
    return lax.dot_general(a, b, (((1,), (1,)), ((), ())), preferred_element_type=_F32)


def _sigmoid(x):
    return 0.5 * jnp.tanh(0.5 * x) + 0.5


def _rms_mod(x, g, sc, sh):
    ms = jnp.mean(x * x, axis=-1, keepdims=True)
    return (x * lax.rsqrt(ms + EPS)) * (g * (1.0 + sc)) + sh


def _rep_rows(v, reps):
    return v if reps == 1 else jnp.concatenate([v] * reps, axis=0)


def _params(n_axes):
    return pltpu.CompilerParams(dimension_semantics=("arbitrary",) * n_axes, vmem_limit_bytes=VMEM_LIMIT)


N_MOD = 6
_SH1, _SC1, _GT1, _SH2, _SC2, _GT2 = range(N_MOD)


def _ada_kernel(c_ref, w_ref, b_ref, o_ref):
    c = c_ref[...]
    a = (c * _sigmoid(c)).astype(_BF)
    o_ref[0, 0] = _dot(a, w_ref[0].astype(_BF)) + b_ref[0]


def _ada_call(c_all, w_ada, b_ada):
    n_rows = c_all.shape[0]
    return pl.pallas_call(
        _ada_kernel,
        grid=(DEPTH, N_MOD),
        in_specs=[
            pl.BlockSpec((n_rows, D_MODEL), lambda l, j: (0, 0)),
            pl.BlockSpec((1, D_MODEL, D_MODEL), lambda l, j: (l, 0, j)),
            pl.BlockSpec((1, 1, D_MODEL), lambda l, j: (l, 0, j)),
        ],
        out_specs=pl.BlockSpec((1, 1, n_rows, D_MODEL), lambda l, j: (l, j, 0, 0)),
        out_shape=jax.ShapeDtypeStruct((DEPTH, N_MOD, n_rows, D_MODEL), _F32),
        compiler_params=_params(2),
        name="ada_mod",
    )(c_all, w_ada, b_ada.reshape(DEPTH, 1, N_MOD * D_MODEL))


def _w_in_specs(layer, idx):
    def spec(width, offset):
        return pl.BlockSpec((1, width, D_MODEL), lambda *g: (layer, offset // width + idx(*g), 0))
    return [spec(256, _OFF_XC), spec(256, _OFF_BG), spec(256, _OFF_CG), spec(128, _OFF_Q), spec(128, _OFF_K),
            spec(256, _OFF_V), spec(256, _OFF_O), spec(256, 0), spec(256, _OFF_GB - _OFF_GA)]


def _scan_lanes(x, combine, identity, seg_len):
    pos = lax.broadcasted_iota(jnp.int32, x.shape, 1) % seg_len
    shift = 1
    while shift < seg_len:
        x = combine(x, jnp.where(pos >= shift, pltpu.roll(x, shift, 1), identity))
        shift *= 2
    return x


def _softcap(a):
    return GATE_CAP * jnp.tanh(a / GATE_CAP)


def _log_sigmoid(x):
    return -(jnp.maximum(-x, 0.0) + jnp.log(1.0 + jnp.exp(-jnp.abs(x))))


def _gate_rows_prompt(gates, bias, gt_ref, first_chunk):
    n_chunks = gates.shape[0] // CHUNK
    gates_t = gates.T

    def stack(r0):
        return jnp.concatenate([gates_t[r0:r0 + 8, c * CHUNK:(c + 1) * CHUNK] for c in range(n_chunks)], axis=0)

    i_pre = _softcap(stack(0) + jnp.concatenate([bias[0:8]] * n_chunks, axis=0))
    f_pre = _softcap(stack(GATE_F_COL) + jnp.concatenate([bias[8:16]] * n_chunks, axis=0))
    b = _scan_lanes(_log_sigmoid(f_pre), jnp.add, 0.0, CHUNK)
    g = i_pre - b
    cm = _scan_lanes(g, jnp.maximum, -3e38, CHUNK)
    for c in range(n_chunks):
        gt_ref[first_chunk + c, 0:8] = b[c * 8:(c + 1) * 8]
        gt_ref[first_chunk + c, 8:16] = g[c * 8:(c + 1) * 8]
        gt_ref[first_chunk + c, 16:24] = cm[c * 8:(c + 1) * 8]


def _inproj_project(h, w_refs):
    wxc, wbg, wcg, wq, wk, wv, wo, wga, wgb = w_refs
    xc = _dot_nt(h, wxc[0])
    cg = _dot_nt(h, wcg[0])
    qk = _dot_nt(h, jnp.concatenate([wq[0], wk[0]], axis=0))
    return dict(u=cg * xc, bg=_dot_nt(h, wbg[0]), ga=_dot_nt(h, wga[0]), qk=qk,
                v=_dot_nt(h, wv[0]), o=_dot_nt(h, wo[0]), gb=_dot_nt(h, wgb[0]))


def _inproj_epilogue(p, conv, mid_ref, out_dtype, r0=0):
    rows = slice(r0, r0 + conv.shape[0])
    mid_ref[rows, _M_YA:_M_YA + 256] = (_sigmoid(p["ga"]) * (p["bg"] * conv)).astype(out_dtype)
    mid_ref[rows, _M_Q:_M_Q + DK] = p["qk"][:, :DK].astype(out_dtype)
    mid_ref[rows, _M_K:_M_K + DK] = (p["qk"][:, DK:] * (DK ** -0.5)).astype(out_dtype)
    mid_ref[rows, _M_V:_M_V + 256] = p["v"].astype(out_dtype)
    mid_ref[rows, _M_OG:_M_OG + 256] = (_sigmoid(p["o"]) * _sigmoid(p["gb"])).astype(out_dtype)


_RAW_COLS = 8 * 256


def _inproj_prompt_kernel(x_ref, sc_ref, sh_ref, g_ref, wxc, wbg, wcg, wq, wk, wv, wo, wga, wgb, wg_ref,
                          bg_ref, cw_ref, cb_ref, cprev_ref, mid_ref, gt_ref, nconv_ref,
                          h_scr, gate_scr, raw_even, raw_odd, ubuf, carry, *, tiles_per_seq, n_work):
    j = pl.program_id(0)
    tm = x_ref.shape[0]
    cur = jnp.minimum(j, n_work - 1)
    s = cur % N_HEADS
    prev = jnp.maximum(j - 1, 0)
    tile_p = prev // N_HEADS
    s_p = prev % N_HEADS
    chunks_per_step = (tm // CHUNK) // N_HEADS
    raw = (raw_even, raw_odd)

    @pl.when(j == 0)
    def _():
        raw_odd[...] = jnp.zeros(raw_odd.shape, _F32)
        carry[...] = jnp.zeros(carry.shape, _F32)

    @pl.when(jnp.logical_and(s == 0, j < n_work))
    def _():
        h = _rms_mod(x_ref[...], g_ref[0], sc_ref[0, 0, 0], sh_ref[0, 0, 0]).astype(_BF)
        h_scr[...] = h
        gate_scr[...] = _dot_nt(h, wg_ref[0])

    def step(slot):
        rows0 = pl.multiple_of(s * (chunks_per_step * CHUNK), CHUNK)
        _gate_rows_prompt(gate_scr[pl.ds(rows0, chunks_per_step * CHUNK), :], bg_ref[0], gt_ref,
                          s * chunks_per_step)

        h = h_scr[...]
        weights = (wxc[0], wcg[0], wbg[0], wga[0], jnp.concatenate([wq[0], wk[0]], axis=0),
                   wv[0], wo[0], wgb[0])
        seq_start = (tile_p % tiles_per_seq) == 0
        ubuf[6:8, :] = jnp.where(seq_start, cprev_ref[0], carry[s_p])
        cw = cw_ref[0]
        conv_bias = cb_ref[0]
        rows_per_part = tm // len(weights)
        rc = min(CHUNK // 2, rows_per_part)
        for idx, w in enumerate(weights):
            raw[slot][:, idx * 256:(idx + 1) * 256] = _dot_nt(h, w)
            for r0 in range(idx * rows_per_part, (idx + 1) * rows_per_part, rc):
                def piece(k):
                    return raw[1 - slot][r0:r0 + rc, k * 256:(k + 1) * 256]

                u = piece(1) * piece(0)
                ubuf[8 + r0:8 + r0 + rc, :] = u
                conv = (conv_bias + cw[0:1] * ubuf[6 + r0:6 + r0 + rc, :]
                        + cw[1:2] * ubuf[7 + r0:7 + r0 + rc, :] + cw[2:3] * u)
                _inproj_epilogue(dict(bg=piece(2), ga=piece(3), qk=piece(4), v=piece(5), o=piece(6),
                                      gb=piece(7)), conv, mid_ref, _BF, r0)
        last2 = ubuf[6 + tm:8 + tm, :]
        carry[s_p] = last2
        nconv_ref[0] = last2

    for parity in range(2):
        pl.when(j % 2 == parity)(functools.partial(step, parity))


def _mod_row_spec(layer, piece, row0, seq_of):
    return pl.BlockSpec((1, 1, 1, 1, D_MODEL), lambda *g: (layer, piece, row0 + seq_of(*g), 0, 0))


def _inproj_prompt_call(layer, x, mod_rows, row0, g, w_t, w_tail, w_gate, bgate, conv_w, conv_b, conv_prev,
                        *, seq_len, tm):
    m_rows = x.shape[0]
    n_tiles = m_rows // tm
    tiles_per_seq = seq_len // tm
    n_work = n_tiles * N_HEADS

    def cur(j):
        return jnp.minimum(j, n_work - 1)

    def prev(j):
        return jnp.maximum(j - 1, 0)

    kern = functools.partial(_inproj_prompt_kernel, tiles_per_seq=tiles_per_seq, n_work=n_work)
    w_specs = _w_in_specs(layer, lambda j: cur(j) % N_HEADS)
    seq_of_cur = lambda j: cur(j) // N_HEADS // tiles_per_seq
    return pl.pallas_call(
        kern,
        grid=(n_work + 1,),
        in_specs=[
            pl.BlockSpec((tm, D_MODEL), lambda j: (cur(j) // N_HEADS, 0)),
            _mod_row_spec(layer, _SC1, row0, seq_of_cur),
            _mod_row_spec(layer, _SH1, row0, seq_of_cur),
            pl.BlockSpec((1, 1, D_MODEL), lambda j: (layer, 0, 0)),
            *w_specs,
            pl.BlockSpec((1, GATE_COLS, D_MODEL), lambda j: (layer, 0, 0)),
            pl.BlockSpec((1, BIAS_ROWS, CHUNK), lambda j: (layer, 0, 0)),
            pl.BlockSpec((1, CONV_W, 256), lambda j: (layer, 0, prev(j) % N_HEADS)),
            pl.BlockSpec((1, 1, 256), lambda j: (layer, 0, prev(j) % N_HEADS)),
            pl.BlockSpec((1, CONV_W - 1, 256),
                         lambda j: (prev(j) // N_HEADS // tiles_per_seq, 0, prev(j) % N_HEADS)),
        ],
        out_specs=[
            pl.BlockSpec((tm, HEAD_COLS), lambda j: (prev(j) // N_HEADS, prev(j) % N_HEADS)),
            pl.BlockSpec((tm // CHUNK, GATE_ROWS, CHUNK), lambda j: (cur(j) // N_HEADS, 0, 0)),
            pl.BlockSpec((1, CONV_W - 1, 256), lambda j: (prev(j) // N_HEADS, 0, prev(j) % N_HEADS)),
        ],
        out_shape=[
            jax.ShapeDtypeStruct((m_rows, N_HEADS * HEAD_COLS), _BF),
            jax.ShapeDtypeStruct((m_rows // CHUNK, GATE_ROWS, CHUNK), _F32),
            jax.ShapeDtypeStruct((n_tiles, CONV_W - 1, D_MODEL), _F32),
        ],
        scratch_shapes=[
            pltpu.VMEM((tm, D_MODEL), _BF),
            pltpu.VMEM((tm, GATE_COLS), _F32),
            pltpu.VMEM((tm, _RAW_COLS), _F32),
            pltpu.VMEM((tm, _RAW_COLS), _F32),
            pltpu.VMEM((tm + 8, 256), _F32),
            pltpu.VMEM((N_HEADS, CONV_W - 1, 256), _F32),
        ],
        compiler_params=_params(1),
        name="inproj_prompt",
    )(x, mod_rows, mod_rows, g, *([w_t] * 7), *([w_tail] * 2), w_gate, bgate, conv_w, conv_b, conv_prev)


def _inproj_sample_kernel(x_ref, sc_ref, sh_ref, g_ref, wxc, wbg, wcg, wq, wk, wv, wo, wga, wgb, wg_ref,
                          cw_ref, cb_ref, cprev_ref, mid_ref, gt_ref, nconv_ref, h_scr, *, seq_len):
    s = pl.program_id(0)
    n_b = sc_ref.shape[2]

    @pl.when(s == 0)
    def _():
        h = _rms_mod(x_ref[...], g_ref[0], _rep_rows(sc_ref[0, 0], seq_len), _rep_rows(sh_ref[0, 0], seq_len))
        h = h.astype(_BF)
        h_scr[...] = h
        gt_ref[...] = _dot_nt(h, wg_ref[0])

    p = _inproj_project(h_scr[...], (wxc, wbg, wcg, wq, wk, wv, wo, wga, wgb))
    u = p["u"]
    prev0 = cprev_ref[0, 0]
    prev1 = cprev_ref[0, 1]
    p1 = jnp.concatenate([prev1, u[0:(seq_len - 1) * n_b]], axis=0)
    p2 = jnp.concatenate([prev0, prev1, u[0:(seq_len - 2) * n_b]], axis=0)
    cw = cw_ref[0]
    conv = cb_ref[0] + cw[0:1] * p2 + cw[1:2] * p1 + cw[2:3] * u
    nconv_ref[0] = u[(seq_len - 2) * n_b:(seq_len - 1) * n_b]
    nconv_ref[1] = u[(seq_len - 1) * n_b:seq_len * n_b]
    _inproj_epilogue(p, conv, mid_ref, _F32)


def _inproj_sample_call(layer, x, mod, g, w_t, w_tail, w_gate, conv_w, conv_b, conv_prev, *, seq_len):
    tm = x.shape[0]
    n_b = tm // seq_len
    kern = functools.partial(_inproj_sample_kernel, seq_len=seq_len)
    w_specs = _w_in_specs(layer, lambda s: s)
    return pl.pallas_call(
        kern,
        grid=(N_HEADS,),
        in_specs=[
            pl.BlockSpec((tm, D_MODEL), lambda s: (0, 0)),
            pl.BlockSpec((1, 1, n_b, D_MODEL), lambda s: (layer, _SC1, 0, 0)),
            pl.BlockSpec((1, 1, n_b, D_MODEL), lambda s: (layer, _SH1, 0, 0)),
            pl.BlockSpec((1, 1, D_MODEL), lambda s: (layer, 0, 0)),
            *w_specs,
            pl.BlockSpec((1, GATE_COLS, D_MODEL), lambda s: (layer, 0, 0)),
            pl.BlockSpec((1, CONV_W, 256), lambda s: (layer, 0, s)),
            pl.BlockSpec((1, 1, 256), lambda s: (layer, 0, s)),
            pl.BlockSpec((1, CONV_W - 1, n_b, 256), lambda s: (layer, 0, 0, s)),
        ],
        out_specs=[
            pl.BlockSpec((tm, HEAD_COLS), lambda s: (0, s)),
            pl.BlockSpec((tm, GATE_COLS), lambda s: (0, 0)),
            pl.BlockSpec((CONV_W - 1, n_b, 256), lambda s: (0, 0, s)),
        ],
        out_shape=[
            jax.ShapeDtypeStruct((tm, N_HEADS * HEAD_COLS), _F32),
            jax.ShapeDtypeStruct((tm, GATE_COLS), _F32),
            jax.ShapeDtypeStruct((CONV_W - 1, n_b, D_MODEL), _F32),
        ],
        scratch_shapes=[pltpu.VMEM((tm, D_MODEL), _BF)],
        compiler_params=_params(1),
        name="inproj_sample",
    )(x, mod, mod, g, *([w_t] * 7), *([w_tail] * 2), w_gate, conv_w, conv_b, conv_prev)


def _causal_mask(lq):
    row_id = lax.broadcasted_iota(jnp.int32, (lq, CHUNK), 0)
    col_id = lax.broadcasted_iota(jnp.int32, (lq, CHUNK), 1)
    return col_id <= row_id


def _chunk_local(a, causal):
    dmat = jnp.where(causal, jnp.exp(a["g_row"] - a["m_col"]), 0.0)
    s_bf = (_dot_nt(a["q"], a["k"]) * dmat).astype(_BF)
    kw = a["k"].astype(_F32) * a["ws_col"]
    return s_bf, _dot(kw.T.astype(_BF), a["v_ext"])


def _chunk_output(a, s_bf, d_state, cx_ref, h, hng_h):
    q_inter = (a["q"].astype(_F32) * a["inter_col"]).astype(_BF)
    cx = cx_ref[h]
    rhs = jnp.concatenate([a["v_ext"], cx.astype(_BF)], axis=0)
    res = _dot(jnp.concatenate([s_bf, q_inter], axis=1), rhs)
    cx_ref[h] = a["dec"] * cx + d_state
    num = res[:, :DV]
    rden = 1.0 / jnp.maximum(jnp.abs(res[:, DV:]), a["em_col"])
    sq = jnp.sum(num * num, axis=-1, keepdims=True) * (1.0 / DV)
    scale = rden * lax.rsqrt(rden * rden * sq + EPS)
    hn = num * jnp.concatenate([scale, scale], axis=1) * hng_h
    return a["ya"] + a["og"] * hn


def _mlstm_prompt_kernel(mid_ref, gt_ref, hng_ref, c0_ref, n0_ref, m0_ref, cbuf_ref,
                         mg_ref, c_ref, n_ref, m_ref, cx_ref, cols_scr, rows_scr):
    del cbuf_ref
    t = pl.program_id(1)

    @pl.when(t == 0)
    def _():
        m_ref[...] = m0_ref[...]
        for h in range(N_HEADS):
            cx_ref[h, :, 0:DV] = c0_ref[0, h]
            cx_ref[h, :, DV:EXT_COLS] = jnp.broadcast_to(n0_ref[0, h:h + 1, :], (CHUNK, DK)).T

    n_chunks = mid_ref.shape[0] // CHUNK
    rows = n_chunks * 8
    hng = hng_ref[0]
    gates = gt_ref[...]
    b = gates[:, 0:8, :].reshape(rows, CHUNK)
    g = gates[:, 8:16, :].reshape(rows, CHUNK)
    cm = gates[:, 16:24, :].reshape(rows, CHUNK)
    b_last = jnp.broadcast_to(b[:, CHUNK - 1:CHUNK], (rows, CHUNK))
    cm_last = jnp.broadcast_to(cm[:, CHUNK - 1:CHUNK], (rows, CHUNK))

    m_prev = m_ref[0]
    m_prevs = []
    for c in range(n_chunks):
        m_prevs.append(m_prev)
        m_prev = b_last[c * 8:(c + 1) * 8] + jnp.maximum(cm_last[c * 8:(c + 1) * 8], m_prev)
    m_ref[0] = m_prev
    m_prev_all = jnp.concatenate(m_prevs, axis=0)

    m_run = jnp.maximum(cm, m_prev_all)
    inter = jnp.exp(m_prev_all - m_run)
    em = jnp.exp(-(b + m_run))
    m_last = jnp.maximum(cm_last, m_prev_all)
    ws = jnp.exp(g - m_last)
    decay = jnp.exp(m_prev_all - m_last)

    pad = jnp.zeros((CHUNK - 32, CHUNK), _F32)
    for c in range(n_chunks):
        sl = slice(c * 8, (c + 1) * 8)
        cols_scr[c] = jnp.concatenate([m_run[sl], inter[sl], em[sl], ws[sl], pad], axis=0).T
        rows_scr[c, 0:8] = g[sl]
        rows_scr[c, 8:16] = decay[sl]

    causal = _causal_mask(CHUNK)
    ones = jnp.ones((CHUNK, CHUNK), _BF)

    def chunk_body(c, carry):
        cols = cols_scr[c]
        vec = rows_scr[c]
        rows_c = pl.ds(pl.multiple_of(c * CHUNK, CHUNK), CHUNK)
        heads = []
        for h in range(N_HEADS):
            mb = h * HEAD_COLS
            v = mid_ref[rows_c, mb + _M_V:mb + _M_V + 256]
            heads.append(dict(
                q=mid_ref[rows_c, mb + _M_Q:mb + _M_Q + 128],
                k=mid_ref[rows_c, mb + _M_K:mb + _M_K + 128],
                v_ext=jnp.concatenate([v, ones], axis=1),
                ya=mid_ref[rows_c, mb + _M_YA:mb + _M_YA + 256].astype(_F32),
                og=mid_ref[rows_c, mb + _M_OG:mb + _M_OG + 256].astype(_F32),
                g_row=vec[h:h + 1, :],
                m_col=cols[:, h:h + 1],
                inter_col=cols[:, 8 + h:9 + h],
                em_col=cols[:, 16 + h:17 + h],
                ws_col=cols[:, 24 + h:25 + h],
                dec=vec[8 + h:9 + h, 0:1]))
        outs = []
        for h, a in enumerate(heads):
            s_bf, d_state = _chunk_local(a, causal)
            outs.append(_chunk_output(a, s_bf, d_state, cx_ref, h, hng[:, h * DV:(h + 1) * DV]))
        mg_ref[rows_c, :] = jnp.concatenate(outs, axis=1).astype(_BF)
        return carry

    lax.fori_loop(0, n_chunks, chunk_body, 0)

    @pl.when(t == pl.num_programs(1) - 1)
    def _():
        for h in range(N_HEADS):
            c_ref[0, 0, h] = cx_ref[h, :, 0:DV]
            n_ref[0, h:h + 1, :] = cx_ref[h, :, DV:EXT_COLS].T[0:1, :]


def _mlstm_prompt_call(layer, mid, gates_t, hng, c0, n0, m0, c_buf, *, seq_len, tb):
    m_rows = mid.shape[0]
    n_seq = m_rows // seq_len
    steps = seq_len // tb
    aliases = {} if c_buf is None else {6: 1}
    if c_buf is None:
        c_buf = jnp.zeros((1,), _F32)
    return pl.pallas_call(
        _mlstm_prompt_kernel,
        grid=(n_seq, steps),
        in_specs=[
            pl.BlockSpec((tb, N_HEADS * HEAD_COLS), lambda b, t: (b * steps + t, 0)),
            pl.BlockSpec((tb // CHUNK, GATE_ROWS, CHUNK), lambda b, t: (b * steps + t, 0, 0)),
            pl.BlockSpec((1, 1, D_MODEL), lambda b, t: (layer, 0, 0)),
            pl.BlockSpec((1, N_HEADS, DK, DV), lambda b, t: (b, 0, 0, 0)),
            pl.BlockSpec((1, N_HEADS, DK), lambda b, t: (b, 0, 0)),
            pl.BlockSpec((1, 8, CHUNK), lambda b, t: (b, 0, 0)),
            pl.BlockSpec(memory_space=pl.ANY),
        ],
        out_specs=[
            pl.BlockSpec((tb, D_MODEL), lambda b, t: (b * steps + t, 0)),
            pl.BlockSpec((1, 1, N_HEADS, DK, DV), lambda b, t: (layer, b, 0, 0, 0)),
            pl.BlockSpec((1, N_HEADS, DK), lambda b, t: (b, 0, 0)),
            pl.BlockSpec((1, 8, CHUNK), lambda b, t: (b, 0, 0)),
        ],
        out_shape=[
            jax.ShapeDtypeStruct((m_rows, D_MODEL), _BF),
            jax.ShapeDtypeStruct((DEPTH, n_seq, N_HEADS, DK, DV), _F32),
            jax.ShapeDtypeStruct((n_seq, N_HEADS, DK), _F32),
            jax.ShapeDtypeStruct((n_seq, 8, CHUNK), _F32),
        ],
        scratch_shapes=[
            pltpu.VMEM((N_HEADS, DK, EXT_COLS), _F32),
            pltpu.VMEM((tb // CHUNK, CHUNK, CHUNK), _F32),
            pltpu.VMEM((tb // CHUNK, 16, CHUNK), _F32),
        ],
        input_output_aliases=aliases,
        compiler_params=_params(2),
        name="mlstm_prompt",
    )(mid, gates_t, hng, c0, n0, m0, c_buf)


def _mlstm_sample_kernel(mid_ref, gt_ref, bias_ref, hng_ref, c0_ref, n0_ref, m0_ref, cbuf_ref,
                         mg_ref, c_ref, n_ref, m_ref,
                         qs_scr, kk_scr, wv_scr, rs_scr, dec_scr, lhs_q, lhs_k, lhs_wv):
    del cbuf_ref
    seq_len, bb, _ = mid_ref.shape
    hng = hng_ref[0]
    bias = bias_ref[0]
    m_prev = m0_ref[0]

    b_t, g_t, cm_t = [], [], []
    for t in range(seq_len):
        pre = _softcap(gt_ref[t] + bias)
        logf = _log_sigmoid(pre[:, GATE_F_COL:GATE_F_COL + CHUNK])
        b_t.append(logf if t == 0 else b_t[-1] + logf)
        g_t.append(pre[:, 0:CHUNK] - b_t[-1])
        cm_t.append(g_t[-1] if t == 0 else jnp.maximum(cm_t[-1], g_t[-1]))
    m_run = [jnp.maximum(cm, m_prev) for cm in cm_t]
    inter = [jnp.exp(m_prev - mr) for mr in m_run]
    em = [jnp.exp(-(b + mr)) for b, mr in zip(b_t, m_run)]
    m_last = m_run[-1]
    ws = [jnp.exp(g - m_last) for g in g_t]
    decay = jnp.exp(m_prev - m_last)
    m_ref[...] = b_t[-1] + m_last
    dec_scr[...] = decay

    lhs_q[...] = jnp.zeros(lhs_q.shape, _F32)
    lhs_k[...] = jnp.zeros(lhs_k.shape, _F32)
    lhs_wv[...] = jnp.zeros(lhs_wv.shape, _F32)
    n_all = n0_ref[0]
    den_part = {}
    pairs = [(t, s) for t in range(seq_len) for s in range(t + 1)]
    for h in range(N_HEADS):
        def col(x):
            return x[:, h:h + 1]
        mb = h * HEAD_COLS
        q = [mid_ref[t, :, mb + _M_Q:mb + _M_Q + DK] for t in range(seq_len)]
        k = [mid_ref[t, :, mb + _M_K:mb + _M_K + DK] for t in range(seq_len)]
        v = [mid_ref[t, :, mb + _M_V:mb + _M_V + DV] for t in range(seq_len)]
        n_h = n_all[:, h * DK:(h + 1) * DK]
        qs = [q[t] * col(inter[t]) for t in range(seq_len)]
        n_new = col(decay) * n_h
        for t in range(seq_len):
            qs_scr[t, :, h * DK:(h + 1) * DK] = qs[t]
            kk_scr[t, :, h * DK:(h + 1) * DK] = k[t]
            wv_scr[t, :, h * DV:(h + 1) * DV] = col(ws[t]) * v[t]
            n_new = n_new + col(ws[t]) * k[t]
        n_ref[:, h * DK:(h + 1) * DK] = n_new
        stacked = jnp.concatenate([q[t] * k[s] for t, s in pairs] + [qs[t] * n_h for t in range(seq_len)], axis=0)
        dots = jnp.sum(stacked, axis=-1, keepdims=True)
        decays = jnp.exp(jnp.concatenate([col(g_t[s]) - col(m_run[t]) for t, s in pairs], axis=0))
        weights = dots[:len(pairs) * bb] * decays
        for t in range(seq_len):
            den = dots[(len(pairs) + t) * bb:(len(pairs) + t + 1) * bb]
            num = None
            for idx, (t2, s) in enumerate(pairs):
                if t2 == t:
                    w = weights[idx * bb:(idx + 1) * bb]
                    num = w * v[s] if num is None else num + w * v[s]
                    den = den + w
            mg_ref[t, :, h * DV:(h + 1) * DV] = num
            den_part[h, t] = den

    n_par = lhs_q.shape[0] // 8

    def body(ii, carry):
        for u in range(n_par):
            i = ii * n_par + u
            rows = slice(8 * u, 8 * u + 8)
            for t in range(seq_len):
                lhs_q[8 * u + t:8 * u + t + 1, :] = qs_scr[t, pl.ds(i, 1), :]
                lhs_k[8 * u + t:8 * u + t + 1, :] = kk_scr[t, pl.ds(i, 1), :]
                lhs_wv[8 * u + t:8 * u + t + 1, :] = wv_scr[t, pl.ds(i, 1), :]
            for h in range(N_HEADS):
                c_old = c0_ref[0, i, h]
                r = _dot(lhs_q[rows, h * DK:(h + 1) * DK].astype(_BF), c_old.astype(_BF))
                for t in range(seq_len):
                    rs_scr[t, pl.ds(i, 1), h * DV:(h + 1) * DV] = r[t:t + 1, :]
                d_c = lax.dot_general(lhs_k[rows, h * DK:(h + 1) * DK].astype(_BF),
                                      lhs_wv[rows, h * DV:(h + 1) * DV].astype(_BF),
                                      (((0,), (0,)), ((), ())), preferred_element_type=_F32)
                c_ref[0, i, h] = dec_scr[pl.ds(i, 1), h:h + 1] * c_old + d_c
        return carry

    lax.fori_loop(0, bb // n_par, body, 0)

    blocks = [(h, t) for h in range(N_HEADS) for t in range(seq_len)]
    dens = jnp.concatenate([jnp.maximum(jnp.abs(den_part[h, t]), em[t][:, h:h + 1]) for h, t in blocks], axis=0)
    hraw = jnp.concatenate([mg_ref[t, :, h * DV:(h + 1) * DV] + rs_scr[t, :, h * DV:(h + 1) * DV]
                            for h, t in blocks], axis=0) * (1.0 / dens)
    ms = jnp.mean(hraw * hraw, axis=-1, keepdims=True)
    hn_all = hraw * lax.rsqrt(ms + EPS)
    for idx, (h, t) in enumerate(blocks):
        mb = h * HEAD_COLS
        hn = hn_all[idx * bb:(idx + 1) * bb] * hng[:, h * DV:(h + 1) * DV]
        ya = mid_ref[t, :, mb + _M_YA:mb + _M_YA + DV]
        og = mid_ref[t, :, mb + _M_OG:mb + _M_OG + DV]
        mg_ref[t, :, h * DV:(h + 1) * DV] = ya + og * hn


def _mlstm_sample_call(layer, mid3, gates3, bias, hng, c0, n0, m0, c_buf, *, bb):
    seq_len, n_seq, _ = mid3.shape
    aliases = {} if c_buf is None else {7: 1}
    if c_buf is None:
        c_buf = jnp.zeros((1,), _F32)
    return pl.pallas_call(
        _mlstm_sample_kernel,
        grid=(n_seq // bb,),
        in_specs=[
            pl.BlockSpec((seq_len, bb, N_HEADS * HEAD_COLS), lambda i: (0, i, 0)),
            pl.BlockSpec((seq_len, bb, GATE_COLS), lambda i: (0, i, 0)),
            pl.BlockSpec((1, 1, GATE_COLS), lambda i: (layer, 0, 0)),
            pl.BlockSpec((1, 1, D_MODEL), lambda i: (layer, 0, 0)),
            pl.BlockSpec((1, bb, N_HEADS, DK, DV), lambda i: (layer, i, 0, 0, 0)),
            pl.BlockSpec((1, bb, N_HEADS * DK), lambda i: (layer, i, 0)),
            pl.BlockSpec((1, bb, CHUNK), lambda i: (layer, i, 0)),
            pl.BlockSpec(memory_space=pl.ANY),
        ],
        out_specs=[
            pl.BlockSpec((seq_len, bb, D_MODEL), lambda i: (0, i, 0)),
            pl.BlockSpec((1, bb, N_HEADS, DK, DV), lambda i: (layer, i, 0, 0, 0)),
            pl.BlockSpec((bb, N_HEADS * DK), lambda i: (i, 0)),
            pl.BlockSpec((bb, CHUNK), lambda i: (i, 0)),
        ],
        out_shape=[
            jax.ShapeDtypeStruct((seq_len, n_seq, D_MODEL), _F32),
            jax.ShapeDtypeStruct((DEPTH, n_seq, N_HEADS, DK, DV), _F32),
            jax.ShapeDtypeStruct((n_seq, N_HEADS * DK), _F32),
            jax.ShapeDtypeStruct((n_seq, CHUNK), _F32),
        ],
        scratch_shapes=[
            pltpu.VMEM((seq_len, bb, N_HEADS * DK), _F32),
            pltpu.VMEM((seq_len, bb, N_HEADS * DK), _F32),
            pltpu.VMEM((seq_len, bb, N_HEADS * DV), _F32),
            pltpu.VMEM((seq_len, bb, N_HEADS * DV), _F32),
            pltpu.VMEM((bb, CHUNK), _F32),
            pltpu.VMEM((32, N_HEADS * DK), _F32),
            pltpu.VMEM((32, N_HEADS * DK), _F32),
            pltpu.VMEM((32, N_HEADS * DV), _F32),
        ],
        input_output_aliases=aliases,
        compiler_params=_params(1),
        name="mlstm_sample",
    )(mid3, gates3, bias, hng, c0, n0, m0, c_buf)


def _mlp_kernel(x_ref, mg_ref, gt1_ref, sc2_ref, sh2_ref, gt2_ref, g2_ref, wo_ref, wu_ref, wd_ref, gf_ref,
                o_ref, xmid, h2, acc, *, mod_reps, final_norm):
    f = pl.program_id(1)

    def mod(ref):
        return ref[0, 0, 0] if mod_reps is None else _rep_rows(ref[0, 0], mod_reps)

    @pl.when(f == 0)
    def _():
        mix = _dot(mg_ref[...].astype(_BF), wo_ref[0])
        xm = x_ref[...] + mod(gt1_ref) * mix
        xmid[...] = xm
        h2[...] = _rms_mod(xm, g2_ref[0], mod(sc2_ref), mod(sh2_ref)).astype(_BF)
        acc[...] = jnp.zeros(acc.shape, _F32)

    a = jnp.maximum(_dot(h2[...], wu_ref[0]), 0.0)
    acc[...] += _dot((a * a).astype(_BF), wd_ref[0])

    @pl.when(f == pl.num_programs(1) - 1)
    def _():
        y = xmid[...] + mod(gt2_ref) * acc[...]
        if final_norm:
            ms = jnp.mean(y * y, axis=-1, keepdims=True)
            y = y * lax.rsqrt(ms + EPS) * gf_ref[...]
        o_ref[...] = y


def _mlp_call(layer, x, merged, mod, row0, g2, w_out, w_up, w_down, g_final, *, tm, tf, seq_len, final_norm):
    m_rows = x.shape[0]
    pieces = (_GT1, _SC2, _SH2, _GT2)
    if seq_len >= tm:
        tiles_per_seq = seq_len // tm
        mod_reps = None
        mod_specs = [_mod_row_spec(layer, p, row0, lambda i, f: i // tiles_per_seq) for p in pieces]
    else:
        mod_reps = seq_len
        mod_specs = [pl.BlockSpec((1, 1, tm // seq_len, D_MODEL), functools.partial(lambda p, i, f: (layer, p, 0, 0), p))
                     for p in pieces]
    kern = functools.partial(_mlp_kernel, mod_reps=mod_reps, final_norm=final_norm)
    return pl.pallas_call(
        kern,
        grid=(m_rows // tm, D_FF // tf),
        in_specs=[
            pl.BlockSpec((tm, D_MODEL), lambda i, f: (i, 0)),
            pl.BlockSpec((tm, D_MODEL), lambda i, f: (i, 0)),
            *mod_specs,
            pl.BlockSpec((1, 1, D_MODEL), lambda i, f: (layer, 0, 0)),
            pl.BlockSpec((1, D_MODEL, D_MODEL), lambda i, f: (layer, 0, 0)),
            pl.BlockSpec((1, D_MODEL, tf), lambda i, f: (layer, 0, f)),
            pl.BlockSpec((1, tf, D_MODEL), lambda i, f: (layer, f, 0)),
            pl.BlockSpec((1, D_MODEL), lambda i, f: (0, 0)),
        ],
        out_specs=pl.BlockSpec((tm, D_MODEL), lambda i, f: (i, 0)),
        out_shape=jax.ShapeDtypeStruct((m_rows, D_MODEL), _F32),
        scratch_shapes=[
            pltpu.VMEM((tm, D_MODEL), _F32),
            pltpu.VMEM((tm, D_MODEL), _BF),
            pltpu.VMEM((tm, D_MODEL), _F32),
        ],
        compiler_params=_params(2),
        name="outproj_mlp",
    )(x, merged, mod, mod, mod, mod, g2, w_out, w_up, w_down, g_final)


def kernel(x_prompt, x_sample, state_conv, state_C, state_n, state_m, c_prompt, c_sample,
           w_ada, b_ada, g_norm1, g_norm2, w_in, b_gate, conv_w, conv_b, hn_g, w_out, w_up, w_down, g_final):
    n_p, seq_p, _ = x_prompt.shape
    n_s, seq_s, _ = x_sample.shape
    rows_s = n_s * seq_s
    tm_p = 1024

    w_t = jnp.swapaxes(w_in, 1, 2).astype(_BF)
    w_tail = w_t[:, _OFF_GA:]
    zrows = jnp.zeros((DEPTH, GATE_F_COL - N_HEADS, D_MODEL), _BF)
    w_gate = jnp.concatenate([w_t[:, _OFF_IG:_OFF_IG + N_HEADS], zrows,
                              w_t[:, _OFF_IG + N_HEADS:_OFF_GA], zrows], axis=1)
    w_out_b = w_out.astype(_BF)
    w_up_b = w_up.astype(_BF)
    w_down_b = w_down.astype(_BF)
    zb = jnp.zeros((DEPTH, 4), _F32)
    bgate = jnp.broadcast_to(
        jnp.concatenate([b_gate[:, :N_HEADS], zb, b_gate[:, N_HEADS:], zb], axis=-1)[:, :, None],
        (DEPTH, BIAS_ROWS, CHUNK))
    zb = jnp.zeros((DEPTH, GATE_F_COL - N_HEADS), _F32)
    bias_row = jnp.concatenate([b_gate[:, :N_HEADS], zb, b_gate[:, N_HEADS:], zb], axis=-1)[:, None, :]
    g1 = g_norm1.reshape(DEPTH, 1, D_MODEL)
    g2 = g_norm2.reshape(DEPTH, 1, D_MODEL)
    hng = hn_g.reshape(DEPTH, 1, D_MODEL)
    cb = conv_b.reshape(DEPTH, 1, D_MODEL)
    gfin = g_final.reshape(1, D_MODEL)

    mod = _ada_call(jnp.concatenate([c_sample, c_prompt], axis=0), w_ada, b_ada)
    mod_rows = mod.reshape(DEPTH, N_MOD, n_s + n_p, 1, D_MODEL)

    xp = x_prompt.reshape(n_p * seq_p, D_MODEL)
    xs = x_sample.transpose(1, 0, 2).reshape(rows_s, D_MODEL)
    conv_s_in = state_conv.transpose(0, 2, 1, 3)
    n_s_in = state_n.reshape(DEPTH, n_s, N_HEADS * DK)
    m_s_in = jnp.pad(state_m, ((0, 0), (0, 0), (0, CHUNK - N_HEADS)))

    zeros_conv = jnp.zeros((n_p, CONV_W - 1, D_MODEL), _F32)
    zeros_c = jnp.zeros((n_p, N_HEADS, DK, DV), _F32)
    zeros_n = jnp.zeros((n_p, N_HEADS, DK), _F32)
    zeros_m = jnp.zeros((n_p, 8, CHUNK), _F32)

    p_conv, p_n, p_m, s_conv, s_n, s_m = [], [], [], [], [], []
    p_c = s_c = None
    for l in range(DEPTH):
        final = l == DEPTH - 1

        mid, gates_t, nconv = _inproj_prompt_call(
            l, xp, mod_rows, n_s, g1, w_t, w_tail, w_gate, bgate, conv_w, cb, zeros_conv, seq_len=seq_p, tm=tm_p)
        merged, p_c, n1, m1 = _mlstm_prompt_call(
            l, mid, gates_t, hng, zeros_c, zeros_n, zeros_m, p_c, seq_len=seq_p, tb=1024)
        xp = _mlp_call(l, xp, merged, mod_rows, n_s, g2, w_out_b, w_up_b, w_down_b, gfin,
                       tm=512, tf=2048, seq_len=seq_p, final_norm=final)
        tiles_per_seq = seq_p // tm_p
        p_conv.append(nconv[tiles_per_seq - 1::tiles_per_seq]); p_n.append(n1); p_m.append(m1[:, :N_HEADS, 0])

        mid, gates, nconv = _inproj_sample_call(
            l, xs, mod, g1, w_t, w_tail, w_gate, conv_w, cb, conv_s_in, seq_len=seq_s)
        merged3, s_c, n1, m1 = _mlstm_sample_call(
            l, mid.reshape(seq_s, n_s, N_HEADS * HEAD_COLS), gates.reshape(seq_s, n_s, GATE_COLS), bias_row, hng,
            state_C, n_s_in, m_s_in, s_c, bb=16)
        xs = _mlp_call(l, xs, merged3.reshape(rows_s, D_MODEL), mod, 0, g2, w_out_b, w_up_b, w_down_b, gfin,
                       tm=rows_s, tf=1024, seq_len=seq_s, final_norm=final)
        s_conv.append(nconv.transpose(1, 0, 2))
        s_n.append(n1.reshape(n_s, N_HEADS, DK)); s_m.append(m1[:, :N_HEADS])

    return (xp.reshape(n_p, seq_p, D_MODEL), xs.reshape(seq_s, n_s, D_MODEL).transpose(1, 0, 2),
            jnp.stack(p_conv), p_c, jnp.stack(p_n), jnp.stack(p_m),
            jnp.stack(s_conv), s_c, jnp.stack(s_n), jnp.stack(s_m))
```

```python
import functools

import jax
import jax.numpy as jnp
from jax import lax
from jax.experimental import pallas as pl
from jax.experimental.pallas import tpu as pltpu

D_MODEL = 1024
N_HEADS = 4
DK = 128
DV = 256
D_FF = 4096
DEPTH = 4
CONV_W = 3
GATE_CAP = 15.0
EPS = 1e-6
CHUNK = 128
HEAD_COLS = 1024
GATE_COLS = 256
GATE_F_COL = 128
BIAS_ROWS = 16
GATE_ROWS = 24
EXT_COLS = DV + CHUNK
VMEM_LIMIT = 56 * 1024 * 1024

_OFF_XC, _OFF_BG, _OFF_CG = 0, 1024, 2048
_OFF_Q, _OFF_K, _OFF_V, _OFF_O = 3072, 3584, 4096, 5120
_OFF_IG, _OFF_GA, _OFF_GB = 6144, 6152, 7176

_M_YA, _M_Q, _M_K, _M_V, _M_OG = 0, 256, 384, 512, 768

_BF = jnp.bfloat16
_F32 = jnp.float32


def _dot(a, b):
    return jnp.dot(a, b, preferred_element_type=_F32)


def _dot_nt(a, b):
    return lax.dot_general(a, b, (((1,), (1,)), ((), ())), preferred_element_type=_F32)


def _sigmoid(x):
    return 0.5 * jnp.tanh(0.5 * x) + 0.5


def _rms_mod(x, g, sc, sh):
    ms = jnp.mean(x * x, axis=-1, keepdims=True)
    return (x * lax.rsqrt(ms + EPS)) * (g * (1.0 + sc)) + sh


def _rep_rows(v, reps):
    return v if reps == 1 else jnp.concatenate([v] * reps, axis=0)


def _params(n_axes):
    return pltpu.CompilerParams(dimension_semantics=("arbitrary",) * n_axes, vmem_limit_bytes=VMEM_LIMIT)


N_MOD = 6
_SH1, _SC1, _GT1, _SH2, _SC2, _GT2 = range(N_MOD)


def _ada_kernel(c_ref, w_ref, b_ref, o_ref):
    c = c_ref[...]
    a = (c * _sigmoid(c)).astype(_BF)
    o_ref[0, 0] = _dot(a, w_ref[0].astype(_BF)) + b_ref[0]


def _ada_call(c_all, w_ada, b_ada):
    n_rows = c_all.shape[0]
    return pl.pallas_call(
        _ada_kernel,
        grid=(DEPTH, N_MOD),
        in_specs=[
            pl.BlockSpec((n_rows, D_MODEL), lambda l, j: (0, 0)),
            pl.BlockSpec((1, D_MODEL, D_MODEL), lambda l, j: (l, 0, j)),
            pl.BlockSpec((1, 1, D_MODEL), lambda l, j: (l, 0, j)),
        ],
        out_specs=pl.BlockSpec((1, 1, n_rows, D_MODEL), lambda l, j: (l, j, 0, 0)),
        out_shape=jax.ShapeDtypeStruct((DEPTH, N_MOD, n_rows, D_MODEL), _F32),
        compiler_params=_params(2),
        name="ada_mod",
    )(c_all, w_ada, b_ada.reshape(DEPTH, 1, N_MOD * D_MODEL))


def _w_in_specs(layer, idx):
    def spec(width, offset):
        return pl.BlockSpec((1, width, D_MODEL), lambda *g: (layer, offset // width + idx(*g), 0))
    return [spec(256, _OFF_XC), spec(256, _OFF_BG), spec(256, _OFF_CG), spec(128, _OFF_Q), spec(128, _OFF_K),
            spec(256, _OFF_V), spec(256, _OFF_O), spec(256, 0), spec(256, _OFF_GB - _OFF_GA)]


def _scan_lanes(x, combine, identity, seg_len):
    pos = lax.broadcasted_iota(jnp.int32, x.shape, 1) % seg_len
    shift = 1
    while shift < seg_len:
        x = combine(x, jnp.where(pos >= shift, pltpu.roll(x, shift, 1), identity))
        shift *= 2
    return x


def _softcap(a):
    return GATE_CAP * jnp.tanh(a / GATE_CAP)


def _log_sigmoid(x):
    return -(jnp.maximum(-x, 0.0) + jnp.log(1.0 + jnp.exp(-jnp.abs(x))))


def _gate_rows_prompt(gates, bias, gt_ref, first_chunk):
    n_chunks = gates.shape[0] // CHUNK
    gates_t = gates.T

    def stack(r0):
        return jnp.concatenate([gates_t[r0:r0 + 8, c * CHUNK:(c + 1) * CHUNK] for c in range(n_chunks)], axis=0)

    i_pre = _softcap(stack(0) + jnp.concatenate([bias[0:8]] * n_chunks, axis=0))
    f_pre = _softcap(stack(GATE_F_COL) + jnp.concatenate([bias[8:16]] * n_chunks, axis=0))
    b = _scan_lanes(_log_sigmoid(f_pre), jnp.add, 0.0, CHUNK)
    g = i_pre - b
    cm = _scan_lanes(g, jnp.maximum, -3e38, CHUNK)
    for c in range(n_chunks):
        gt_ref[first_chunk + c, 0:8] = b[c * 8:(c + 1) * 8]
        gt_ref[first_chunk + c, 8:16] = g[c * 8:(c + 1) * 8]
        gt_ref[first_chunk + c, 16:24] = cm[c * 8:(c + 1) * 8]


def _inproj_project(h, w_refs):
    wxc, wbg, wcg, wq, wk, wv, wo, wga, wgb = w_refs
    xc = _dot_nt(h, wxc[0])
    cg = _dot_nt(h, wcg[0])
    qk = _dot_nt(h, jnp.concatenate([wq[0], wk[0]], axis=0))
    return dict(u=cg * xc, bg=_dot_nt(h, wbg[0]), ga=_dot_nt(h, wga[0]), qk=qk,
                v=_dot_nt(h, wv[0]), o=_dot_nt(h, wo[0]), gb=_dot_nt(h, wgb[0]))


def _inproj_epilogue(p, conv, mid_ref, out_dtype, r0=0):
    rows = slice(r0, r0 + conv.shape[0])
    mid_ref[rows, _M_YA:_M_YA + 256] = (_sigmoid(p["ga"]) * (p["bg"] * conv)).astype(out_dtype)
    mid_ref[rows, _M_Q:_M_Q + DK] = p["qk"][:, :DK].astype(out_dtype)
    mid_ref[rows, _M_K:_M_K + DK] = (p["qk"][:, DK:] * (DK ** -0.5)).astype(out_dtype)
    mid_ref[rows, _M_V:_M_V + 256] = p["v"].astype(out_dtype)
    mid_ref[rows, _M_OG:_M_OG + 256] = (_sigmoid(p["o"]) * _sigmoid(p["gb"])).astype(out_dtype)


_RAW_COLS = 8 * 256


def _inproj_prompt_kernel(x_ref, sc_ref, sh_ref, g_ref, wxc, wbg, wcg, wq, wk, wv, wo, wga, wgb, wg_ref,
                          bg_ref, cw_ref, cb_ref, cprev_ref, mid_ref, gt_ref, nconv_ref,
                          h_scr, gate_scr, raw_even, raw_odd, ubuf, carry, *, tiles_per_seq, n_work):
    j = pl.program_id(0)
    tm = x_ref.shape[0]
    cur = jnp.minimum(j, n_work - 1)
    s = cur % N_HEADS
    prev = jnp.maximum(j - 1, 0)
    tile_p = prev // N_HEADS
    s_p = prev % N_HEADS
    chunks_per_step = (tm // CHUNK) // N_HEADS
    raw = (raw_even, raw_odd)

    @pl.when(j == 0)
    def _():
        raw_odd[...] = jnp.zeros(raw_odd.shape, _F32)
        carry[...] = jnp.zeros(carry.shape, _F32)

    @pl.when(jnp.logical_and(s == 0, j < n_work))
    def _():
        h = _rms_mod(x_ref[...], g_ref[0], sc_ref[0, 0, 0], sh_ref[0, 0, 0]).astype(_BF)
        h_scr[...] = h
        gate_scr[...] = _dot_nt(h, wg_ref[0])

    def step(slot):
        rows0 = pl.multiple_of(s * (chunks_per_step * CHUNK), CHUNK)
        _gate_rows_prompt(gate_scr[pl.ds(rows0, chunks_per_step * CHUNK), :], bg_ref[0], gt_ref,
                          s * chunks_per_step)

        h = h_scr[...]
        weights = (wxc[0], wcg[0], wbg[0], wga[0], jnp.concatenate([wq[0], wk[0]], axis=0),
                   wv[0], wo[0], wgb[0])
        seq_start = (tile_p % tiles_per_seq) == 0
        ubuf[6:8, :] = jnp.where(seq_start, cprev_ref[0], carry[s_p])
        cw = cw_ref[0]
        conv_bias = cb_ref[0]
        rows_per_part = tm // len(weights)
        rc = min(CHUNK // 2, rows_per_part)
        for idx, w in enumerate(weights):
            raw[slot][:, idx * 256:(idx + 1) * 256] = _dot_nt(h, w)
            for r0 in range(idx * rows_per_part, (idx + 1) * rows_per_part, rc):
                def piece(k):
                    return raw[1 - slot][r0:r0 + rc, k * 256:(k + 1) * 256]

                u = piece(1) * piece(0)
                ubuf[8 + r0:8 + r0 + rc, :] = u
                conv = (conv_bias + cw[0:1] * ubuf[6 + r0:6 + r0 + rc, :]
                        + cw[1:2] * ubuf[7 + r0:7 + r0 + rc, :] + cw[2:3] * u)
                _inproj_epilogue(dict(bg=piece(2), ga=piece(3), qk=piece(4), v=piece(5), o=piece(6),
                                      gb=piece(7)), conv, mid_ref, _BF, r0)
        last2 = ubuf[6 + tm:8 + tm, :]
        carry[s_p] = last2
        nconv_ref[0] = last2

    for parity in range(2):
        pl.when(j % 2 == parity)(functools.partial(step, parity))


def _mod_row_spec(layer, piece, row0, seq_of):
    return pl.BlockSpec((1, 1, 1, 1, D_MODEL), lambda *g: (layer, piece, row0 + seq_of(*g), 0, 0))


def _inproj_prompt_call(layer, x, mod_rows, row0, g, w_t, w_tail, w_gate, bgate, conv_w, conv_b, conv_prev,
                        *, seq_len, tm):
    m_rows = x.shape[0]
    n_tiles = m_rows // tm
    tiles_per_seq = seq_len // tm
    n_work = n_tiles * N_HEADS

    def cur(j):
        return jnp.minimum(j, n_work - 1)

    def prev(j):
        return jnp.maximum(j - 1, 0)

    kern = functools.partial(_inproj_prompt_kernel, tiles_per_seq=tiles_per_seq, n_work=n_work)
    w_specs = _w_in_specs(layer, lambda j: cur(j) % N_HEADS)
    seq_of_cur = lambda j: cur(j) // N_HEADS // tiles_per_seq
    return pl.pallas_call(
        kern,
        grid=(n_work + 1,),
        in_specs=[
            pl.BlockSpec((tm, D_MODEL), lambda j: (cur(j) // N_HEADS, 0)),
            _mod_row_spec(layer, _SC1, row0, seq_of_cur),
            _mod_row_spec(layer, _SH1, row0, seq_of_cur),
            pl.BlockSpec((1, 1, D_MODEL), lambda j: (layer, 0, 0)),
            *w_specs,
            pl.BlockSpec((1, GATE_COLS, D_MODEL), lambda j: (layer, 0, 0)),
            pl.BlockSpec((1, BIAS_ROWS, CHUNK), lambda j: (layer, 0, 0)),
            pl.BlockSpec((1, CONV_W, 256), lambda j: (layer, 0, prev(j) % N_HEADS)),
            pl.BlockSpec((1, 1, 256), lambda j: (layer, 0, prev(j) % N_HEADS)),
            pl.BlockSpec((1, CONV_W - 1, 256),
                         lambda j: (prev(j) // N_HEADS // tiles_per_seq, 0, prev(j) % N_HEADS)),
        ],
        out_specs=[
            pl.BlockSpec((tm, HEAD_COLS), lambda j: (prev(j) // N_HEADS, prev(j) % N_HEADS)),
            pl.BlockSpec((tm // CHUNK, GATE_ROWS, CHUNK), lambda j: (cur(j) // N_HEADS, 0, 0)),
            pl.BlockSpec((1, CONV_W - 1, 256), lambda j: (prev(j) // N_HEADS, 0, prev(j) % N_HEADS)),
        ],
        out_shape=[
            jax.ShapeDtypeStruct((m_rows, N_HEADS * HEAD_COLS), _BF),
            jax.ShapeDtypeStruct((m_rows // CHUNK, GATE_ROWS, CHUNK), _F32),
            jax.ShapeDtypeStruct((n_tiles, CONV_W - 1, D_MODEL), _F32),
        ],
        scratch_shapes=[
            pltpu.VMEM((tm, D_MODEL), _BF),
            pltpu.VMEM((tm, GATE_COLS), _F32),
            pltpu.VMEM((tm, _RAW_COLS), _F32),
            pltpu.VMEM((tm, _RAW_COLS), _F32),
            pltpu.VMEM((tm + 8, 256), _F32),
            pltpu.VMEM((N_HEADS, CONV_W - 1, 256), _F32),
        ],
        compiler_params=_params(1),
        name="inproj_prompt",
    )(x, mod_rows, mod_rows, g, *([w_t] * 7), *([w_tail] * 2), w_gate, bgate, conv_w, conv_b, conv_prev)


def _inproj_sample_kernel(x_ref, sc_ref, sh_ref, g_ref, wxc, wbg, wcg, wq, wk, wv, wo, wga, wgb, wg_ref,
                          cw_ref, cb_ref, cprev_ref, mid_ref, gt_ref, nconv_ref, h_scr, *, seq_len):
    s = pl.program_id(0)
    n_b = sc_ref.shape[2]

    @pl.when(s == 0)
    def _():
        h = _rms_mod(x_ref[...], g_ref[0], _rep_rows(sc_ref[0, 0], seq_len), _rep_rows(sh_ref[0, 0], seq_len))
        h = h.astype(_BF)
        h_scr[...] = h
        gt_ref[...] = _dot_nt(h, wg_ref[0])

    p = _inproj_project(h_scr[...], (wxc, wbg, wcg, wq, wk, wv, wo, wga, wgb))
    u = p["u"]
    prev0 = cprev_ref[0, 0]
    prev1 = cprev_ref[0, 1]
    p1 = jnp.concatenate([prev1, u[0:(seq_len - 1) * n_b]], axis=0)
    p2 = jnp.concatenate([prev0, prev1, u[0:(seq_len - 2) * n_b]], axis=0)
    cw = cw_ref[0]
    conv = cb_ref[0] + cw[0:1] * p2 + cw[1:2] * p1 + cw[2:3] * u
    nconv_ref[0] = u[(seq_len - 2) * n_b:(seq_len - 1) * n_b]
    nconv_ref[1] = u[(seq_len - 1) * n_b:seq_len * n_b]
    _inproj_epilogue(p, conv, mid_ref, _F32)


def _inproj_sample_call(layer, x, mod, g, w_t, w_tail, w_gate, conv_w, conv_b, conv_prev, *, seq_len):
    tm = x.shape[0]
    n_b = tm // seq_len
    kern = functools.partial(_inproj_sample_kernel, seq_len=seq_len)
    w_specs = _w_in_specs(layer, lambda s: s)
    return pl.pallas_call(
        kern,
        grid=(N_HEADS,),
        in_specs=[
            pl.BlockSpec((tm, D_MODEL), lambda s: (0, 0)),
            pl.BlockSpec((1, 1, n_b, D_MODEL), lambda s: (layer, _SC1, 0, 0)),
            pl.BlockSpec((1, 1, n_b, D_MODEL), lambda s: (layer, _SH1, 0, 0)),
            pl.BlockSpec((1, 1, D_MODEL), lambda s: (layer, 0, 0)),
            *w_specs,
            pl.BlockSpec((1, GATE_COLS, D_MODEL), lambda s: (layer, 0, 0)),
            pl.BlockSpec((1, CONV_W, 256), lambda s: (layer, 0, s)),
            pl.BlockSpec((1, 1, 256), lambda s: (layer, 0, s)),
            pl.BlockSpec((1, CONV_W - 1, n_b, 256), lambda s: (layer, 0, 0, s)),
        ],
        out_specs=[
            pl.BlockSpec((tm, HEAD_COLS), lambda s: (0, s)),
            pl.BlockSpec((tm, GATE_COLS), lambda s: (0, 0)),
            pl.BlockSpec((CONV_W - 1, n_b, 256), lambda s: (0, 0, s)),
        ],
        out_shape=[
            jax.ShapeDtypeStruct((tm, N_HEADS * HEAD_COLS), _F32),
            jax.ShapeDtypeStruct((tm, GATE_COLS), _F32),
            jax.ShapeDtypeStruct((CONV_W - 1, n_b, D_MODEL), _F32),
        ],
        scratch_shapes=[pltpu.VMEM((tm, D_MODEL), _BF)],
        compiler_params=_params(1),
        name="inproj_sample",
    )(x, mod, mod, g, *([w_t] * 7), *([w_tail] * 2), w_gate, conv_w, conv_b, conv_prev)


def _causal_mask(lq):
    row_id = lax.broadcasted_iota(jnp.int32, (lq, CHUNK), 0)
    col_id = lax.broadcasted_iota(jnp.int32, (lq, CHUNK), 1)
    return col_id <= row_id


def _chunk_scores(a, causal):
    dmat = jnp.where(causal, jnp.exp(a["g_row"] - a["m_col"]), 0.0)
    return (_dot_nt(a["q"], a["k"]) * dmat).astype(_BF)


def _chunk_output(a, s_bf, cx_ref, h, hng_h):
    q_inter = (a["q"].astype(_F32) * a["inter_col"]).astype(_BF)
    cx = cx_ref[h]
    rhs = jnp.concatenate([a["v_ext"], cx.astype(_BF)], axis=0)
    res = _dot(jnp.concatenate([s_bf, q_inter], axis=1), rhs)
    kw_t = a["k"].T.astype(_F32) * a["ws_row"]
    cx_ref[h] = a["dec"] * cx + _dot(kw_t.astype(_BF), a["v_ext"])
    num = res[:, :DV]
    rden = 1.0 / jnp.maximum(jnp.abs(res[:, DV:]), a["em_col"])
    sq = jnp.sum(num * num, axis=-1, keepdims=True) * (1.0 / DV)
    scale = rden * lax.rsqrt(rden * rden * sq + EPS)
    hn = num * jnp.concatenate([scale, scale], axis=1) * hng_h
    return a["ya"] + a["og"] * hn


def _mlstm_prompt_kernel(mid_ref, gt_ref, hng_ref, c0_ref, n0_ref, m0_ref, cbuf_ref,
                         mg_ref, c_ref, n_ref, m_ref, cx_ref, cols_scr, rows_scr):
    del cbuf_ref
    t = pl.program_id(1)

    @pl.when(t == 0)
    def _():
        m_ref[...] = m0_ref[...]
        for h in range(N_HEADS):
            cx_ref[h, :, 0:DV] = c0_ref[0, h]
            cx_ref[h, :, DV:EXT_COLS] = jnp.broadcast_to(n0_ref[0, h:h + 1, :], (CHUNK, DK)).T

    n_chunks = mid_ref.shape[0] // CHUNK
    rows = n_chunks * 8
    hng = hng_ref[0]
    gates = gt_ref[...]
    b = gates[:, 0:8, :].reshape(rows, CHUNK)
    g = gates[:, 8:16, :].reshape(rows, CHUNK)
    cm = gates[:, 16:24, :].reshape(rows, CHUNK)
    b_last = jnp.broadcast_to(b[:, CHUNK - 1:CHUNK], (rows, CHUNK))
    cm_last = jnp.broadcast_to(cm[:, CHUNK - 1:CHUNK], (rows, CHUNK))

    m_prev = m_ref[0]
    m_prevs = []
    for c in range(n_chunks):
        m_prevs.append(m_prev)
        m_prev = b_last[c * 8:(c + 1) * 8] + jnp.maximum(cm_last[c * 8:(c + 1) * 8], m_prev)
    m_ref[0] = m_prev
    m_prev_all = jnp.concatenate(m_prevs, axis=0)

    m_run = jnp.maximum(cm, m_prev_all)
    inter = jnp.exp(m_prev_all - m_run)
    em = jnp.exp(-(b + m_run))
    m_last = jnp.maximum(cm_last, m_prev_all)
    ws = jnp.exp(g - m_last)
    decay = jnp.exp(m_prev_all - m_last)

    pad = jnp.zeros((CHUNK - 24, CHUNK), _F32)
    for c in range(n_chunks):
        sl = slice(c * 8, (c + 1) * 8)
        cols_scr[c] = jnp.concatenate([m_run[sl], inter[sl], em[sl], pad], axis=0).T
        rows_scr[c, 0:8] = g[sl]
        rows_scr[c, 8:16] = decay[sl]
        rows_scr[c, 16:24] = ws[sl]

    causal = _causal_mask(CHUNK)
    ones = jnp.ones((CHUNK, CHUNK), _BF)

    def chunk_body(c, carry):
        cols = cols_scr[c]
        vec = rows_scr[c]
        rows_c = pl.ds(pl.multiple_of(c * CHUNK, CHUNK), CHUNK)
        heads = []
        for h in range(N_HEADS):
            mb = h * HEAD_COLS
            v = mid_ref[rows_c, mb + _M_V:mb + _M_V + 256]
            heads.append(dict(
                q=mid_ref[rows_c, mb + _M_Q:mb + _M_Q + 128],
                k=mid_ref[rows_c, mb + _M_K:mb + _M_K + 128],
                v_ext=jnp.concatenate([v, ones], axis=1),
                ya=mid_ref[rows_c, mb + _M_YA:mb + _M_YA + 256].astype(_F32),
                og=mid_ref[rows_c, mb + _M_OG:mb + _M_OG + 256].astype(_F32),
                g_row=vec[h:h + 1, :],
                m_col=cols[:, h:h + 1],
                inter_col=cols[:, 8 + h:9 + h],
                em_col=cols[:, 16 + h:17 + h],
                ws_row=vec[16 + h:17 + h, :],
                dec=vec[8 + h:9 + h, 0:1]))
        outs = []
        ahead = 2
        scores = [_chunk_scores(heads[h], causal) for h in range(ahead)]
        for h, a in enumerate(heads):
            if h + ahead < N_HEADS:
                scores.append(_chunk_scores(heads[h + ahead], causal))
            outs.append(_chunk_output(a, scores[h], cx_ref, h, hng[:, h * DV:(h + 1) * DV]))
        mg_ref[rows_c, :] = jnp.concatenate(outs, axis=1).astype(_BF)
        return carry

    lax.fori_loop(0, n_chunks, chunk_body, 0)

    @pl.when(t == pl.num_programs(1) - 1)
    def _():
        for h in range(N_HEADS):
            c_ref[0, 0, h] = cx_ref[h, :, 0:DV]
            n_ref[0, h:h + 1, :] = cx_ref[h, :, DV:EXT_COLS].T[0:1, :]


def _mlstm_prompt_call(layer, mid, gates_t, hng, c0, n0, m0, c_buf, *, seq_len, tb):
    m_rows = mid.shape[0]
    n_seq = m_rows // seq_len
    steps = seq_len // tb
    aliases = {} if c_buf is None else {6: 1}
    if c_buf is None:
        c_buf = jnp.zeros((1,), _F32)
    return pl.pallas_call(
        _mlstm_prompt_kernel,
        grid=(n_seq, steps),
        in_specs=[
            pl.BlockSpec((tb, N_HEADS * HEAD_COLS), lambda b, t: (b * steps + t, 0)),
            pl.BlockSpec((tb // CHUNK, GATE_ROWS, CHUNK), lambda b, t: (b * steps + t, 0, 0)),
            pl.BlockSpec((1, 1, D_MODEL), lambda b, t: (layer, 0, 0)),
            pl.BlockSpec((1, N_HEADS, DK, DV), lambda b, t: (b, 0, 0, 0)),
            pl.BlockSpec((1, N_HEADS, DK), lambda b, t: (b, 0, 0)),
            pl.BlockSpec((1, 8, CHUNK), lambda b, t: (b, 0, 0)),
            pl.BlockSpec(memory_space=pl.ANY),
        ],
        out_specs=[
            pl.BlockSpec((tb, D_MODEL), lambda b, t: (b * steps + t, 0)),
            pl.BlockSpec((1, 1, N_HEADS, DK, DV), lambda b, t: (layer, b, 0, 0, 0)),
            pl.BlockSpec((1, N_HEADS, DK), lambda b, t: (b, 0, 0)),
            pl.BlockSpec((1, 8, CHUNK), lambda b, t: (b, 0, 0)),
        ],
        out_shape=[
            jax.ShapeDtypeStruct((m_rows, D_MODEL), _BF),
            jax.ShapeDtypeStruct((DEPTH, n_seq, N_HEADS, DK, DV), _F32),
            jax.ShapeDtypeStruct((n_seq, N_HEADS, DK), _F32),
            jax.ShapeDtypeStruct((n_seq, 8, CHUNK), _F32),
        ],
        scratch_shapes=[
            pltpu.VMEM((N_HEADS, DK, EXT_COLS), _F32),
            pltpu.VMEM((tb // CHUNK, CHUNK, CHUNK), _F32),
            pltpu.VMEM((tb // CHUNK, 24, CHUNK), _F32),
        ],
        input_output_aliases=aliases,
        compiler_params=_params(2),
        name="mlstm_prompt",
    )(mid, gates_t, hng, c0, n0, m0, c_buf)


def _mlstm_sample_kernel(mid_ref, gt_ref, bias_ref, hng_ref, c0_ref, n0_ref, m0_ref, cbuf_ref,
                         mg_ref, c_ref, n_ref, m_ref,
                         qs_scr, kk_scr, wv_scr, rs_scr, dec_scr, lhs_q, lhs_k, lhs_wv):
    del cbuf_ref
    seq_len, bb, _ = mid_ref.shape
    hng = hng_ref[0]
    bias = bias_ref[0]
    m_prev = m0_ref[0]

    b_t, g_t, cm_t = [], [], []
    for t in range(seq_len):
        pre = _softcap(gt_ref[t] + bias)
        logf = _log_sigmoid(pre[:, GATE_F_COL:GATE_F_COL + CHUNK])
        b_t.append(logf if t == 0 else b_t[-1] + logf)
        g_t.append(pre[:, 0:CHUNK] - b_t[-1])
        cm_t.append(g_t[-1] if t == 0 else jnp.maximum(cm_t[-1], g_t[-1]))
    m_run = [jnp.maximum(cm, m_prev) for cm in cm_t]
    inter = [jnp.exp(m_prev - mr) for mr in m_run]
    em = [jnp.exp(-(b + mr)) for b, mr in zip(b_t, m_run)]
    m_last = m_run[-1]
    ws = [jnp.exp(g - m_last) for g in g_t]
    decay = jnp.exp(m_prev - m_last)
    m_ref[...] = b_t[-1] + m_last
    dec_scr[...] = decay

    lhs_q[...] = jnp.zeros(lhs_q.shape, _F32)
    lhs_k[...] = jnp.zeros(lhs_k.shape, _F32)
    lhs_wv[...] = jnp.zeros(lhs_wv.shape, _F32)
    n_all = n0_ref[0]
    den_part = {}
    pairs = [(t, s) for t in range(seq_len) for s in range(t + 1)]
    for h in range(N_HEADS):
        def col(x):
            return x[:, h:h + 1]
        mb = h * HEAD_COLS
        q = [mid_ref[t, :, mb + _M_Q:mb + _M_Q + DK] for t in range(seq_len)]
        k = [mid_ref[t, :, mb + _M_K:mb + _M_K + DK] for t in range(seq_len)]
        v = [mid_ref[t, :, mb + _M_V:mb + _M_V + DV] for t in range(seq_len)]
        n_h = n_all[:, h * DK:(h + 1) * DK]
        qs = [q[t] * col(inter[t]) for t in range(seq_len)]
        n_new = col(decay) * n_h
        for t in range(seq_len):
            qs_scr[t, :, h * DK:(h + 1) * DK] = qs[t]
            kk_scr[t, :, h * DK:(h + 1) * DK] = k[t]
            wv_scr[t, :, h * DV:(h + 1) * DV] = col(ws[t]) * v[t]
            n_new = n_new + col(ws[t]) * k[t]
        n_ref[:, h * DK:(h + 1) * DK] = n_new
        stacked = jnp.concatenate([q[t] * k[s] for t, s in pairs] + [qs[t] * n_h for t in range(seq_len)], axis=0)
        dots = jnp.sum(stacked, axis=-1, keepdims=True)
        decays = jnp.exp(jnp.concatenate([col(g_t[s]) - col(m_run[t]) for t, s in pairs], axis=0))
        weights = dots[:len(pairs) * bb] * decays
        for t in range(seq_len):
            den = dots[(len(pairs) + t) * bb:(len(pairs) + t + 1) * bb]
            num = None
            for idx, (t2, s) in enumerate(pairs):
                if t2 == t:
                    w = weights[idx * bb:(idx + 1) * bb]
                    num = w * v[s] if num is None else num + w * v[s]
                    den = den + w
            mg_ref[t, :, h * DV:(h + 1) * DV] = num
            den_part[h, t] = den

    n_par = lhs_q.shape[0] // 8

    def body(ii, carry):
        for u in range(n_par):
            i = ii * n_par + u
            rows = slice(8 * u, 8 * u + 8)
            for t in range(seq_len):
                lhs_q[8 * u + t:8 * u + t + 1, :] = qs_scr[t, pl.ds(i, 1), :]
                lhs_k[8 * u + t:8 * u + t + 1, :] = kk_scr[t, pl.ds(i, 1), :]
                lhs_wv[8 * u + t:8 * u + t + 1, :] = wv_scr[t, pl.ds(i, 1), :]
            for h in range(N_HEADS):
                c_old = c0_ref[0, i, h]
                r = _dot(lhs_q[rows, h * DK:(h + 1) * DK].astype(_BF), c_old.astype(_BF))
                for t in range(seq_len):
                    rs_scr[t, pl.ds(i, 1), h * DV:(h + 1) * DV] = r[t:t + 1, :]
                d_c = lax.dot_general(lhs_k[rows, h * DK:(h + 1) * DK].astype(_BF),
                                      lhs_wv[rows, h * DV:(h + 1) * DV].astype(_BF),
                                      (((0,), (0,)), ((), ())), preferred_element_type=_F32)
                c_ref[0, i, h] = dec_scr[pl.ds(i, 1), h:h + 1] * c_old + d_c
        return carry

    lax.fori_loop(0, bb // n_par, body, 0)

    blocks = [(h, t) for h in range(N_HEADS) for t in range(seq_len)]
    dens = jnp.concatenate([jnp.maximum(jnp.abs(den_part[h, t]), em[t][:, h:h + 1]) for h, t in blocks], axis=0)
    hraw = jnp.concatenate([mg_ref[t, :, h * DV:(h + 1) * DV] + rs_scr[t, :, h * DV:(h + 1) * DV]
                            for h, t in blocks], axis=0) * (1.0 / dens)
    ms = jnp.mean(hraw * hraw, axis=-1, keepdims=True)
    hn_all = hraw * lax.rsqrt(ms + EPS)
    for idx, (h, t) in enumerate(blocks):
        mb = h * HEAD_COLS
        hn = hn_all[idx * bb:(idx + 1) * bb] * hng[:, h * DV:(h + 1) * DV]
        ya = mid_ref[t, :, mb + _M_YA:mb + _M_YA + DV]
        og = mid_ref[t, :, mb + _M_OG:mb + _M_OG + DV]
        mg_ref[t, :, h * DV:(h + 1) * DV] = ya + og * hn


def _mlstm_sample_call(layer, mid3, gates3, bias, hng, c0, n0, m0, c_buf, *, bb):
    seq_len, n_seq, _ = mid3.shape
    aliases = {} if c_buf is None else {7: 1}
    if c_buf is None:
        c_buf = jnp.zeros((1,), _F32)
    return pl.pallas_call(
        _mlstm_sample_kernel,
        grid=(n_seq // bb,),
        in_specs=[
            pl.BlockSpec((seq_len, bb, N_HEADS * HEAD_COLS), lambda i: (0, i, 0)),
            pl.BlockSpec((seq_len, bb, GATE_COLS), lambda i: (0, i, 0)),
            pl.BlockSpec((1, 1, GATE_COLS), lambda i: (layer, 0, 0)),
            pl.BlockSpec((1, 1, D_MODEL), lambda i: (layer, 0, 0)),
            pl.BlockSpec((1, bb, N_HEADS, DK, DV), lambda i: (layer, i, 0, 0, 0)),
            pl.BlockSpec((1, bb, N_HEADS * DK), lambda i: (layer, i, 0)),
            pl.BlockSpec((1, bb, CHUNK), lambda i: (layer, i, 0)),
            pl.BlockSpec(memory_space=pl.ANY),
        ],
        out_specs=[
            pl.BlockSpec((seq_len, bb, D_MODEL), lambda i: (0, i, 0)),
            pl.BlockSpec((1, bb, N_HEADS, DK, DV), lambda i: (layer, i, 0, 0, 0)),
            pl.BlockSpec((bb, N_HEADS * DK), lambda i: (i, 0)),
            pl.BlockSpec((bb, CHUNK), lambda i: (i, 0)),
        ],
        out_shape=[
            jax.ShapeDtypeStruct((seq_len, n_seq, D_MODEL), _F32),
            jax.ShapeDtypeStruct((DEPTH, n_seq, N_HEADS, DK, DV), _F32),
            jax.ShapeDtypeStruct((n_seq, N_HEADS * DK), _F32),
            jax.ShapeDtypeStruct((n_seq, CHUNK), _F32),
        ],
        scratch_shapes=[
            pltpu.VMEM((seq_len, bb, N_HEADS * DK), _F32),
            pltpu.VMEM((seq_len, bb, N_HEADS * DK), _F32),
            pltpu.VMEM((seq_len, bb, N_HEADS * DV), _F32),
            pltpu.VMEM((seq_len, bb, N_HEADS * DV), _F32),
            pltpu.VMEM((bb, CHUNK), _F32),
            pltpu.VMEM((32, N_HEADS * DK), _F32),
            pltpu.VMEM((32, N_HEADS * DK), _F32),
            pltpu.VMEM((32, N_HEADS * DV), _F32),
        ],
        input_output_aliases=aliases,
        compiler_params=_params(1),
        name="mlstm_sample",
    )(mid3, gates3, bias, hng, c0, n0, m0, c_buf)


def _mlp_kernel(x_ref, mg_ref, gt1_ref, sc2_ref, sh2_ref, gt2_ref, g2_ref, wo_ref, wu_ref, wd_ref, gf_ref,
                o_ref, xmid, h2, acc, *, mod_reps, final_norm):
    f = pl.program_id(1)

    def mod(ref):
        return ref[0, 0, 0] if mod_reps is None else _rep_rows(ref[0, 0], mod_reps)

    @pl.when(f == 0)
    def _():
        mix = _dot(mg_ref[...].astype(_BF), wo_ref[0])
        xm = x_ref[...] + mod(gt1_ref) * mix
        xmid[...] = xm
        h2[...] = _rms_mod(xm, g2_ref[0], mod(sc2_ref), mod(sh2_ref)).astype(_BF)
        acc[...] = jnp.zeros(acc.shape, _F32)

    a = jnp.maximum(_dot(h2[...], wu_ref[0]), 0.0)
    acc[...] += _dot((a * a).astype(_BF), wd_ref[0])

    @pl.when(f == pl.num_programs(1) - 1)
    def _():
        y = xmid[...] + mod(gt2_ref) * acc[...]
        if final_norm:
            ms = jnp.mean(y * y, axis=-1, keepdims=True)
            y = y * lax.rsqrt(ms + EPS) * gf_ref[...]
        o_ref[...] = y


def _mlp_call(layer, x, merged, mod, row0, g2, w_out, w_up, w_down, g_final, *, tm, tf, seq_len, final_norm):
    m_rows = x.shape[0]
    pieces = (_GT1, _SC2, _SH2, _GT2)
    if seq_len >= tm:
        tiles_per_seq = seq_len // tm
        mod_reps = None
        mod_specs = [_mod_row_spec(layer, p, row0, lambda i, f: i // tiles_per_seq) for p in pieces]
    else:
        mod_reps = seq_len
        mod_specs = [pl.BlockSpec((1, 1, tm // seq_len, D_MODEL), functools.partial(lambda p, i, f: (layer, p, 0, 0), p))
                     for p in pieces]
    kern = functools.partial(_mlp_kernel, mod_reps=mod_reps, final_norm=final_norm)
    return pl.pallas_call(
        kern,
        grid=(m_rows // tm, D_FF // tf),
        in_specs=[
            pl.BlockSpec((tm, D_MODEL), lambda i, f: (i, 0)),
            pl.BlockSpec((tm, D_MODEL), lambda i, f: (i, 0)),
            *mod_specs,
            pl.BlockSpec((1, 1, D_MODEL), lambda i, f: (layer, 0, 0)),
            pl.BlockSpec((1, D_MODEL, D_MODEL), lambda i, f: (layer, 0, 0)),
            pl.BlockSpec((1, D_MODEL, tf), lambda i, f: (layer, 0, f)),
            pl.BlockSpec((1, tf, D_MODEL), lambda i, f: (layer, f, 0)),
            pl.BlockSpec((1, D_MODEL), lambda i, f: (0, 0)),
        ],
        out_specs=pl.BlockSpec((tm, D_MODEL), lambda i, f: (i, 0)),
        out_shape=jax.ShapeDtypeStruct((m_rows, D_MODEL), _F32),
        scratch_shapes=[
            pltpu.VMEM((tm, D_MODEL), _F32),
            pltpu.VMEM((tm, D_MODEL), _BF),
            pltpu.VMEM((tm, D_MODEL), _F32),
        ],
        compiler_params=_params(2),
        name="outproj_mlp",
    )(x, merged, mod, mod, mod, mod, g2, w_out, w_up, w_down, g_final)


def kernel(x_prompt, x_sample, state_conv, state_C, state_n, state_m, c_prompt, c_sample,
           w_ada, b_ada, g_norm1, g_norm2, w_in, b_gate, conv_w, conv_b, hn_g, w_out, w_up, w_down, g_final):
    n_p, seq_p, _ = x_prompt.shape
    n_s, seq_s, _ = x_sample.shape
    rows_s = n_s * seq_s
    tm_p = 1024

    w_t = jnp.swapaxes(w_in, 1, 2).astype(_BF)
    w_tail = w_t[:, _OFF_GA:]
    zrows = jnp.zeros((DEPTH, GATE_F_COL - N_HEADS, D_MODEL), _BF)
    w_gate = jnp.concatenate([w_t[:, _OFF_IG:_OFF_IG + N_HEADS], zrows,
                              w_t[:, _OFF_IG + N_HEADS:_OFF_GA], zrows], axis=1)
    w_out_b = w_out.astype(_BF)
    w_up_b = w_up.astype(_BF)
    w_down_b = w_down.astype(_BF)
    zb = jnp.zeros((DEPTH, 4), _F32)
    bgate = jnp.broadcast_to(
        jnp.concatenate([b_gate[:, :N_HEADS], zb, b_gate[:, N_HEADS:], zb], axis=-1)[:, :, None],
        (DEPTH, BIAS_ROWS, CHUNK))
    zb = jnp.zeros((DEPTH, GATE_F_COL - N_HEADS), _F32)
    bias_row = jnp.concatenate([b_gate[:, :N_HEADS], zb, b_gate[:, N_HEADS:], zb], axis=-1)[:, None, :]
    g1 = g_norm1.reshape(DEPTH, 1, D_MODEL)
    g2 = g_norm2.reshape(DEPTH, 1, D_MODEL)
    hng = hn_g.reshape(DEPTH, 1, D_MODEL)
    cb = conv_b.reshape(DEPTH, 1, D_MODEL)
    gfin = g_final.reshape(1, D_MODEL)

    mod = _ada_call(jnp.concatenate([c_sample, c_prompt], axis=0), w_ada, b_ada)
    mod_rows = mod.reshape(DEPTH, N_MOD, n_s + n_p, 1, D_MODEL)

    xp = x_prompt.reshape(n_p * seq_p, D_MODEL)
    xs = x_sample.transpose(1, 0, 2).reshape(rows_s, D_MODEL)
    conv_s_in = state_conv.transpose(0, 2, 1, 3)
    n_s_in = state_n.reshape(DEPTH, n_s, N_HEADS * DK)
    m_s_in = jnp.pad(state_m, ((0, 0), (0, 0), (0, CHUNK - N_HEADS)))

    zeros_conv = jnp.zeros((n_p, CONV_W - 1, D_MODEL), _F32)
    zeros_c = jnp.zeros((n_p, N_HEADS, DK, DV), _F32)
    zeros_n = jnp.zeros((n_p, N_HEADS, DK), _F32)
    zeros_m = jnp.zeros((n_p, 8, CHUNK), _F32)

    p_conv, p_n, p_m, s_conv, s_n, s_m = [], [], [], [], [], []
    p_c = s_c = None
    for l in range(DEPTH):
        final = l == DEPTH - 1

        mid, gates_t, nconv = _inproj_prompt_call(
            l, xp, mod_rows, n_s, g1, w_t, w_tail, w_gate, bgate, conv_w, cb, zeros_conv, seq_len=seq_p, tm=tm_p)
        merged, p_c, n1, m1 = _mlstm_prompt_call(
            l, mid, gates_t, hng, zeros_c, zeros_n, zeros_m, p_c, seq_len=seq_p, tb=2048)
        xp = _mlp_call(l, xp, merged, mod_rows, n_s, g2, w_out_b, w_up_b, w_down_b, gfin,
                       tm=512, tf=2048, seq_len=seq_p, final_norm=final)
        tiles_per_seq = seq_p // tm_p
        p_conv.append(nconv[tiles_per_seq - 1::tiles_per_seq]); p_n.append(n1); p_m.append(m1[:, :N_HEADS, 0])

        mid, gates, nconv = _inproj_sample_call(
            l, xs, mod, g1, w_t, w_tail, w_gate, conv_w, cb, conv_s_in, seq_len=seq_s)
        merged3, s_c, n1, m1 = _mlstm_sample_call(
            l, mid.reshape(seq_s, n_s, N_HEADS * HEAD_COLS), gates.reshape(seq_s, n_s, GATE_COLS), bias_row, hng,
            state_C, n_s_in, m_s_in, s_c, bb=16)
        xs = _mlp_call(l, xs, merged3.reshape(rows_s, D_MODEL), mod, 0, g2, w_out_b, w_up_b, w_down_b, gfin,
                       tm=rows_s, tf=1024, seq_len=seq_s, final_norm=final)
        s_conv.append(nconv.transpose(1, 0, 2))
        s_n.append(n1.reshape(n_s, N_HEADS, DK)); s_m.append(m1[:, :N_HEADS])

    return (xp.reshape(n_p, seq_p, D_MODEL), xs.reshape(seq_s, n_s, D_MODEL).transpose(1, 0, 2),
            jnp.stack(p_conv), p_c, jnp.stack(p_n), jnp.stack(p_m),
            jnp.stack(s_conv), s_c, jnp.stack(s_n), jnp.stack(s_m))
```

```python
import functools

import jax
import jax.numpy as jnp
from jax import lax
from jax.experimental import pallas as pl
from jax.experimental.pallas import tpu as pltpu

D_MODEL = 1024
N_HEADS = 4
DK = 128
DV = 256
D_FF = 4096
DEPTH = 4
CONV_W = 3
GATE_CAP = 15.0
EPS = 1e-6
CHUNK = 128
HEAD_COLS = 1024
GATE_COLS = 256
GATE_F_COL = 128
BIAS_ROWS = 16
GATE_ROWS = 24
EXT_COLS = DV + CHUNK
VMEM_LIMIT = 56 * 1024 * 1024

_OFF_XC, _OFF_BG, _OFF_CG = 0, 1024, 2048
_OFF_Q, _OFF_K, _OFF_V, _OFF_O = 3072, 3584, 4096, 5120
_OFF_IG, _OFF_GA, _OFF_GB = 6144, 6152, 7176

_M_YA, _M_Q, _M_K, _M_V, _M_OG = 0, 256, 384, 512, 768

_BF = jnp.bfloat16
_F32 = jnp.float32


def _dot(a, b):
    return jnp.dot(a, b, preferred_element_type=_F32)


def _dot_nt(a, b):
    return lax.dot_general(a, b, (((1,), (1,)), ((), ())), preferred_element_type=_F32)


def _sigmoid(x):
    return 0.5 * jnp.tanh(0.5 * x) + 0.5


def _rms_mod(x, g, sc, sh):
    ms = jnp.mean(x * x, axis=-1, keepdims=True)
    return (x * lax.rsqrt(ms + EPS)) * (g * (1.0 + sc)) + sh


def _rep_rows(v, reps):
    return v if reps == 1 else jnp.concatenate([v] * reps, axis=0)


def _params(n_axes):
    return pltpu.CompilerParams(dimension_semantics=("arbitrary",) * n_axes, vmem_limit_bytes=VMEM_LIMIT)


N_MOD = 6
_SH1, _SC1, _GT1, _SH2, _SC2, _GT2 = range(N_MOD)


def _ada_kernel(c_ref, w_ref, b_ref, o_ref):
    c = c_ref[...]
    a = (c * _sigmoid(c)).astype(_BF)
    o_ref[0, 0] = _dot(a, w_ref[0].astype(_BF)) + b_ref[0]


def _ada_call(c_all, w_ada, b_ada):
    n_rows = c_all.shape[0]
    return pl.pallas_call(
        _ada_kernel,
        grid=(DEPTH, N_MOD),
        in_specs=[
            pl.BlockSpec((n_rows, D_MODEL), lambda l, j: (0, 0)),
            pl.BlockSpec((1, D_MODEL, D_MODEL), lambda l, j: (l, 0, j)),
            pl.BlockSpec((1, 1, D_MODEL), lambda l, j: (l, 0, j)),
        ],
        out_specs=pl.BlockSpec((1, 1, n_rows, D_MODEL), lambda l, j: (l, j, 0, 0)),
        out_shape=jax.ShapeDtypeStruct((DEPTH, N_MOD, n_rows, D_MODEL), _F32),
        compiler_params=_params(2),
        name="ada_mod",
    )(c_all, w_ada, b_ada.reshape(DEPTH, 1, N_MOD * D_MODEL))


def _w_in_specs(layer, idx):
    def spec(width, offset):
        return pl.BlockSpec((1, width, D_MODEL), lambda *g: (layer, offset // width + idx(*g), 0))
    return [spec(256, _OFF_XC), spec(256, _OFF_BG), spec(256, _OFF_CG), spec(128, _OFF_Q), spec(128, _OFF_K),
            spec(256, _OFF_V), spec(256, _OFF_O), spec(256, 0), spec(256, _OFF_GB - _OFF_GA)]


def _scan_lanes(x, combine, identity, seg_len):
    pos = lax.broadcasted_iota(jnp.int32, x.shape, 1) % seg_len
    shift = 1
    while shift < seg_len:
        x = combine(x, jnp.where(pos >= shift, pltpu.roll(x, shift, 1), identity))
        shift *= 2
    return x


def _softcap(a):
    return GATE_CAP * jnp.tanh(a / GATE_CAP)


def _log_sigmoid(x):
    return -(jnp.maximum(-x, 0.0) + jnp.log(1.0 + jnp.exp(-jnp.abs(x))))


def _gate_rows_prompt(gates, bias, gt_ref, first_chunk):
    n_chunks = gates.shape[0] // CHUNK
    gates_t = gates.T

    def stack(r0):
        return jnp.concatenate([gates_t[r0:r0 + 8, c * CHUNK:(c + 1) * CHUNK] for c in range(n_chunks)], axis=0)

    i_pre = _softcap(stack(0) + jnp.concatenate([bias[0:8]] * n_chunks, axis=0))
    f_pre = _softcap(stack(GATE_F_COL) + jnp.concatenate([bias[8:16]] * n_chunks, axis=0))
    b = _scan_lanes(_log_sigmoid(f_pre), jnp.add, 0.0, CHUNK)
    g = i_pre - b
    cm = _scan_lanes(g, jnp.maximum, -3e38, CHUNK)
    for c in range(n_chunks):
        gt_ref[first_chunk + c, 0:8] = b[c * 8:(c + 1) * 8]
        gt_ref[first_chunk + c, 8:16] = g[c * 8:(c + 1) * 8]
        gt_ref[first_chunk + c, 16:24] = cm[c * 8:(c + 1) * 8]


def _inproj_project(h, w_refs):
    wxc, wbg, wcg, wq, wk, wv, wo, wga, wgb = w_refs
    xc = _dot_nt(h, wxc[0])
    cg = _dot_nt(h, wcg[0])
    qk = _dot_nt(h, jnp.concatenate([wq[0], wk[0]], axis=0))
    return dict(u=cg * xc, bg=_dot_nt(h, wbg[0]), ga=_dot_nt(h, wga[0]), qk=qk,
                v=_dot_nt(h, wv[0]), o=_dot_nt(h, wo[0]), gb=_dot_nt(h, wgb[0]))


def _inproj_epilogue(p, conv, mid_ref, out_dtype, r0=0):
    rows = slice(r0, r0 + conv.shape[0])
    mid_ref[rows, _M_YA:_M_YA + 256] = (_sigmoid(p["ga"]) * (p["bg"] * conv)).astype(out_dtype)
    mid_ref[rows, _M_Q:_M_Q + DK] = p["qk"][:, :DK].astype(out_dtype)
    mid_ref[rows, _M_K:_M_K + DK] = (p["qk"][:, DK:] * (DK ** -0.5)).astype(out_dtype)
    mid_ref[rows, _M_V:_M_V + 256] = p["v"].astype(out_dtype)
    mid_ref[rows, _M_OG:_M_OG + 256] = (_sigmoid(p["o"]) * _sigmoid(p["gb"])).astype(out_dtype)


_RAW_COLS = 8 * 256


def _inproj_prompt_kernel(x_ref, sc_ref, sh_ref, g_ref, wxc, wbg, wcg, wq, wk, wv, wo, wga, wgb, wg_ref,
                          bg_ref, cw_ref, cb_ref, cprev_ref, mid_ref, gt_ref, nconv_ref,
                          h_scr, gate_scr, raw_even, raw_odd, ubuf, carry, *, tiles_per_seq, n_work):
    j = pl.program_id(0)
    tm = x_ref.shape[0]
    cur = jnp.minimum(j, n_work - 1)
    s = cur % N_HEADS
    prev = jnp.maximum(j - 1, 0)
    tile_p = prev // N_HEADS
    s_p = prev % N_HEADS
    chunks_per_step = (tm // CHUNK) // N_HEADS
    raw = (raw_even, raw_odd)

    @pl.when(j == 0)
    def _():
        raw_odd[...] = jnp.zeros(raw_odd.shape, _F32)
        carry[...] = jnp.zeros(carry.shape, _F32)

    @pl.when(jnp.logical_and(s == 0, j < n_work))
    def _():
        h = _rms_mod(x_ref[...], g_ref[0], sc_ref[0, 0, 0], sh_ref[0, 0, 0]).astype(_BF)
        h_scr[...] = h
        gate_scr[...] = _dot_nt(h, wg_ref[0])

    def step(slot):
        rows0 = pl.multiple_of(s * (chunks_per_step * CHUNK), CHUNK)
        _gate_rows_prompt(gate_scr[pl.ds(rows0, chunks_per_step * CHUNK), :], bg_ref[0], gt_ref,
                          s * chunks_per_step)

        h = h_scr[...]
        weights = (wxc[0], wcg[0], wbg[0], wga[0], jnp.concatenate([wq[0], wk[0]], axis=0),
                   wv[0], wo[0], wgb[0])
        seq_start = (tile_p % tiles_per_seq) == 0
        ubuf[6:8, :] = jnp.where(seq_start, cprev_ref[0], carry[s_p])
        cw = cw_ref[0]
        conv_bias = cb_ref[0]
        rows_per_part = tm // len(weights)
        rc = min(CHUNK // 2, rows_per_part)
        for idx, w in enumerate(weights):
            raw[slot][:, idx * 256:(idx + 1) * 256] = _dot_nt(h, w)
            for r0 in range(idx * rows_per_part, (idx + 1) * rows_per_part, rc):
                def piece(k):
                    return raw[1 - slot][r0:r0 + rc, k * 256:(k + 1) * 256]

                u = piece(1) * piece(0)
                ubuf[8 + r0:8 + r0 + rc, :] = u
                conv = (conv_bias + cw[0:1] * ubuf[6 + r0:6 + r0 + rc, :]
                        + cw[1:2] * ubuf[7 + r0:7 + r0 + rc, :] + cw[2:3] * u)
                _inproj_epilogue(dict(bg=piece(2), ga=piece(3), qk=piece(4), v=piece(5), o=piece(6),
                                      gb=piece(7)), conv, mid_ref, _BF, r0)
        last2 = ubuf[6 + tm:8 + tm, :]
        carry[s_p] = last2
        nconv_ref[0] = last2

    for parity in range(2):
        pl.when(j % 2 == parity)(functools.partial(step, parity))


def _mod_row_spec(layer, piece, row0, seq_of):
    return pl.BlockSpec((1, 1, 1, 1, D_MODEL), lambda *g: (layer, piece, row0 + seq_of(*g), 0, 0))


def _inproj_prompt_call(layer, x, mod_rows, row0, g, w_t, w_tail, w_gate, bgate, conv_w, conv_b, conv_prev,
                        *, seq_len, tm):
    m_rows = x.shape[0]
    n_tiles = m_rows // tm
    tiles_per_seq = seq_len // tm
    n_work = n_tiles * N_HEADS

    def cur(j):
        return jnp.minimum(j, n_work - 1)

    def prev(j):
        return jnp.maximum(j - 1, 0)

    kern = functools.partial(_inproj_prompt_kernel, tiles_per_seq=tiles_per_seq, n_work=n_work)
    w_specs = _w_in_specs(layer, lambda j: cur(j) % N_HEADS)
    seq_of_cur = lambda j: cur(j) // N_HEADS // tiles_per_seq
    return pl.pallas_call(
        kern,
        grid=(n_work + 1,),
        in_specs=[
            pl.BlockSpec((tm, D_MODEL), lambda j: (cur(j) // N_HEADS, 0)),
            _mod_row_spec(layer, _SC1, row0, seq_of_cur),
            _mod_row_spec(layer, _SH1, row0, seq_of_cur),
            pl.BlockSpec((1, 1, D_MODEL), lambda j: (layer, 0, 0)),
            *w_specs,
            pl.BlockSpec((1, GATE_COLS, D_MODEL), lambda j: (layer, 0, 0)),
            pl.BlockSpec((1, BIAS_ROWS, CHUNK), lambda j: (layer, 0, 0)),
            pl.BlockSpec((1, CONV_W, 256), lambda j: (layer, 0, prev(j) % N_HEADS)),
            pl.BlockSpec((1, 1, 256), lambda j: (layer, 0, prev(j) % N_HEADS)),
            pl.BlockSpec((1, CONV_W - 1, 256),
                         lambda j: (prev(j) // N_HEADS // tiles_per_seq, 0, prev(j) % N_HEADS)),
        ],
        out_specs=[
            pl.BlockSpec((tm, HEAD_COLS), lambda j: (prev(j) // N_HEADS, prev(j) % N_HEADS)),
            pl.BlockSpec((tm // CHUNK, GATE_ROWS, CHUNK), lambda j: (cur(j) // N_HEADS, 0, 0)),
            pl.BlockSpec((1, CONV_W - 1, 256), lambda j: (prev(j) // N_HEADS, 0, prev(j) % N_HEADS)),
        ],
        out_shape=[
            jax.ShapeDtypeStruct((m_rows, N_HEADS * HEAD_COLS), _BF),
            jax.ShapeDtypeStruct((m_rows // CHUNK, GATE_ROWS, CHUNK), _F32),
            jax.ShapeDtypeStruct((n_tiles, CONV_W - 1, D_MODEL), _F32),
        ],
        scratch_shapes=[
            pltpu.VMEM((tm, D_MODEL), _BF),
            pltpu.VMEM((tm, GATE_COLS), _F32),
            pltpu.VMEM((tm, _RAW_COLS), _F32),
            pltpu.VMEM((tm, _RAW_COLS), _F32),
            pltpu.VMEM((tm + 8, 256), _F32),
            pltpu.VMEM((N_HEADS, CONV_W - 1, 256), _F32),
        ],
        compiler_params=_params(1),
        name="inproj_prompt",
    )(x, mod_rows, mod_rows, g, *([w_t] * 7), *([w_tail] * 2), w_gate, bgate, conv_w, conv_b, conv_prev)


def _inproj_sample_kernel(x_ref, sc_ref, sh_ref, g_ref, wxc, wbg, wcg, wq, wk, wv, wo, wga, wgb, wg_ref,
                          cw_ref, cb_ref, cprev_ref, mid_ref, gt_ref, nconv_ref, h_scr, *, seq_len):
    s = pl.program_id(0)
    n_b = sc_ref.shape[2]

    @pl.when(s == 0)
    def _():
        h = _rms_mod(x_ref[...], g_ref[0], _rep_rows(sc_ref[0, 0], seq_len), _rep_rows(sh_ref[0, 0], seq_len))
        h = h.astype(_BF)
        h_scr[...] = h
        gt_ref[...] = _dot_nt(h, wg_ref[0])

    p = _inproj_project(h_scr[...], (wxc, wbg, wcg, wq, wk, wv, wo, wga, wgb))
    u = p["u"]
    prev0 = cprev_ref[0, 0]
    prev1 = cprev_ref[0, 1]
    p1 = jnp.concatenate([prev1, u[0:(seq_len - 1) * n_b]], axis=0)
    p2 = jnp.concatenate([prev0, prev1, u[0:(seq_len - 2) * n_b]], axis=0)
    cw = cw_ref[0]
    conv = cb_ref[0] + cw[0:1] * p2 + cw[1:2] * p1 + cw[2:3] * u
    nconv_ref[0] = u[(seq_len - 2) * n_b:(seq_len - 1) * n_b]
    nconv_ref[1] = u[(seq_len - 1) * n_b:seq_len * n_b]
    _inproj_epilogue(p, conv, mid_ref, _F32)


def _inproj_sample_call(layer, x, mod, g, w_t, w_tail, w_gate, conv_w, conv_b, conv_prev, *, seq_len):
    tm = x.shape[0]
    n_b = tm // seq_len
    kern = functools.partial(_inproj_sample_kernel, seq_len=seq_len)
    w_specs = _w_in_specs(layer, lambda s: s)
    return pl.pallas_call(
        kern,
        grid=(N_HEADS,),
        in_specs=[
            pl.BlockSpec((tm, D_MODEL), lambda s: (0, 0)),
            pl.BlockSpec((1, 1, n_b, D_MODEL), lambda s: (layer, _SC1, 0, 0)),
            pl.BlockSpec((1, 1, n_b, D_MODEL), lambda s: (layer, _SH1, 0, 0)),
            pl.BlockSpec((1, 1, D_MODEL), lambda s: (layer, 0, 0)),
            *w_specs,
            pl.BlockSpec((1, GATE_COLS, D_MODEL), lambda s: (layer, 0, 0)),
            pl.BlockSpec((1, CONV_W, 256), lambda s: (layer, 0, s)),
            pl.BlockSpec((1, 1, 256), lambda s: (layer, 0, s)),
            pl.BlockSpec((1, CONV_W - 1, n_b, 256), lambda s: (layer, 0, 0, s)),
        ],
        out_specs=[
            pl.BlockSpec((tm, HEAD_COLS), lambda s: (0, s)),
            pl.BlockSpec((tm, GATE_COLS), lambda s: (0, 0)),
            pl.BlockSpec((CONV_W - 1, n_b, 256), lambda s: (0, 0, s)),
        ],
        out_shape=[
            jax.ShapeDtypeStruct((tm, N_HEADS * HEAD_COLS), _F32),
            jax.ShapeDtypeStruct((tm, GATE_COLS), _F32),
            jax.ShapeDtypeStruct((CONV_W - 1, n_b, D_MODEL), _F32),
        ],
        scratch_shapes=[pltpu.VMEM((tm, D_MODEL), _BF)],
        compiler_params=_params(1),
        name="inproj_sample",
    )(x, mod, mod, g, *([w_t] * 7), *([w_tail] * 2), w_gate, conv_w, conv_b, conv_prev)


def _causal_mask(lq):
    row_id = lax.broadcasted_iota(jnp.int32, (lq, CHUNK), 0)
    col_id = lax.broadcasted_iota(jnp.int32, (lq, CHUNK), 1)
    return col_id <= row_id


def _chunk_scores(a, causal):
    dmat = jnp.where(causal, jnp.exp(a["g_row"] - a["m_col"]), 0.0)
    return (_dot_nt(a["q"], a["k"]) * dmat).astype(_BF)


def _chunk_output(a, s_bf, cx_ref, h, hng_h):
    q_inter = (a["q"].astype(_F32) * a["inter_col"]).astype(_BF)
    cx = cx_ref[h]
    rhs = jnp.concatenate([a["v_ext"], cx.astype(_BF)], axis=0)
    res = _dot(jnp.concatenate([s_bf, q_inter], axis=1), rhs)
    kw_t = a["k"].T.astype(_F32) * a["ws_row"]
    cx_ref[h] = a["dec"] * cx + _dot(kw_t.astype(_BF), a["v_ext"])
    num = res[:, :DV]
    rden = 1.0 / jnp.maximum(jnp.abs(res[:, DV:]), a["em_col"])
    sq = jnp.sum(num * num, axis=-1, keepdims=True) * (1.0 / DV)
    scale = rden * lax.rsqrt(rden * rden * sq + EPS)
    hn = num * jnp.concatenate([scale, scale], axis=1) * hng_h
    return a["ya"] + a["og"] * hn


def _mlstm_prompt_kernel(mid_ref, gt_ref, hng_ref, c0_ref, n0_ref, m0_ref, cbuf_ref,
                         mg_ref, c_ref, n_ref, m_ref, cx_ref, cols_scr, rows_scr):
    del cbuf_ref
    t = pl.program_id(1)

    @pl.when(t == 0)
    def _():
        m_ref[...] = m0_ref[...]
        for h in range(N_HEADS):
            cx_ref[h, :, 0:DV] = c0_ref[0, h]
            cx_ref[h, :, DV:EXT_COLS] = jnp.broadcast_to(n0_ref[0, h:h + 1, :], (CHUNK, DK)).T

    n_chunks = mid_ref.shape[0] // CHUNK
    rows = n_chunks * 8
    hng = hng_ref[0]
    gates = gt_ref[...]
    b = gates[:, 0:8, :].reshape(rows, CHUNK)
    g = gates[:, 8:16, :].reshape(rows, CHUNK)
    cm = gates[:, 16:24, :].reshape(rows, CHUNK)
    b_last = jnp.broadcast_to(b[:, CHUNK - 1:CHUNK], (rows, CHUNK))
    cm_last = jnp.broadcast_to(cm[:, CHUNK - 1:CHUNK], (rows, CHUNK))

    m_prev = m_ref[0]
    m_prevs = []
    for c in range(n_chunks):
        m_prevs.append(m_prev)
        m_prev = b_last[c * 8:(c + 1) * 8] + jnp.maximum(cm_last[c * 8:(c + 1) * 8], m_prev)
    m_ref[0] = m_prev
    m_prev_all = jnp.concatenate(m_prevs, axis=0)

    m_run = jnp.maximum(cm, m_prev_all)
    inter = jnp.exp(m_prev_all - m_run)
    em = jnp.exp(-(b + m_run))
    m_last = jnp.maximum(cm_last, m_prev_all)
    ws = jnp.exp(g - m_last)
    decay = jnp.exp(m_prev_all - m_last)

    pad = jnp.zeros((CHUNK - 24, CHUNK), _F32)
    for c in range(n_chunks):
        sl = slice(c * 8, (c + 1) * 8)
        cols_scr[c] = jnp.concatenate([m_run[sl], inter[sl], em[sl], pad], axis=0).T
        rows_scr[c, 0:8] = g[sl]
        rows_scr[c, 8:16] = decay[sl]
        rows_scr[c, 16:24] = ws[sl]

    causal = _causal_mask(CHUNK)
    ones = jnp.ones((CHUNK, CHUNK), _BF)

    def chunk_body(c, carry):
        cols = cols_scr[c]
        vec = rows_scr[c]
        rows_c = pl.ds(pl.multiple_of(c * CHUNK, CHUNK), CHUNK)
        heads = []
        for h in range(N_HEADS):
            mb = h * HEAD_COLS
            v = mid_ref[rows_c, mb + _M_V:mb + _M_V + 256]
            heads.append(dict(
                q=mid_ref[rows_c, mb + _M_Q:mb + _M_Q + 128],
                k=mid_ref[rows_c, mb + _M_K:mb + _M_K + 128],
                v_ext=jnp.concatenate([v, ones], axis=1),
                ya=mid_ref[rows_c, mb + _M_YA:mb + _M_YA + 256].astype(_F32),
                og=mid_ref[rows_c, mb + _M_OG:mb + _M_OG + 256].astype(_F32),
                g_row=vec[h:h + 1, :],
                m_col=cols[:, h:h + 1],
                inter_col=cols[:, 8 + h:9 + h],
                em_col=cols[:, 16 + h:17 + h],
                ws_row=vec[16 + h:17 + h, :],
                dec=vec[8 + h:9 + h, 0:1]))
        outs = []
        ahead = 2
        scores = [_chunk_scores(heads[h], causal) for h in range(ahead)]
        for h, a in enumerate(heads):
            if h + ahead < N_HEADS:
                scores.append(_chunk_scores(heads[h + ahead], causal))
            outs.append(_chunk_output(a, scores[h], cx_ref, h, hng[:, h * DV:(h + 1) * DV]))
        mg_ref[rows_c, :] = jnp.concatenate(outs, axis=1).astype(_BF)
        return carry

    lax.fori_loop(0, n_chunks, chunk_body, 0)

    @pl.when(t == pl.num_programs(1) - 1)
    def _():
        for h in range(N_HEADS):
            c_ref[0, 0, h] = cx_ref[h, :, 0:DV]
            n_ref[0, h:h + 1, :] = cx_ref[h, :, DV:EXT_COLS].T[0:1, :]


def _mlstm_prompt_call(layer, mid, gates_t, hng, c0, n0, m0, c_buf, *, seq_len, tb):
    m_rows = mid.shape[0]
    n_seq = m_rows // seq_len
    steps = seq_len // tb
    aliases = {} if c_buf is None else {6: 1}
    if c_buf is None:
        c_buf = jnp.zeros((1,), _F32)
    return pl.pallas_call(
        _mlstm_prompt_kernel,
        grid=(n_seq, steps),
        in_specs=[
            pl.BlockSpec((tb, N_HEADS * HEAD_COLS), lambda b, t: (b * steps + t, 0)),
            pl.BlockSpec((tb // CHUNK, GATE_ROWS, CHUNK), lambda b, t: (b * steps + t, 0, 0)),
            pl.BlockSpec((1, 1, D_MODEL), lambda b, t: (layer, 0, 0)),
            pl.BlockSpec((1, N_HEADS, DK, DV), lambda b, t: (b, 0, 0, 0)),
            pl.BlockSpec((1, N_HEADS, DK), lambda b, t: (b, 0, 0)),
            pl.BlockSpec((1, 8, CHUNK), lambda b, t: (b, 0, 0)),
            pl.BlockSpec(memory_space=pl.ANY),
        ],
        out_specs=[
            pl.BlockSpec((tb, D_MODEL), lambda b, t: (b * steps + t, 0)),
            pl.BlockSpec((1, 1, N_HEADS, DK, DV), lambda b, t: (layer, b, 0, 0, 0)),
            pl.BlockSpec((1, N_HEADS, DK), lambda b, t: (b, 0, 0)),
            pl.BlockSpec((1, 8, CHUNK), lambda b, t: (b, 0, 0)),
        ],
        out_shape=[
            jax.ShapeDtypeStruct((m_rows, D_MODEL), _BF),
            jax.ShapeDtypeStruct((DEPTH, n_seq, N_HEADS, DK, DV), _F32),
            jax.ShapeDtypeStruct((n_seq, N_HEADS, DK), _F32),
            jax.ShapeDtypeStruct((n_seq, 8, CHUNK), _F32),
        ],
        scratch_shapes=[
            pltpu.VMEM((N_HEADS, DK, EXT_COLS), _F32),
            pltpu.VMEM((tb // CHUNK, CHUNK, CHUNK), _F32),
            pltpu.VMEM((tb // CHUNK, 24, CHUNK), _F32),
        ],
        input_output_aliases=aliases,
        compiler_params=_params(2),
        name="mlstm_prompt",
    )(mid, gates_t, hng, c0, n0, m0, c_buf)


def _mlstm_sample_kernel(mid_ref, gt_ref, bias_ref, hng_ref, c0_ref, n0_ref, m0_ref, cbuf_ref,
                         mg_ref, c_ref, n_ref, m_ref,
                         qs_scr, kk_scr, wv_scr, rs_scr, dec_scr, lhs_q, lhs_k, lhs_wv):
    del cbuf_ref
    seq_len, bb, _ = mid_ref.shape
    hng = hng_ref[0]
    bias = bias_ref[0]
    m_prev = m0_ref[0]

    b_t, g_t, cm_t = [], [], []
    for t in range(seq_len):
        pre = _softcap(gt_ref[t] + bias)
        logf = _log_sigmoid(pre[:, GATE_F_COL:GATE_F_COL + CHUNK])
        b_t.append(logf if t == 0 else b_t[-1] + logf)
        g_t.append(pre[:, 0:CHUNK] - b_t[-1])
        cm_t.append(g_t[-1] if t == 0 else jnp.maximum(cm_t[-1], g_t[-1]))
    m_run = [jnp.maximum(cm, m_prev) for cm in cm_t]
    inter = [jnp.exp(m_prev - mr) for mr in m_run]
    em = [jnp.exp(-(b + mr)) for b, mr in zip(b_t, m_run)]
    m_last = m_run[-1]
    ws = [jnp.exp(g - m_last) for g in g_t]
    decay = jnp.exp(m_prev - m_last)
    m_ref[...] = b_t[-1] + m_last
    dec_scr[...] = decay

    lhs_q[...] = jnp.zeros(lhs_q.shape, _F32)
    lhs_k[...] = jnp.zeros(lhs_k.shape, _F32)
    lhs_wv[...] = jnp.zeros(lhs_wv.shape, _F32)
    n_all = n0_ref[0]
    den_part = {}
    pairs = [(t, s) for t in range(seq_len) for s in range(t + 1)]
    for h in range(N_HEADS):
        def col(x):
            return x[:, h:h + 1]
        mb = h * HEAD_COLS
        q = [mid_ref[t, :, mb + _M_Q:mb + _M_Q + DK] for t in range(seq_len)]
        k = [mid_ref[t, :, mb + _M_K:mb + _M_K + DK] for t in range(seq_len)]
        v = [mid_ref[t, :, mb + _M_V:mb + _M_V + DV] for t in range(seq_len)]
        n_h = n_all[:, h * DK:(h + 1) * DK]
        qs = [q[t] * col(inter[t]) for t in range(seq_len)]
        n_new = col(decay) * n_h
        for t in range(seq_len):
            qs_scr[t, :, h * DK:(h + 1) * DK] = qs[t]
            kk_scr[t, :, h * DK:(h + 1) * DK] = k[t]
            wv_scr[t, :, h * DV:(h + 1) * DV] = col(ws[t]) * v[t]
            n_new = n_new + col(ws[t]) * k[t]
        n_ref[:, h * DK:(h + 1) * DK] = n_new
        stacked = jnp.concatenate([q[t] * k[s] for t, s in pairs] + [qs[t] * n_h for t in range(seq_len)], axis=0)
        dots = jnp.sum(stacked, axis=-1, keepdims=True)
        decays = jnp.exp(jnp.concatenate([col(g_t[s]) - col(m_run[t]) for t, s in pairs], axis=0))
        weights = dots[:len(pairs) * bb] * decays
        for t in range(seq_len):
            den = dots[(len(pairs) + t) * bb:(len(pairs) + t + 1) * bb]
            num = None
            for idx, (t2, s) in enumerate(pairs):
                if t2 == t:
                    w = weights[idx * bb:(idx + 1) * bb]
                    num = w * v[s] if num is None else num + w * v[s]
                    den = den + w
            mg_ref[t, :, h * DV:(h + 1) * DV] = num
            den_part[h, t] = den

    n_par = lhs_q.shape[0] // 8

    def body(ii, carry):
        for u in range(n_par):
            i = ii * n_par + u
            rows = slice(8 * u, 8 * u + 8)
            for t in range(seq_len):
                lhs_q[8 * u + t:8 * u + t + 1, :] = qs_scr[t, pl.ds(i, 1), :]
                lhs_k[8 * u + t:8 * u + t + 1, :] = kk_scr[t, pl.ds(i, 1), :]
                lhs_wv[8 * u + t:8 * u + t + 1, :] = wv_scr[t, pl.ds(i, 1), :]
            for h in range(N_HEADS):
                c_old = c0_ref[0, i, h]
                r = _dot(lhs_q[rows, h * DK:(h + 1) * DK].astype(_BF), c_old.astype(_BF))
                for t in range(seq_len):
                    rs_scr[t, pl.ds(i, 1), h * DV:(h + 1) * DV] = r[t:t + 1, :]
                d_c = lax.dot_general(lhs_k[rows, h * DK:(h + 1) * DK].astype(_BF),
                                      lhs_wv[rows, h * DV:(h + 1) * DV].astype(_BF),
                                      (((0,), (0,)), ((), ())), preferred_element_type=_F32)
                c_ref[0, i, h] = dec_scr[pl.ds(i, 1), h:h + 1] * c_old + d_c
        return carry

    lax.fori_loop(0, bb // n_par, body, 0)

    blocks = [(h, t) for h in range(N_HEADS) for t in range(seq_len)]
    dens = jnp.concatenate([jnp.maximum(jnp.abs(den_part[h, t]), em[t][:, h:h + 1]) for h, t in blocks], axis=0)
    hraw = jnp.concatenate([mg_ref[t, :, h * DV:(h + 1) * DV] + rs_scr[t, :, h * DV:(h + 1) * DV]
                            for h, t in blocks], axis=0) * (1.0 / dens)
    ms = jnp.mean(hraw * hraw, axis=-1, keepdims=True)
    hn_all = hraw * lax.rsqrt(ms + EPS)
    for idx, (h, t) in enumerate(blocks):
        mb = h * HEAD_COLS
        hn = hn_all[idx * bb:(idx + 1) * bb] * hng[:, h * DV:(h + 1) * DV]
        ya = mid_ref[t, :, mb + _M_YA:mb + _M_YA + DV]
        og = mid_ref[t, :, mb + _M_OG:mb + _M_OG + DV]
        mg_ref[t, :, h * DV:(h + 1) * DV] = ya + og * hn


def _mlstm_sample_call(layer, mid3, gates3, bias, hng, c0, n0, m0, c_buf, *, bb):
    seq_len, n_seq, _ = mid3.shape
    aliases = {} if c_buf is None else {7: 1}
    if c_buf is None:
        c_buf = jnp.zeros((1,), _F32)
    return pl.pallas_call(
        _mlstm_sample_kernel,
        grid=(n_seq // bb,),
        in_specs=[
            pl.BlockSpec((seq_len, bb, N_HEADS * HEAD_COLS), lambda i: (0, i, 0)),
            pl.BlockSpec((seq_len, bb, GATE_COLS), lambda i: (0, i, 0)),
            pl.BlockSpec((1, 1, GATE_COLS), lambda i: (layer, 0, 0)),
            pl.BlockSpec((1, 1, D_MODEL), lambda i: (layer, 0, 0)),
            pl.BlockSpec((1, bb, N_HEADS, DK, DV), lambda i: (layer, i, 0, 0, 0)),
            pl.BlockSpec((1, bb, N_HEADS * DK), lambda i: (layer, i, 0)),
            pl.BlockSpec((1, bb, CHUNK), lambda i: (layer, i, 0)),
            pl.BlockSpec(memory_space=pl.ANY),
        ],
        out_specs=[
            pl.BlockSpec((seq_len, bb, D_MODEL), lambda i: (0, i, 0)),
            pl.BlockSpec((1, bb, N_HEADS, DK, DV), lambda i: (layer, i, 0, 0, 0)),
            pl.BlockSpec((bb, N_HEADS * DK), lambda i: (i, 0)),
            pl.BlockSpec((bb, CHUNK), lambda i: (i, 0)),
        ],
        out_shape=[
            jax.ShapeDtypeStruct((seq_len, n_seq, D_MODEL), _F32),
            jax.ShapeDtypeStruct((DEPTH, n_seq, N_HEADS, DK, DV), _F32),
            jax.ShapeDtypeStruct((n_seq, N_HEADS * DK), _F32),
            jax.ShapeDtypeStruct((n_seq, CHUNK), _F32),
        ],
        scratch_shapes=[
            pltpu.VMEM((seq_len, bb, N_HEADS * DK), _F32),
            pltpu.VMEM((seq_len, bb, N_HEADS * DK), _F32),
            pltpu.VMEM((seq_len, bb, N_HEADS * DV), _F32),
            pltpu.VMEM((seq_len, bb, N_HEADS * DV), _F32),
            pltpu.VMEM((bb, CHUNK), _F32),
            pltpu.VMEM((32, N_HEADS * DK), _F32),
            pltpu.VMEM((32, N_HEADS * DK), _F32),
            pltpu.VMEM((32, N_HEADS * DV), _F32),
        ],
        input_output_aliases=aliases,
        compiler_params=_params(1),
        name="mlstm_sample",
    )(mid3, gates3, bias, hng, c0, n0, m0, c_buf)


def _mlp_kernel(x_ref, mg_ref, gt1_ref, sc2_ref, sh2_ref, gt2_ref, g2_ref, wo_ref, wu_ref, wd_ref, gf_ref,
                o_ref, xmid, h2, acc, *, mod_reps, final_norm, n_f, lead):
    i = pl.program_id(0)
    f = pl.program_id(1)

    def mod(ref):
        return ref[0, 0, 0] if mod_reps is None else _rep_rows(ref[0, 0], mod_reps)

    def prologue(slot):
        mix = _dot(mg_ref[...].astype(_BF), wo_ref[0])
        xm = x_ref[...] + mod(gt1_ref) * mix
        xmid[slot] = xm
        h2[slot] = _rms_mod(xm, g2_ref[0], mod(sc2_ref), mod(sh2_ref)).astype(_BF)

    def mlp_step(slot):
        a = jnp.maximum(_dot(h2[slot], wu_ref[0]), 0.0)
        acc[...] += _dot((a * a).astype(_BF), wd_ref[0])

    def finalize(slot):
        y = xmid[slot] + mod(gt2_ref) * acc[...]
        if final_norm:
            ms = jnp.mean(y * y, axis=-1, keepdims=True)
            y = y * lax.rsqrt(ms + EPS) * gf_ref[...]
        o_ref[...] = y

    if not lead:
        @pl.when(f == 0)
        def _():
            prologue(0)
            acc[...] = jnp.zeros(acc.shape, _F32)

        mlp_step(0)
        pl.when(f == n_f - 1)(functools.partial(finalize, 0))
        return

    @pl.when(jnp.logical_and(i == 0, f == 0))
    def _():
        prologue(0)
        acc[...] = jnp.zeros(acc.shape, _F32)

    def last_step(slot):
        prologue(1 - slot)
        mlp_step(slot)
        finalize(slot)
        acc[...] = jnp.zeros(acc.shape, _F32)

    for slot in range(2):
        mine = (i % 2) == slot
        pl.when(jnp.logical_and(mine, f < n_f - 1))(functools.partial(mlp_step, slot))
        pl.when(jnp.logical_and(mine, f == n_f - 1))(functools.partial(last_step, slot))


def _mlp_call(layer, x, merged, mod, row0, g2, w_out, w_up, w_down, g_final, *, tm, tf, seq_len, final_norm):
    m_rows = x.shape[0]
    n_tiles = m_rows // tm
    n_f = D_FF // tf
    lead = n_tiles > 1

    def lead_tile(i, f):
        return jnp.minimum(i + (f + 1) // n_f, n_tiles - 1) if lead else i

    pieces = (_GT1, _SC2, _SH2, _GT2)
    if seq_len >= tm:
        tiles_per_seq = seq_len // tm
        mod_reps = None
        mod_specs = [_mod_row_spec(layer, p, row0, lambda i, f: lead_tile(i, f) // tiles_per_seq)
                     for p in pieces[:3]]
        mod_specs.append(_mod_row_spec(layer, _GT2, row0, lambda i, f: i // tiles_per_seq))
    else:
        mod_reps = seq_len
        mod_specs = [pl.BlockSpec((1, 1, tm // seq_len, D_MODEL), functools.partial(lambda p, i, f: (layer, p, 0, 0), p))
                     for p in pieces]
    kern = functools.partial(_mlp_kernel, mod_reps=mod_reps, final_norm=final_norm, n_f=n_f, lead=lead)
    return pl.pallas_call(
        kern,
        grid=(n_tiles, n_f),
        in_specs=[
            pl.BlockSpec((tm, D_MODEL), lambda i, f: (lead_tile(i, f), 0)),
            pl.BlockSpec((tm, D_MODEL), lambda i, f: (lead_tile(i, f), 0)),
            *mod_specs,
            pl.BlockSpec((1, 1, D_MODEL), lambda i, f: (layer, 0, 0)),
            pl.BlockSpec((1, D_MODEL, D_MODEL), lambda i, f: (layer, 0, 0)),
            pl.BlockSpec((1, D_MODEL, tf), lambda i, f: (layer, 0, f)),
            pl.BlockSpec((1, tf, D_MODEL), lambda i, f: (layer, f, 0)),
            pl.BlockSpec((1, D_MODEL), lambda i, f: (0, 0)),
        ],
        out_specs=pl.BlockSpec((tm, D_MODEL), lambda i, f: (i, 0)),
        out_shape=jax.ShapeDtypeStruct((m_rows, D_MODEL), _F32),
        scratch_shapes=[
            pltpu.VMEM((2, tm, D_MODEL), _F32),
            pltpu.VMEM((2, tm, D_MODEL), _BF),
            pltpu.VMEM((tm, D_MODEL), _F32),
        ],
        compiler_params=_params(2),
        name="outproj_mlp",
    )(x, merged, mod, mod, mod, mod, g2, w_out, w_up, w_down, g_final)


def kernel(x_prompt, x_sample, state_conv, state_C, state_n, state_m, c_prompt, c_sample,
           w_ada, b_ada, g_norm1, g_norm2, w_in, b_gate, conv_w, conv_b, hn_g, w_out, w_up, w_down, g_final):
    n_p, seq_p, _ = x_prompt.shape
    n_s, seq_s, _ = x_sample.shape
    rows_s = n_s * seq_s
    tm_p = 1024

    w_t = jnp.swapaxes(w_in, 1, 2).astype(_BF)
    w_tail = w_t[:, _OFF_GA:]
    zrows = jnp.zeros((DEPTH, GATE_F_COL - N_HEADS, D_MODEL), _BF)
    w_gate = jnp.concatenate([w_t[:, _OFF_IG:_OFF_IG + N_HEADS], zrows,
                              w_t[:, _OFF_IG + N_HEADS:_OFF_GA], zrows], axis=1)
    w_out_b = w_out.astype(_BF)
    w_up_b = w_up.astype(_BF)
    w_down_b = w_down.astype(_BF)
    zb = jnp.zeros((DEPTH, 4), _F32)
    bgate = jnp.broadcast_to(
        jnp.concatenate([b_gate[:, :N_HEADS], zb, b_gate[:, N_HEADS:], zb], axis=-1)[:, :, None],
        (DEPTH, BIAS_ROWS, CHUNK))
    zb = jnp.zeros((DEPTH, GATE_F_COL - N_HEADS), _F32)
    bias_row = jnp.concatenate([b_gate[:, :N_HEADS], zb, b_gate[:, N_HEADS:], zb], axis=-1)[:, None, :]
    g1 = g_norm1.reshape(DEPTH, 1, D_MODEL)
    g2 = g_norm2.reshape(DEPTH, 1, D_MODEL)
    hng = hn_g.reshape(DEPTH, 1, D_MODEL)
    cb = conv_b.reshape(DEPTH, 1, D_MODEL)
    gfin = g_final.reshape(1, D_MODEL)

    mod = _ada_call(jnp.concatenate([c_sample, c_prompt], axis=0), w_ada, b_ada)
    mod_rows = mod.reshape(DEPTH, N_MOD, n_s + n_p, 1, D_MODEL)

    xp = x_prompt.reshape(n_p * seq_p, D_MODEL)
    xs = x_sample.transpose(1, 0, 2).reshape(rows_s, D_MODEL)
    conv_s_in = state_conv.transpose(0, 2, 1, 3)
    n_s_in = state_n.reshape(DEPTH, n_s, N_HEADS * DK)
    m_s_in = jnp.pad(state_m, ((0, 0), (0, 0), (0, CHUNK - N_HEADS)))

    zeros_conv = jnp.zeros((n_p, CONV_W - 1, D_MODEL), _F32)
    zeros_c = jnp.zeros((n_p, N_HEADS, DK, DV), _F32)
    zeros_n = jnp.zeros((n_p, N_HEADS, DK), _F32)
    zeros_m = jnp.zeros((n_p, 8, CHUNK), _F32)

    p_conv, p_n, p_m, s_conv, s_n, s_m = [], [], [], [], [], []
    p_c = s_c = None
    for l in range(DEPTH):
        final = l == DEPTH - 1

        mid, gates_t, nconv = _inproj_prompt_call(
            l, xp, mod_rows, n_s, g1, w_t, w_tail, w_gate, bgate, conv_w, cb, zeros_conv, seq_len=seq_p, tm=tm_p)
        merged, p_c, n1, m1 = _mlstm_prompt_call(
            l, mid, gates_t, hng, zeros_c, zeros_n, zeros_m, p_c, seq_len=seq_p, tb=2048)
        xp = _mlp_call(l, xp, merged, mod_rows, n_s, g2, w_out_b, w_up_b, w_down_b, gfin,
                       tm=512, tf=2048, seq_len=seq_p, final_norm=final)
        tiles_per_seq = seq_p // tm_p
        p_conv.append(nconv[tiles_per_seq - 1::tiles_per_seq]); p_n.append(n1); p_m.append(m1[:, :N_HEADS, 0])

        mid, gates, nconv = _inproj_sample_call(
            l, xs, mod, g1, w_t, w_tail, w_gate, conv_w, cb, conv_s_in, seq_len=seq_s)
        merged3, s_c, n1, m1 = _mlstm_sample_call(
            l, mid.reshape(seq_s, n_s, N_HEADS * HEAD_COLS), gates.reshape(seq_s, n_s, GATE_COLS), bias_row, hng,
            state_C, n_s_in, m_s_in, s_c, bb=16)
        xs = _mlp_call(l, xs, merged3.reshape(rows_s, D_MODEL), mod, 0, g2, w_out_b, w_up_b, w_down_b, gfin,
                       tm=rows_s, tf=1024, seq_len=seq_s, final_norm=final)
        s_conv.append(nconv.transpose(1, 0, 2))
        s_n.append(n1.reshape(n_s, N_HEADS, DK)); s_m.append(m1[:, :N_HEADS])

    return (xp.reshape(n_p, seq_p, D_MODEL), xs.reshape(seq_s, n_s, D_MODEL).transpose(1, 0, 2),
            jnp.stack(p_conv), p_c, jnp.stack(p_n), jnp.stack(p_m),
            jnp.stack(s_conv), s_c, jnp.stack(s_n), jnp.stack(s_m))
```

```python
import functools

import jax
import jax.numpy as jnp
from jax import lax
from jax.experimental import pallas as pl
from jax.experimental.pallas import tpu as pltpu

D_MODEL = 1024
N_HEADS = 4
DK = 128
DV = 256
D_FF = 4096
DEPTH = 4
CONV_W = 3
GATE_CAP = 15.0
EPS = 1e-6
CHUNK = 128
HEAD_COLS = 1024
GATE_COLS = 256
GATE_F_COL = 128
BIAS_ROWS = 16
GATE_ROWS = 24
EXT_COLS = DV + CHUNK
VMEM_LIMIT = 56 * 1024 * 1024

_OFF_XC, _OFF_BG, _OFF_CG = 0, 1024, 2048
_OFF_Q, _OFF_K, _OFF_V, _OFF_O = 3072, 3584, 4096, 5120
_OFF_IG, _OFF_GA, _OFF_GB = 6144, 6152, 7176

_M_YA, _M_Q, _M_K, _M_V, _M_OG = 0, 256, 384, 512, 768

_BF = jnp.bfloat16
_F32 = jnp.float32


def _dot(a, b):
    return jnp.dot(a, b, preferred_element_type=_F32)


def _dot_nt(a, b):
    return lax.dot_general(a, b, (((1,), (1,)), ((), ())), preferred_element_type=_F32)


def _sigmoid(x):
    return 0.5 * jnp.tanh(0.5 * x) + 0.5


def _rms_mod(x, g, sc, sh):
    ms = jnp.mean(x * x, axis=-1, keepdims=True)
    return (x * lax.rsqrt(ms + EPS)) * (g * (1.0 + sc)) + sh


def _rep_rows(v, reps):
    return v if reps == 1 else jnp.concatenate([v] * reps, axis=0)


def _params(n_axes):
    return pltpu.CompilerParams(dimension_semantics=("arbitrary",) * n_axes, vmem_limit_bytes=VMEM_LIMIT)


N_MOD = 6
_SH1, _SC1, _GT1, _SH2, _SC2, _GT2 = range(N_MOD)


def _ada_kernel(c_ref, w_ref, b_ref, o_ref):
    c = c_ref[...]
    a = (c * _sigmoid(c)).astype(_BF)
    o_ref[0, 0] = _dot(a, w_ref[0].astype(_BF)) + b_ref[0]


def _ada_call(c_all, w_ada, b_ada):
    n_rows = c_all.shape[0]
    return pl.pallas_call(
        _ada_kernel,
        grid=(DEPTH, N_MOD),
        in_specs=[
            pl.BlockSpec((n_rows, D_MODEL), lambda l, j: (0, 0)),
            pl.BlockSpec((1, D_MODEL, D_MODEL), lambda l, j: (l, 0, j)),
            pl.BlockSpec((1, 1, D_MODEL), lambda l, j: (l, 0, j)),
        ],
        out_specs=pl.BlockSpec((1, 1, n_rows, D_MODEL), lambda l, j: (l, j, 0, 0)),
        out_shape=jax.ShapeDtypeStruct((DEPTH, N_MOD, n_rows, D_MODEL), _F32),
        compiler_params=_params(2),
        name="ada_mod",
    )(c_all, w_ada, b_ada.reshape(DEPTH, 1, N_MOD * D_MODEL))


def _w_in_specs(layer, idx):
    def spec(width, offset):
        return pl.BlockSpec((1, width, D_MODEL), lambda *g: (layer, offset // width + idx(*g), 0))
    return [spec(256, _OFF_XC), spec(256, _OFF_BG), spec(256, _OFF_CG), spec(128, _OFF_Q), spec(128, _OFF_K),
            spec(256, _OFF_V), spec(256, _OFF_O), spec(256, 0), spec(256, _OFF_GB - _OFF_GA)]


def _scan_lanes(x, combine, identity, seg_len):
    pos = lax.broadcasted_iota(jnp.int32, x.shape, 1) % seg_len
    shift = 1
    while shift < seg_len:
        x = combine(x, jnp.where(pos >= shift, pltpu.roll(x, shift, 1), identity))
        shift *= 2
    return x


def _softcap(a):
    return GATE_CAP * jnp.tanh(a / GATE_CAP)


def _log_sigmoid(x):
    return -(jnp.maximum(-x, 0.0) + jnp.log(1.0 + jnp.exp(-jnp.abs(x))))


def _gate_rows_prompt(gates, bias, gt_ref, first_chunk):
    n_chunks = gates.shape[0] // CHUNK
    gates_t = gates.T

    def stack(r0):
        return jnp.concatenate([gates_t[r0:r0 + 8, c * CHUNK:(c + 1) * CHUNK] for c in range(n_chunks)], axis=0)

    i_pre = _softcap(stack(0) + jnp.concatenate([bias[0:8]] * n_chunks, axis=0))
    f_pre = _softcap(stack(GATE_F_COL) + jnp.concatenate([bias[8:16]] * n_chunks, axis=0))
    b = _scan_lanes(_log_sigmoid(f_pre), jnp.add, 0.0, CHUNK)
    g = i_pre - b
    cm = _scan_lanes(g, jnp.maximum, -3e38, CHUNK)
    for c in range(n_chunks):
        gt_ref[first_chunk + c, 0:8] = b[c * 8:(c + 1) * 8]
        gt_ref[first_chunk + c, 8:16] = g[c * 8:(c + 1) * 8]
        gt_ref[first_chunk + c, 16:24] = cm[c * 8:(c + 1) * 8]


def _inproj_project(h, w_refs):
    wxc, wbg, wcg, wq, wk, wv, wo, wga, wgb = w_refs
    xc = _dot_nt(h, wxc[0])
    cg = _dot_nt(h, wcg[0])
    qk = _dot_nt(h, jnp.concatenate([wq[0], wk[0]], axis=0))
    return dict(u=cg * xc, bg=_dot_nt(h, wbg[0]), ga=_dot_nt(h, wga[0]), qk=qk,
                v=_dot_nt(h, wv[0]), o=_dot_nt(h, wo[0]), gb=_dot_nt(h, wgb[0]))


def _inproj_epilogue(p, conv, mid_ref, out_dtype, r0=0):
    rows = slice(r0, r0 + conv.shape[0])
    mid_ref[rows, _M_YA:_M_YA + 256] = (_sigmoid(p["ga"]) * (p["bg"] * conv)).astype(out_dtype)
    mid_ref[rows, _M_Q:_M_Q + DK] = p["qk"][:, :DK].astype(out_dtype)
    mid_ref[rows, _M_K:_M_K + DK] = (p["qk"][:, DK:] * (DK ** -0.5)).astype(out_dtype)
    mid_ref[rows, _M_V:_M_V + 256] = p["v"].astype(out_dtype)
    mid_ref[rows, _M_OG:_M_OG + 256] = (_sigmoid(p["o"]) * _sigmoid(p["gb"])).astype(out_dtype)


_RAW_COLS = 8 * 256


def _inproj_prompt_kernel(x_ref, sc_ref, sh_ref, g_ref, wxc, wbg, wcg, wq, wk, wv, wo, wga, wgb, wg_ref,
                          bg_ref, cw_ref, cb_ref, cprev_ref, mid_ref, gt_ref, nconv_ref,
                          h_scr, gate_scr, raw_even, raw_odd, ubuf, carry, *, tiles_per_seq, n_work):
    j = pl.program_id(0)
    tm = x_ref.shape[0]
    cur = jnp.minimum(j, n_work - 1)
    s = cur % N_HEADS
    prev = jnp.maximum(j - 1, 0)
    tile_p = prev // N_HEADS
    s_p = prev % N_HEADS
    chunks_per_step = (tm // CHUNK) // N_HEADS
    raw = (raw_even, raw_odd)

    @pl.when(j == 0)
    def _():
        raw_odd[...] = jnp.zeros(raw_odd.shape, _F32)
        carry[...] = jnp.zeros(carry.shape, _F32)

    @pl.when(jnp.logical_and(s == 0, j < n_work))
    def _():
        h = _rms_mod(x_ref[...], g_ref[0], sc_ref[0, 0, 0], sh_ref[0, 0, 0]).astype(_BF)
        h_scr[...] = h
        gate_scr[...] = _dot_nt(h, wg_ref[0])

    def step(slot):
        rows0 = pl.multiple_of(s * (chunks_per_step * CHUNK), CHUNK)
        _gate_rows_prompt(gate_scr[pl.ds(rows0, chunks_per_step * CHUNK), :], bg_ref[0], gt_ref,
                          s * chunks_per_step)

        h = h_scr[...]
        weights = (wxc[0], wcg[0], wbg[0], wga[0], jnp.concatenate([wq[0], wk[0]], axis=0),
                   wv[0], wo[0], wgb[0])
        seq_start = (tile_p % tiles_per_seq) == 0
        ubuf[6:8, :] = jnp.where(seq_start, cprev_ref[0], carry[s_p])
        cw = cw_ref[0]
        conv_bias = cb_ref[0]
        rows_per_part = tm // len(weights)
        rc = min(CHUNK // 2, rows_per_part)
        for idx, w in enumerate(weights):
            raw[slot][:, idx * 256:(idx + 1) * 256] = _dot_nt(h, w)
            for r0 in range(idx * rows_per_part, (idx + 1) * rows_per_part, rc):
                def piece(k):
                    return raw[1 - slot][r0:r0 + rc, k * 256:(k + 1) * 256]

                u = piece(1) * piece(0)
                ubuf[8 + r0:8 + r0 + rc, :] = u
                conv = (conv_bias + cw[0:1] * ubuf[6 + r0:6 + r0 + rc, :]
                        + cw[1:2] * ubuf[7 + r0:7 + r0 + rc, :] + cw[2:3] * u)
                _inproj_epilogue(dict(bg=piece(2), ga=piece(3), qk=piece(4), v=piece(5), o=piece(6),
                                      gb=piece(7)), conv, mid_ref, _BF, r0)
        last2 = ubuf[6 + tm:8 + tm, :]
        carry[s_p] = last2
        nconv_ref[0] = last2

    for parity in range(2):
        pl.when(j % 2 == parity)(functools.partial(step, parity))


def _mod_row_spec(layer, piece, row0, seq_of):
    return pl.BlockSpec((1, 1, 1, 1, D_MODEL), lambda *g: (layer, piece, row0 + seq_of(*g), 0, 0))


def _inproj_prompt_call(layer, x, mod_rows, row0, g, w_t, w_tail, w_gate, bgate, conv_w, conv_b, conv_prev,
                        *, seq_len, tm):
    m_rows = x.shape[0]
    n_tiles = m_rows // tm
    tiles_per_seq = seq_len // tm
    n_work = n_tiles * N_HEADS

    def cur(j):
        return jnp.minimum(j, n_work - 1)

    def prev(j):
        return jnp.maximum(j - 1, 0)

    kern = functools.partial(_inproj_prompt_kernel, tiles_per_seq=tiles_per_seq, n_work=n_work)
    w_specs = _w_in_specs(layer, lambda j: cur(j) % N_HEADS)
    seq_of_cur = lambda j: cur(j) // N_HEADS // tiles_per_seq
    return pl.pallas_call(
        kern,
        grid=(n_work + 1,),
        in_specs=[
            pl.BlockSpec((tm, D_MODEL), lambda j: (cur(j) // N_HEADS, 0)),
            _mod_row_spec(layer, _SC1, row0, seq_of_cur),
            _mod_row_spec(layer, _SH1, row0, seq_of_cur),
            pl.BlockSpec((1, 1, D_MODEL), lambda j: (layer, 0, 0)),
            *w_specs,
            pl.BlockSpec((1, GATE_COLS, D_MODEL), lambda j: (layer, 0, 0)),
            pl.BlockSpec((1, BIAS_ROWS, CHUNK), lambda j: (layer, 0, 0)),
            pl.BlockSpec((1, CONV_W, 256), lambda j: (layer, 0, prev(j) % N_HEADS)),
            pl.BlockSpec((1, 1, 256), lambda j: (layer, 0, prev(j) % N_HEADS)),
            pl.BlockSpec((1, CONV_W - 1, 256),
                         lambda j: (prev(j) // N_HEADS // tiles_per_seq, 0, prev(j) % N_HEADS)),
        ],
        out_specs=[
            pl.BlockSpec((tm, HEAD_COLS), lambda j: (prev(j) // N_HEADS, prev(j) % N_HEADS)),
            pl.BlockSpec((tm // CHUNK, GATE_ROWS, CHUNK), lambda j: (cur(j) // N_HEADS, 0, 0)),
            pl.BlockSpec((1, CONV_W - 1, 256), lambda j: (prev(j) // N_HEADS, 0, prev(j) % N_HEADS)),
        ],
        out_shape=[
            jax.ShapeDtypeStruct((m_rows, N_HEADS * HEAD_COLS), _BF),
            jax.ShapeDtypeStruct((m_rows // CHUNK, GATE_ROWS, CHUNK), _F32),
            jax.ShapeDtypeStruct((n_tiles, CONV_W - 1, D_MODEL), _F32),
        ],
        scratch_shapes=[
            pltpu.VMEM((tm, D_MODEL), _BF),
            pltpu.VMEM((tm, GATE_COLS), _F32),
            pltpu.VMEM((tm, _RAW_COLS), _F32),
            pltpu.VMEM((tm, _RAW_COLS), _F32),
            pltpu.VMEM((tm + 8, 256), _F32),
            pltpu.VMEM((N_HEADS, CONV_W - 1, 256), _F32),
        ],
        compiler_params=_params(1),
        name="inproj_prompt",
    )(x, mod_rows, mod_rows, g, *([w_t] * 7), *([w_tail] * 2), w_gate, bgate, conv_w, conv_b, conv_prev)


def _inproj_sample_kernel(x_ref, sc_ref, sh_ref, g_ref, wxc, wbg, wcg, wq, wk, wv, wo, wga, wgb, wg_ref,
                          cw_ref, cb_ref, cprev_ref, mid_ref, gt_ref, nconv_ref, h_scr, *, seq_len):
    s = pl.program_id(0)
    n_b = sc_ref.shape[2]

    @pl.when(s == 0)
    def _():
        h = _rms_mod(x_ref[...], g_ref[0], _rep_rows(sc_ref[0, 0], seq_len), _rep_rows(sh_ref[0, 0], seq_len))
        h = h.astype(_BF)
        h_scr[...] = h
        gt_ref[...] = _dot_nt(h, wg_ref[0])

    p = _inproj_project(h_scr[...], (wxc, wbg, wcg, wq, wk, wv, wo, wga, wgb))
    u = p["u"]
    prev0 = cprev_ref[0, 0]
    prev1 = cprev_ref[0, 1]
    p1 = jnp.concatenate([prev1, u[0:(seq_len - 1) * n_b]], axis=0)
    p2 = jnp.concatenate([prev0, prev1, u[0:(seq_len - 2) * n_b]], axis=0)
    cw = cw_ref[0]
    conv = cb_ref[0] + cw[0:1] * p2 + cw[1:2] * p1 + cw[2:3] * u
    nconv_ref[0] = u[(seq_len - 2) * n_b:(seq_len - 1) * n_b]
    nconv_ref[1] = u[(seq_len - 1) * n_b:seq_len * n_b]
    _inproj_epilogue(p, conv, mid_ref, _F32)


def _inproj_sample_call(layer, x, mod, g, w_t, w_tail, w_gate, conv_w, conv_b, conv_prev, *, seq_len):
    tm = x.shape[0]
    n_b = tm // seq_len
    kern = functools.partial(_inproj_sample_kernel, seq_len=seq_len)
    w_specs = _w_in_specs(layer, lambda s: s)
    return pl.pallas_call(
        kern,
        grid=(N_HEADS,),
        in_specs=[
            pl.BlockSpec((tm, D_MODEL), lambda s: (0, 0)),
            pl.BlockSpec((1, 1, n_b, D_MODEL), lambda s: (layer, _SC1, 0, 0)),
            pl.BlockSpec((1, 1, n_b, D_MODEL), lambda s: (layer, _SH1, 0, 0)),
            pl.BlockSpec((1, 1, D_MODEL), lambda s: (layer, 0, 0)),
            *w_specs,
            pl.BlockSpec((1, GATE_COLS, D_MODEL), lambda s: (layer, 0, 0)),
            pl.BlockSpec((1, CONV_W, 256), lambda s: (layer, 0, s)),
            pl.BlockSpec((1, 1, 256), lambda s: (layer, 0, s)),
            pl.BlockSpec((1, CONV_W - 1, n_b, 256), lambda s: (layer, 0, 0, s)),
        ],
        out_specs=[
            pl.BlockSpec((tm, HEAD_COLS), lambda s: (0, s)),
            pl.BlockSpec((tm, GATE_COLS), lambda s: (0, 0)),
            pl.BlockSpec((CONV_W - 1, n_b, 256), lambda s: (0, 0, s)),
        ],
        out_shape=[
            jax.ShapeDtypeStruct((tm, N_HEADS * HEAD_COLS), _F32),
            jax.ShapeDtypeStruct((tm, GATE_COLS), _F32),
            jax.ShapeDtypeStruct((CONV_W - 1, n_b, D_MODEL), _F32),
        ],
        scratch_shapes=[pltpu.VMEM((tm, D_MODEL), _BF)],
        compiler_params=_params(1),
        name="inproj_sample",
    )(x, mod, mod, g, *([w_t] * 7), *([w_tail] * 2), w_gate, conv_w, conv_b, conv_prev)


def _causal_mask(lq):
    row_id = lax.broadcasted_iota(jnp.int32, (lq, CHUNK), 0)
    col_id = lax.broadcasted_iota(jnp.int32, (lq, CHUNK), 1)
    return col_id <= row_id


def _chunk_scores(a, causal):
    dmat = jnp.where(causal, jnp.exp(a["g_row"] - a["m_col"]), 0.0)
    return (_dot_nt(a["q"], a["k"]) * dmat).astype(_BF)


def _chunk_output(a, s_bf, cx_ref, h, hng_h):
    q_inter = (a["q"].astype(_F32) * a["inter_col"]).astype(_BF)
    cx = cx_ref[h]
    rhs = jnp.concatenate([a["v_ext"], cx.astype(_BF)], axis=0)
    res = _dot(jnp.concatenate([s_bf, q_inter], axis=1), rhs)
    kw_t = a["k"].T.astype(_F32) * a["ws_row"]
    cx_ref[h] = a["dec"] * cx + _dot(kw_t.astype(_BF), a["v_ext"])
    num = res[:, :DV]
    rden = 1.0 / jnp.maximum(jnp.abs(res[:, DV:]), a["em_col"])
    sq = jnp.sum(num * num, axis=-1, keepdims=True) * (1.0 / DV)
    scale = rden * lax.rsqrt(rden * rden * sq + EPS)
    hn = num * jnp.concatenate([scale, scale], axis=1) * hng_h
    return a["ya"] + a["og"] * hn


def _mlstm_prompt_kernel(mid_ref, gt_ref, hng_ref, c0_ref, n0_ref, m0_ref, cbuf_ref,
                         mg_ref, c_ref, n_ref, m_ref, cx_ref, cols_scr, rows_scr):
    del cbuf_ref
    t = pl.program_id(1)

    @pl.when(t == 0)
    def _():
        m_ref[...] = m0_ref[...]
        for h in range(N_HEADS):
            cx_ref[h, :, 0:DV] = c0_ref[0, h]
            cx_ref[h, :, DV:EXT_COLS] = jnp.broadcast_to(n0_ref[0, h:h + 1, :], (CHUNK, DK)).T

    n_chunks = mid_ref.shape[0] // CHUNK
    rows = n_chunks * 8
    hng = hng_ref[0]
    gates = gt_ref[...]
    b = gates[:, 0:8, :].reshape(rows, CHUNK)
    g = gates[:, 8:16, :].reshape(rows, CHUNK)
    cm = gates[:, 16:24, :].reshape(rows, CHUNK)
    b_last = jnp.broadcast_to(b[:, CHUNK - 1:CHUNK], (rows, CHUNK))
    cm_last = jnp.broadcast_to(cm[:, CHUNK - 1:CHUNK], (rows, CHUNK))

    m_prev = m_ref[0]
    m_prevs = []
    for c in range(n_chunks):
        m_prevs.append(m_prev)
        m_prev = b_last[c * 8:(c + 1) * 8] + jnp.maximum(cm_last[c * 8:(c + 1) * 8], m_prev)
    m_ref[0] = m_prev
    m_prev_all = jnp.concatenate(m_prevs, axis=0)

    m_run = jnp.maximum(cm, m_prev_all)
    inter = jnp.exp(m_prev_all - m_run)
    em = jnp.exp(-(b + m_run))
    m_last = jnp.maximum(cm_last, m_prev_all)
    ws = jnp.exp(g - m_last)
    decay = jnp.exp(m_prev_all - m_last)

    pad = jnp.zeros((CHUNK - 24, CHUNK), _F32)
    for c in range(n_chunks):
        sl = slice(c * 8, (c + 1) * 8)
        cols_scr[c] = jnp.concatenate([m_run[sl], inter[sl], em[sl], pad], axis=0).T
        rows_scr[c, 0:8] = g[sl]
        rows_scr[c, 8:16] = decay[sl]
        rows_scr[c, 16:24] = ws[sl]

    causal = _causal_mask(CHUNK)
    ones = jnp.ones((CHUNK, CHUNK), _BF)

    def chunk_body(c, carry):
        cols = cols_scr[c]
        vec = rows_scr[c]
        rows_c = pl.ds(pl.multiple_of(c * CHUNK, CHUNK), CHUNK)
        heads = []
        for h in range(N_HEADS):
            mb = h * HEAD_COLS
            v = mid_ref[rows_c, mb + _M_V:mb + _M_V + 256]
            heads.append(dict(
                q=mid_ref[rows_c, mb + _M_Q:mb + _M_Q + 128],
                k=mid_ref[rows_c, mb + _M_K:mb + _M_K + 128],
                v_ext=jnp.concatenate([v, ones], axis=1),
                ya=mid_ref[rows_c, mb + _M_YA:mb + _M_YA + 256].astype(_F32),
                og=mid_ref[rows_c, mb + _M_OG:mb + _M_OG + 256].astype(_F32),
                g_row=vec[h:h + 1, :],
                m_col=cols[:, h:h + 1],
                inter_col=cols[:, 8 + h:9 + h],
                em_col=cols[:, 16 + h:17 + h],
                ws_row=vec[16 + h:17 + h, :],
                dec=vec[8 + h:9 + h, 0:1]))
        outs = []
        ahead = 2
        scores = [_chunk_scores(heads[h], causal) for h in range(ahead)]
        for h, a in enumerate(heads):
            if h + ahead < N_HEADS:
                scores.append(_chunk_scores(heads[h + ahead], causal))
            outs.append(_chunk_output(a, scores[h], cx_ref, h, hng[:, h * DV:(h + 1) * DV]))
        mg_ref[rows_c, :] = jnp.concatenate(outs, axis=1).astype(_BF)
        return carry

    lax.fori_loop(0, n_chunks, chunk_body, 0)

    @pl.when(t == pl.num_programs(1) - 1)
    def _():
        for h in range(N_HEADS):
            c_ref[0, 0, h] = cx_ref[h, :, 0:DV]
            n_ref[0, h:h + 1, :] = cx_ref[h, :, DV:EXT_COLS].T[0:1, :]


def _mlstm_prompt_call(layer, mid, gates_t, hng, c0, n0, m0, c_buf, *, seq_len, tb):
    m_rows = mid.shape[0]
    n_seq = m_rows // seq_len
    steps = seq_len // tb
    aliases = {} if c_buf is None else {6: 1}
    if c_buf is None:
        c_buf = jnp.zeros((1,), _F32)
    return pl.pallas_call(
        _mlstm_prompt_kernel,
        grid=(n_seq, steps),
        in_specs=[
            pl.BlockSpec((tb, N_HEADS * HEAD_COLS), lambda b, t: (b * steps + t, 0)),
            pl.BlockSpec((tb // CHUNK, GATE_ROWS, CHUNK), lambda b, t: (b * steps + t, 0, 0)),
            pl.BlockSpec((1, 1, D_MODEL), lambda b, t: (layer, 0, 0)),
            pl.BlockSpec((1, N_HEADS, DK, DV), lambda b, t: (b, 0, 0, 0)),
            pl.BlockSpec((1, N_HEADS, DK), lambda b, t: (b, 0, 0)),
            pl.BlockSpec((1, 8, CHUNK), lambda b, t: (b, 0, 0)),
            pl.BlockSpec(memory_space=pl.ANY),
        ],
        out_specs=[
            pl.BlockSpec((tb, D_MODEL), lambda b, t: (b * steps + t, 0)),
            pl.BlockSpec((1, 1, N_HEADS, DK, DV), lambda b, t: (layer, b, 0, 0, 0)),
            pl.BlockSpec((1, N_HEADS, DK), lambda b, t: (b, 0, 0)),
            pl.BlockSpec((1, 8, CHUNK), lambda b, t: (b, 0, 0)),
        ],
        out_shape=[
            jax.ShapeDtypeStruct((m_rows, D_MODEL), _BF),
            jax.ShapeDtypeStruct((DEPTH, n_seq, N_HEADS, DK, DV), _F32),
            jax.ShapeDtypeStruct((n_seq, N_HEADS, DK), _F32),
            jax.ShapeDtypeStruct((n_seq, 8, CHUNK), _F32),
        ],
        scratch_shapes=[
            pltpu.VMEM((N_HEADS, DK, EXT_COLS), _F32),
            pltpu.VMEM((tb // CHUNK, CHUNK, CHUNK), _F32),
            pltpu.VMEM((tb // CHUNK, 24, CHUNK), _F32),
        ],
        input_output_aliases=aliases,
        compiler_params=_params(2),
        name="mlstm_prompt",
    )(mid, gates_t, hng, c0, n0, m0, c_buf)


def _mlstm_sample_kernel(mid_ref, gt_ref, bias_ref, hng_ref, c0_ref, n0_ref, m0_ref, cbuf_ref,
                         mg_ref, c_ref, n_ref, m_ref,
                         qs_scr, kk_scr, wv_scr, rs_scr, dec_scr, lhs_q, lhs_k, lhs_wv):
    del cbuf_ref
    seq_len, bb, _ = mid_ref.shape
    hng = hng_ref[0]
    bias = bias_ref[0]
    m_prev = m0_ref[0]

    b_t, g_t, cm_t = [], [], []
    for t in range(seq_len):
        pre = _softcap(gt_ref[t] + bias)
        logf = _log_sigmoid(pre[:, GATE_F_COL:GATE_F_COL + CHUNK])
        b_t.append(logf if t == 0 else b_t[-1] + logf)
        g_t.append(pre[:, 0:CHUNK] - b_t[-1])
        cm_t.append(g_t[-1] if t == 0 else jnp.maximum(cm_t[-1], g_t[-1]))
    m_run = [jnp.maximum(cm, m_prev) for cm in cm_t]
    inter = [jnp.exp(m_prev - mr) for mr in m_run]
    em = [jnp.exp(-(b + mr)) for b, mr in zip(b_t, m_run)]
    m_last = m_run[-1]
    ws = [jnp.exp(g - m_last) for g in g_t]
    decay = jnp.exp(m_prev - m_last)
    m_ref[...] = b_t[-1] + m_last
    dec_scr[...] = decay

    lhs_q[...] = jnp.zeros(lhs_q.shape, _F32)
    lhs_k[...] = jnp.zeros(lhs_k.shape, _F32)
    lhs_wv[...] = jnp.zeros(lhs_wv.shape, _F32)
    n_all = n0_ref[0]
    den_part = {}
    pairs = [(t, s) for t in range(seq_len) for s in range(t + 1)]
    for h in range(N_HEADS):
        def col(x):
            return x[:, h:h + 1]
        mb = h * HEAD_COLS
        q = [mid_ref[t, :, mb + _M_Q:mb + _M_Q + DK] for t in range(seq_len)]
        k = [mid_ref[t, :, mb + _M_K:mb + _M_K + DK] for t in range(seq_len)]
        v = [mid_ref[t, :, mb + _M_V:mb + _M_V + DV] for t in range(seq_len)]
        n_h = n_all[:, h * DK:(h + 1) * DK]
        qs = [q[t] * col(inter[t]) for t in range(seq_len)]
        n_new = col(decay) * n_h
        for t in range(seq_len):
            qs_scr[t, :, h * DK:(h + 1) * DK] = qs[t]
            kk_scr[t, :, h * DK:(h + 1) * DK] = k[t]
            wv_scr[t, :, h * DV:(h + 1) * DV] = col(ws[t]) * v[t]
            n_new = n_new + col(ws[t]) * k[t]
        n_ref[:, h * DK:(h + 1) * DK] = n_new
        stacked = jnp.concatenate([q[t] * k[s] for t, s in pairs] + [qs[t] * n_h for t in range(seq_len)], axis=0)
        dots = jnp.sum(stacked, axis=-1, keepdims=True)
        decays = jnp.exp(jnp.concatenate([col(g_t[s]) - col(m_run[t]) for t, s in pairs], axis=0))
        weights = dots[:len(pairs) * bb] * decays
        for t in range(seq_len):
            den = dots[(len(pairs) + t) * bb:(len(pairs) + t + 1) * bb]
            num = None
            for idx, (t2, s) in enumerate(pairs):
                if t2 == t:
                    w = weights[idx * bb:(idx + 1) * bb]
                    num = w * v[s] if num is None else num + w * v[s]
                    den = den + w
            mg_ref[t, :, h * DV:(h + 1) * DV] = num
            den_part[h, t] = den

    n_par = lhs_q.shape[0] // 8

    def body(ii, carry):
        for u in range(n_par):
            i = ii * n_par + u
            rows = slice(8 * u, 8 * u + 8)
            for t in range(seq_len):
                lhs_q[8 * u + t:8 * u + t + 1, :] = qs_scr[t, pl.ds(i, 1), :]
                lhs_k[8 * u + t:8 * u + t + 1, :] = kk_scr[t, pl.ds(i, 1), :]
                lhs_wv[8 * u + t:8 * u + t + 1, :] = wv_scr[t, pl.ds(i, 1), :]
            for h in range(N_HEADS):
                c_old = c0_ref[0, i, h]
                r = _dot(lhs_q[rows, h * DK:(h + 1) * DK].astype(_BF), c_old.astype(_BF))
                for t in range(seq_len):
                    rs_scr[t, pl.ds(i, 1), h * DV:(h + 1) * DV] = r[t:t + 1, :]
                d_c = lax.dot_general(lhs_k[rows, h * DK:(h + 1) * DK].astype(_BF),
                                      lhs_wv[rows, h * DV:(h + 1) * DV].astype(_BF),
                                      (((0,), (0,)), ((), ())), preferred_element_type=_F32)
                c_ref[0, i, h] = dec_scr[pl.ds(i, 1), h:h + 1] * c_old + d_c
        return carry

    lax.fori_loop(0, bb // n_par, body, 0)

    blocks = [(h, t) for h in range(N_HEADS) for t in range(seq_len)]
    dens = jnp.concatenate([jnp.maximum(jnp.abs(den_part[h, t]), em[t][:, h:h + 1]) for h, t in blocks], axis=0)
    hraw = jnp.concatenate([mg_ref[t, :, h * DV:(h + 1) * DV] + rs_scr[t, :, h * DV:(h + 1) * DV]
                            for h, t in blocks], axis=0) * (1.0 / dens)
    ms = jnp.mean(hraw * hraw, axis=-1, keepdims=True)
    hn_all = hraw * lax.rsqrt(ms + EPS)
    for idx, (h, t) in enumerate(blocks):
        mb = h * HEAD_COLS
        hn = hn_all[idx * bb:(idx + 1) * bb] * hng[:, h * DV:(h + 1) * DV]
        ya = mid_ref[t, :, mb + _M_YA:mb + _M_YA + DV]
        og = mid_ref[t, :, mb + _M_OG:mb + _M_OG + DV]
        mg_ref[t, :, h * DV:(h + 1) * DV] = ya + og * hn


def _mlstm_sample_call(layer, mid3, gates3, bias, hng, c0, n0, m0, c_buf, *, bb):
    seq_len, n_seq, _ = mid3.shape
    aliases = {} if c_buf is None else {7: 1}
    if c_buf is None:
        c_buf = jnp.zeros((1,), _F32)
    return pl.pallas_call(
        _mlstm_sample_kernel,
        grid=(n_seq // bb,),
        in_specs=[
            pl.BlockSpec((seq_len, bb, N_HEADS * HEAD_COLS), lambda i: (0, i, 0)),
            pl.BlockSpec((seq_len, bb, GATE_COLS), lambda i: (0, i, 0)),
            pl.BlockSpec((1, 1, GATE_COLS), lambda i: (layer, 0, 0)),
            pl.BlockSpec((1, 1, D_MODEL), lambda i: (layer, 0, 0)),
            pl.BlockSpec((1, bb, N_HEADS, DK, DV), lambda i: (layer, i, 0, 0, 0)),
            pl.BlockSpec((1, bb, N_HEADS * DK), lambda i: (layer, i, 0)),
            pl.BlockSpec((1, bb, CHUNK), lambda i: (layer, i, 0)),
            pl.BlockSpec(memory_space=pl.ANY),
        ],
        out_specs=[
            pl.BlockSpec((seq_len, bb, D_MODEL), lambda i: (0, i, 0)),
            pl.BlockSpec((1, bb, N_HEADS, DK, DV), lambda i: (layer, i, 0, 0, 0)),
            pl.BlockSpec((bb, N_HEADS * DK), lambda i: (i, 0)),
            pl.BlockSpec((bb, CHUNK), lambda i: (i, 0)),
        ],
        out_shape=[
            jax.ShapeDtypeStruct((seq_len, n_seq, D_MODEL), _F32),
            jax.ShapeDtypeStruct((DEPTH, n_seq, N_HEADS, DK, DV), _F32),
            jax.ShapeDtypeStruct((n_seq, N_HEADS * DK), _F32),
            jax.ShapeDtypeStruct((n_seq, CHUNK), _F32),
        ],
        scratch_shapes=[
            pltpu.VMEM((seq_len, bb, N_HEADS * DK), _F32),
            pltpu.VMEM((seq_len, bb, N_HEADS * DK), _F32),
            pltpu.VMEM((seq_len, bb, N_HEADS * DV), _F32),
            pltpu.VMEM((seq_len, bb, N_HEADS * DV), _F32),
            pltpu.VMEM((bb, CHUNK), _F32),
            pltpu.VMEM((32, N_HEADS * DK), _F32),
            pltpu.VMEM((32, N_HEADS * DK), _F32),
            pltpu.VMEM((32, N_HEADS * DV), _F32),
        ],
        input_output_aliases=aliases,
        compiler_params=_params(1),
        name="mlstm_sample",
    )(mid3, gates3, bias, hng, c0, n0, m0, c_buf)


def _mlp_kernel(x_ref, mg_ref, gt1_ref, sc2_ref, sh2_ref, gt2_ref, g2_ref, wo_ref, wu_ref, wd_ref, gf_ref,
                o_ref, xmid, h2, acc, *, mod_reps, final_norm):
    f = pl.program_id(1)

    def mod(ref):
        return ref[0, 0, 0] if mod_reps is None else _rep_rows(ref[0, 0], mod_reps)

    @pl.when(f == 0)
    def _():
        mix = _dot(mg_ref[...].astype(_BF), wo_ref[0])
        xm = x_ref[...] + mod(gt1_ref) * mix
        xmid[...] = xm
        h2[...] = _rms_mod(xm, g2_ref[0], mod(sc2_ref), mod(sh2_ref)).astype(_BF)
        acc[...] = jnp.zeros(acc.shape, _F32)

    a = jnp.maximum(_dot(h2[...], wu_ref[0]), 0.0)
    acc[...] += _dot((a * a).astype(_BF), wd_ref[0])

    @pl.when(f == pl.num_programs(1) - 1)
    def _():
        y = xmid[...] + mod(gt2_ref) * acc[...]
        if final_norm:
            ms = jnp.mean(y * y, axis=-1, keepdims=True)
            y = y * lax.rsqrt(ms + EPS) * gf_ref[...]
        o_ref[...] = y


def _mlp_call(layer, x, merged, mod, row0, g2, w_out, w_up, w_down, g_final, *, tm, tf, seq_len, final_norm):
    m_rows = x.shape[0]
    pieces = (_GT1, _SC2, _SH2, _GT2)
    if seq_len >= tm:
        tiles_per_seq = seq_len // tm
        mod_reps = None
        mod_specs = [_mod_row_spec(layer, p, row0, lambda i, f: i // tiles_per_seq) for p in pieces]
    else:
        mod_reps = seq_len
        mod_specs = [pl.BlockSpec((1, 1, tm // seq_len, D_MODEL), functools.partial(lambda p, i, f: (layer, p, 0, 0), p))
                     for p in pieces]
    kern = functools.partial(_mlp_kernel, mod_reps=mod_reps, final_norm=final_norm)
    return pl.pallas_call(
        kern,
        grid=(m_rows // tm, D_FF // tf),
        in_specs=[
            pl.BlockSpec((tm, D_MODEL), lambda i, f: (i, 0)),
            pl.BlockSpec((tm, D_MODEL), lambda i, f: (i, 0)),
            *mod_specs,
            pl.BlockSpec((1, 1, D_MODEL), lambda i, f: (layer, 0, 0)),
            pl.BlockSpec((1, D_MODEL, D_MODEL), lambda i, f: (layer, 0, 0)),
            pl.BlockSpec((1, D_MODEL, tf), lambda i, f: (layer, 0, f)),
            pl.BlockSpec((1, tf, D_MODEL), lambda i, f: (layer, f, 0)),
            pl.BlockSpec((1, D_MODEL), lambda i, f: (0, 0)),
        ],
        out_specs=pl.BlockSpec((tm, D_MODEL), lambda i, f: (i, 0)),
        out_shape=jax.ShapeDtypeStruct((m_rows, D_MODEL), _F32),
        scratch_shapes=[
            pltpu.VMEM((tm, D_MODEL), _F32),
            pltpu.VMEM((tm, D_MODEL), _BF),
            pltpu.VMEM((tm, D_MODEL), _F32),
        ],
        compiler_params=_params(2),
        name="outproj_mlp",
    )(x, merged, mod, mod, mod, mod, g2, w_out, w_up, w_down, g_final)


def kernel(x_prompt, x_sample, state_conv, state_C, state_n, state_m, c_prompt, c_sample,
           w_ada, b_ada, g_norm1, g_norm2, w_in, b_gate, conv_w, conv_b, hn_g, w_out, w_up, w_down, g_final):
    n_p, seq_p, _ = x_prompt.shape
    n_s, seq_s, _ = x_sample.shape
    rows_s = n_s * seq_s
    tm_p = 1024

    w_t = jnp.swapaxes(w_in, 1, 2).astype(_BF)
    w_tail = w_t[:, _OFF_GA:]
    zrows = jnp.zeros((DEPTH, GATE_F_COL - N_HEADS, D_MODEL), _BF)
    w_gate = jnp.concatenate([w_t[:, _OFF_IG:_OFF_IG + N_HEADS], zrows,
                              w_t[:, _OFF_IG + N_HEADS:_OFF_GA], zrows], axis=1)
    w_out_b = w_out.astype(_BF)
    w_up_b = w_up.astype(_BF)
    w_down_b = w_down.astype(_BF)
    zb = jnp.zeros((DEPTH, 4), _F32)
    bgate = jnp.broadcast_to(
        jnp.concatenate([b_gate[:, :N_HEADS], zb, b_gate[:, N_HEADS:], zb], axis=-1)[:, :, None],
        (DEPTH, BIAS_ROWS, CHUNK))
    zb = jnp.zeros((DEPTH, GATE_F_COL - N_HEADS), _F32)
    bias_row = jnp.concatenate([b_gate[:, :N_HEADS], zb, b_gate[:, N_HEADS:], zb], axis=-1)[:, None, :]
    g1 = g_norm1.reshape(DEPTH, 1, D_MODEL)
    g2 = g_norm2.reshape(DEPTH, 1, D_MODEL)
    hng = hn_g.reshape(DEPTH, 1, D_MODEL)
    cb = conv_b.reshape(DEPTH, 1, D_MODEL)
    gfin = g_final.reshape(1, D_MODEL)

    mod = _ada_call(jnp.concatenate([c_sample, c_prompt], axis=0), w_ada, b_ada)
    mod_rows = mod.reshape(DEPTH, N_MOD, n_s + n_p, 1, D_MODEL)

    xp = x_prompt.reshape(n_p * seq_p, D_MODEL)
    xs = x_sample.transpose(1, 0, 2).reshape(rows_s, D_MODEL)
    conv_s_in = state_conv.transpose(0, 2, 1, 3)
    n_s_in = state_n.reshape(DEPTH, n_s, N_HEADS * DK)
    m_s_in = jnp.pad(state_m, ((0, 0), (0, 0), (0, CHUNK - N_HEADS)))

    zeros_conv = jnp.zeros((n_p, CONV_W - 1, D_MODEL), _F32)
    zeros_c = jnp.zeros((n_p, N_HEADS, DK, DV), _F32)
    zeros_n = jnp.zeros((n_p, N_HEADS, DK), _F32)
    zeros_m = jnp.zeros((n_p, 8, CHUNK), _F32)

    p_conv, p_n, p_m, s_conv, s_n, s_m = [], [], [], [], [], []
    p_c = s_c = None
    for l in range(DEPTH):
        final = l == DEPTH - 1

        mid, gates_t, nconv = _inproj_prompt_call(
            l, xp, mod_rows, n_s, g1, w_t, w_tail, w_gate, bgate, conv_w, cb, zeros_conv, seq_len=seq_p, tm=tm_p)
        merged, p_c, n1, m1 = _mlstm_prompt_call(
            l, mid, gates_t, hng, zeros_c, zeros_n, zeros_m, p_c, seq_len=seq_p, tb=2048)
        xp = _mlp_call(l, xp, merged, mod_rows, n_s, g2, w_out_b, w_up_b, w_down_b, gfin,
                       tm=1024, tf=1024, seq_len=seq_p, final_norm=final)
        tiles_per_seq = seq_p // tm_p
        p_conv.append(nconv[tiles_per_seq - 1::tiles_per_seq]); p_n.append(n1); p_m.append(m1[:, :N_HEADS, 0])

        mid, gates, nconv = _inproj_sample_call(
            l, xs, mod, g1, w_t, w_tail, w_gate, conv_w, cb, conv_s_in, seq_len=seq_s)
        merged3, s_c, n1, m1 = _mlstm_sample_call(
            l, mid.reshape(seq_s, n_s, N_HEADS * HEAD_COLS), gates.reshape(seq_s, n_s, GATE_COLS), bias_row, hng,
            state_C, n_s_in, m_s_in, s_c, bb=16)
        xs = _mlp_call(l, xs, merged3.reshape(rows_s, D_MODEL), mod, 0, g2, w_out_b, w_up_b, w_down_b, gfin,
                       tm=rows_s, tf=1024, seq_len=seq_s, final_norm=final)
        s_conv.append(nconv.transpose(1, 0, 2))
        s_n.append(n1.reshape(n_s, N_HEADS, DK)); s_m.append(m1[:, :N_HEADS])

    return (xp.reshape(n_p, seq_p, D_MODEL), xs.reshape(seq_s, n_s, D_MODEL).transpose(1, 0, 2),
            jnp.stack(p_conv), p_c, jnp.stack(p_n), jnp.stack(p_m),
            jnp.stack(s_conv), s_c, jnp.stack(s_n), jnp.stack(s_m))
```

```python
import functools

import jax
import jax.numpy as jnp
from jax import lax
from jax.experimental import pallas as pl
from jax.experimental.pallas import tpu as pltpu

D_MODEL = 1024
N_HEADS = 4
DK = 128
DV = 256
D_FF = 4096
DEPTH = 4
CONV_W = 3
GATE_CAP = 15.0
EPS = 1e-6
CHUNK = 128
HEAD_COLS = 1024
GATE_COLS = 256
GATE_F_COL = 128
BIAS_ROWS = 16
GATE_ROWS = 24
EXT_COLS = DV + CHUNK
VMEM_LIMIT = 56 * 1024 * 1024

_OFF_XC, _OFF_BG, _OFF_CG = 0, 1024, 2048
_OFF_Q, _OFF_K, _OFF_V, _OFF_O = 3072, 3584, 4096, 5120
_OFF_IG, _OFF_GA, _OFF_GB = 6144, 6152, 7176

_M_YA, _M_Q, _M_K, _M_V, _M_OG = 0, 256, 384, 512, 768

_BF = jnp.bfloat16
_F32 = jnp.float32


def _dot(a, b):
    return jnp.dot(a, b, preferred_element_type=_F32)


def _dot_nt(a, b):
    return lax.dot_general(a, b, (((1,), (1,)), ((), ())), preferred_element_type=_F32)


def _sigmoid(x):
    return 0.5 * jnp.tanh(0.5 * x) + 0.5


def _rms_mod(x, g, sc, sh):
    ms = jnp.mean(x * x, axis=-1, keepdims=True)
    return (x * lax.rsqrt(ms + EPS)) * (g * (1.0 + sc)) + sh


def _rep_rows(v, reps):
    return v if reps == 1 else jnp.concatenate([v] * reps, axis=0)


def _params(n_axes):
    return pltpu.CompilerParams(dimension_semantics=("arbitrary",) * n_axes, vmem_limit_bytes=VMEM_LIMIT)


N_MOD = 6
_SH1, _SC1, _GT1, _SH2, _SC2, _GT2 = range(N_MOD)


def _ada_kernel(c_ref, w_ref, b_ref, o_ref):
    c = c_ref[...]
    a = (c * _sigmoid(c)).astype(_BF)
    o_ref[0, 0] = _dot(a, w_ref[0].astype(_BF)) + b_ref[0]


def _ada_call(c_all, w_ada, b_ada):
    n_rows = c_all.shape[0]
    return pl.pallas_call(
        _ada_kernel,
        grid=(DEPTH, N_MOD),
        in_specs=[
            pl.BlockSpec((n_rows, D_MODEL), lambda l, j: (0, 0)),
            pl.BlockSpec((1, D_MODEL, D_MODEL), lambda l, j: (l, 0, j)),
            pl.BlockSpec((1, 1, D_MODEL), lambda l, j: (l, 0, j)),
        ],
        out_specs=pl.BlockSpec((1, 1, n_rows, D_MODEL), lambda l, j: (l, j, 0, 0)),
        out_shape=jax.ShapeDtypeStruct((DEPTH, N_MOD, n_rows, D_MODEL), _F32),
        compiler_params=_params(2),
        name="ada_mod",
    )(c_all, w_ada, b_ada.reshape(DEPTH, 1, N_MOD * D_MODEL))


def _w_in_specs(layer, idx):
    def spec(width, offset):
        return pl.BlockSpec((1, width, D_MODEL), lambda *g: (layer, offset // width + idx(*g), 0))
    return [spec(256, _OFF_XC), spec(256, _OFF_BG), spec(256, _OFF_CG), spec(128, _OFF_Q), spec(128, _OFF_K),
            spec(256, _OFF_V), spec(256, _OFF_O), spec(256, 0), spec(256, _OFF_GB - _OFF_GA)]


def _scan_lanes(x, combine, identity, seg_len):
    pos = lax.broadcasted_iota(jnp.int32, x.shape, 1) % seg_len
    shift = 1
    while shift < seg_len:
        x = combine(x, jnp.where(pos >= shift, pltpu.roll(x, shift, 1), identity))
        shift *= 2
    return x


def _softcap(a):
    return GATE_CAP * jnp.tanh(a / GATE_CAP)


def _log_sigmoid(x):
    return -(jnp.maximum(-x, 0.0) + jnp.log(1.0 + jnp.exp(-jnp.abs(x))))


def _gate_rows_prompt(gates, bias, gt_ref, first_chunk):
    n_chunks = gates.shape[0] // CHUNK
    gates_t = gates.T

    def stack(r0):
        return jnp.concatenate([gates_t[r0:r0 + 8, c * CHUNK:(c + 1) * CHUNK] for c in range(n_chunks)], axis=0)

    i_pre = _softcap(stack(0) + jnp.concatenate([bias[0:8]] * n_chunks, axis=0))
    f_pre = _softcap(stack(GATE_F_COL) + jnp.concatenate([bias[8:16]] * n_chunks, axis=0))
    b = _scan_lanes(_log_sigmoid(f_pre), jnp.add, 0.0, CHUNK)
    g = i_pre - b
    cm = _scan_lanes(g, jnp.maximum, -3e38, CHUNK)
    for c in range(n_chunks):
        gt_ref[first_chunk + c, 0:8] = b[c * 8:(c + 1) * 8]
        gt_ref[first_chunk + c, 8:16] = g[c * 8:(c + 1) * 8]
        gt_ref[first_chunk + c, 16:24] = cm[c * 8:(c + 1) * 8]


def _inproj_project(h, w_refs):
    wxc, wbg, wcg, wq, wk, wv, wo, wga, wgb = w_refs
    xc = _dot_nt(h, wxc[0])
    cg = _dot_nt(h, wcg[0])
    qk = _dot_nt(h, jnp.concatenate([wq[0], wk[0]], axis=0))
    return dict(u=cg * xc, bg=_dot_nt(h, wbg[0]), ga=_dot_nt(h, wga[0]), qk=qk,
                v=_dot_nt(h, wv[0]), o=_dot_nt(h, wo[0]), gb=_dot_nt(h, wgb[0]))


def _inproj_epilogue(p, conv, mid_ref, out_dtype, r0=0):
    rows = slice(r0, r0 + conv.shape[0])
    mid_ref[rows, _M_YA:_M_YA + 256] = ((1.0 + jnp.tanh(p["ga"])) * (p["bg"] * conv)).astype(out_dtype)
    mid_ref[rows, _M_Q:_M_Q + DK] = p["qk"][:, :DK].astype(out_dtype)
    mid_ref[rows, _M_K:_M_K + DK] = (p["qk"][:, DK:] * (DK ** -0.5)).astype(out_dtype)
    mid_ref[rows, _M_V:_M_V + 256] = p["v"].astype(out_dtype)
    mid_ref[rows, _M_OG:_M_OG + 256] = ((0.25 + 0.25 * jnp.tanh(p["o"])) * (1.0 + jnp.tanh(p["gb"]))).astype(out_dtype)


_RAW_COLS = 8 * 256


def _inproj_prompt_kernel(x_ref, sc_ref, sh_ref, g_ref, wxc, wbg, wcg, wq, wk, wv, wo, wga, wgb, wg_ref,
                          bg_ref, cw_ref, cb_ref, cprev_ref, mid_ref, gt_ref, nconv_ref,
                          h_scr, gate_scr, raw_even, raw_odd, ubuf, carry, *, tiles_per_seq, n_work):
    j = pl.program_id(0)
    tm = x_ref.shape[0]
    cur = jnp.minimum(j, n_work - 1)
    s = cur % N_HEADS
    prev = jnp.maximum(j - 1, 0)
    tile_p = prev // N_HEADS
    s_p = prev % N_HEADS
    chunks_per_step = (tm // CHUNK) // N_HEADS
    raw = (raw_even, raw_odd)

    @pl.when(j == 0)
    def _():
        raw_odd[...] = jnp.zeros(raw_odd.shape, _F32)
        carry[...] = jnp.zeros(carry.shape, _F32)

    @pl.when(jnp.logical_and(s == 0, j < n_work))
    def _():
        h = _rms_mod(x_ref[...], g_ref[0], sc_ref[0, 0, 0], sh_ref[0, 0, 0]).astype(_BF)
        h_scr[...] = h
        gate_scr[...] = _dot_nt(h, wg_ref[0])

    def step(slot):
        rows0 = pl.multiple_of(s * (chunks_per_step * CHUNK), CHUNK)
        _gate_rows_prompt(gate_scr[pl.ds(rows0, chunks_per_step * CHUNK), :], bg_ref[0], gt_ref,
                          s * chunks_per_step)

        h = h_scr[...]
        weights = (wxc[0], wcg[0], wbg[0], wga[0], jnp.concatenate([wq[0], wk[0]], axis=0),
                   wv[0], wo[0], wgb[0])
        seq_start = (tile_p % tiles_per_seq) == 0
        ubuf[6:8, :] = jnp.where(seq_start, cprev_ref[0], carry[s_p])
        cw = cw_ref[0]
        conv_bias = cb_ref[0]
        rows_per_part = tm // len(weights)
        rc = min(CHUNK // 2, rows_per_part)
        for idx, w in enumerate(weights):
            raw[slot][:, idx * 256:(idx + 1) * 256] = _dot_nt(h, w)
            for r0 in range(idx * rows_per_part, (idx + 1) * rows_per_part, rc):
                def piece(k):
                    return raw[1 - slot][r0:r0 + rc, k * 256:(k + 1) * 256]

                u = piece(1) * piece(0)
                ubuf[8 + r0:8 + r0 + rc, :] = u
                conv = (conv_bias + cw[0:1] * ubuf[6 + r0:6 + r0 + rc, :]
                        + cw[1:2] * ubuf[7 + r0:7 + r0 + rc, :] + cw[2:3] * u)
                _inproj_epilogue(dict(bg=piece(2), ga=piece(3), qk=piece(4), v=piece(5), o=piece(6),
                                      gb=piece(7)), conv, mid_ref, _BF, r0)
        last2 = ubuf[6 + tm:8 + tm, :]
        carry[s_p] = last2
        nconv_ref[0] = last2

    for parity in range(2):
        pl.when(j % 2 == parity)(functools.partial(step, parity))


def _mod_row_spec(layer, piece, row0, seq_of):
    return pl.BlockSpec((1, 1, 1, 1, D_MODEL), lambda *g: (layer, piece, row0 + seq_of(*g), 0, 0))


def _inproj_prompt_call(layer, x, mod_rows, row0, g, w_t, w_tail, w_gate, bgate, conv_w, conv_b, conv_prev,
                        *, seq_len, tm):
    m_rows = x.shape[0]
    n_tiles = m_rows // tm
    tiles_per_seq = seq_len // tm
    n_work = n_tiles * N_HEADS

    def cur(j):
        return jnp.minimum(j, n_work - 1)

    def prev(j):
        return jnp.maximum(j - 1, 0)

    kern = functools.partial(_inproj_prompt_kernel, tiles_per_seq=tiles_per_seq, n_work=n_work)
    w_specs = _w_in_specs(layer, lambda j: cur(j) % N_HEADS)
    seq_of_cur = lambda j: cur(j) // N_HEADS // tiles_per_seq
    return pl.pallas_call(
        kern,
        grid=(n_work + 1,),
        in_specs=[
            pl.BlockSpec((tm, D_MODEL), lambda j: (cur(j) // N_HEADS, 0)),
            _mod_row_spec(layer, _SC1, row0, seq_of_cur),
            _mod_row_spec(layer, _SH1, row0, seq_of_cur),
            pl.BlockSpec((1, 1, D_MODEL), lambda j: (layer, 0, 0)),
            *w_specs,
            pl.BlockSpec((1, GATE_COLS, D_MODEL), lambda j: (layer, 0, 0)),
            pl.BlockSpec((1, BIAS_ROWS, CHUNK), lambda j: (layer, 0, 0)),
            pl.BlockSpec((1, CONV_W, 256), lambda j: (layer, 0, prev(j) % N_HEADS)),
            pl.BlockSpec((1, 1, 256), lambda j: (layer, 0, prev(j) % N_HEADS)),
            pl.BlockSpec((1, CONV_W - 1, 256),
                         lambda j: (prev(j) // N_HEADS // tiles_per_seq, 0, prev(j) % N_HEADS)),
        ],
        out_specs=[
            pl.BlockSpec((tm, HEAD_COLS), lambda j: (prev(j) // N_HEADS, prev(j) % N_HEADS)),
            pl.BlockSpec((tm // CHUNK, GATE_ROWS, CHUNK), lambda j: (cur(j) // N_HEADS, 0, 0)),
            pl.BlockSpec((1, CONV_W - 1, 256), lambda j: (prev(j) // N_HEADS, 0, prev(j) % N_HEADS)),
        ],
        out_shape=[
            jax.ShapeDtypeStruct((m_rows, N_HEADS * HEAD_COLS), _BF),
            jax.ShapeDtypeStruct((m_rows // CHUNK, GATE_ROWS, CHUNK), _F32),
            jax.ShapeDtypeStruct((n_tiles, CONV_W - 1, D_MODEL), _F32),
        ],
        scratch_shapes=[
            pltpu.VMEM((tm, D_MODEL), _BF),
            pltpu.VMEM((tm, GATE_COLS), _F32),
            pltpu.VMEM((tm, _RAW_COLS), _F32),
            pltpu.VMEM((tm, _RAW_COLS), _F32),
            pltpu.VMEM((tm + 8, 256), _F32),
            pltpu.VMEM((N_HEADS, CONV_W - 1, 256), _F32),
        ],
        compiler_params=_params(1),
        name="inproj_prompt",
    )(x, mod_rows, mod_rows, g, *([w_t] * 7), *([w_tail] * 2), w_gate, bgate, conv_w, conv_b, conv_prev)


def _inproj_sample_kernel(x_ref, sc_ref, sh_ref, g_ref, wxc, wbg, wcg, wq, wk, wv, wo, wga, wgb, wg_ref,
                          cw_ref, cb_ref, cprev_ref, mid_ref, gt_ref, nconv_ref, h_scr, *, seq_len):
    s = pl.program_id(0)
    n_b = sc_ref.shape[2]

    @pl.when(s == 0)
    def _():
        h = _rms_mod(x_ref[...], g_ref[0], _rep_rows(sc_ref[0, 0], seq_len), _rep_rows(sh_ref[0, 0], seq_len))
        h = h.astype(_BF)
        h_scr[...] = h
        gt_ref[...] = _dot_nt(h, wg_ref[0])

    p = _inproj_project(h_scr[...], (wxc, wbg, wcg, wq, wk, wv, wo, wga, wgb))
    u = p["u"]
    prev0 = cprev_ref[0, 0]
    prev1 = cprev_ref[0, 1]
    p1 = jnp.concatenate([prev1, u[0:(seq_len - 1) * n_b]], axis=0)
    p2 = jnp.concatenate([prev0, prev1, u[0:(seq_len - 2) * n_b]], axis=0)
    cw = cw_ref[0]
    conv = cb_ref[0] + cw[0:1] * p2 + cw[1:2] * p1 + cw[2:3] * u
    nconv_ref[0] = u[(seq_len - 2) * n_b:(seq_len - 1) * n_b]
    nconv_ref[1] = u[(seq_len - 1) * n_b:seq_len * n_b]
    _inproj_epilogue(p, conv, mid_ref, _F32)


def _inproj_sample_call(layer, x, mod, g, w_t, w_tail, w_gate, conv_w, conv_b, conv_prev, *, seq_len):
    tm = x.shape[0]
    n_b = tm // seq_len
    kern = functools.partial(_inproj_sample_kernel, seq_len=seq_len)
    w_specs = _w_in_specs(layer, lambda s: s)
    return pl.pallas_call(
        kern,
        grid=(N_HEADS,),
        in_specs=[
            pl.BlockSpec((tm, D_MODEL), lambda s: (0, 0)),
            pl.BlockSpec((1, 1, n_b, D_MODEL), lambda s: (layer, _SC1, 0, 0)),
            pl.BlockSpec((1, 1, n_b, D_MODEL), lambda s: (layer, _SH1, 0, 0)),
            pl.BlockSpec((1, 1, D_MODEL), lambda s: (layer, 0, 0)),
            *w_specs,
            pl.BlockSpec((1, GATE_COLS, D_MODEL), lambda s: (layer, 0, 0)),
            pl.BlockSpec((1, CONV_W, 256), lambda s: (layer, 0, s)),
            pl.BlockSpec((1, 1, 256), lambda s: (layer, 0, s)),
            pl.BlockSpec((1, CONV_W - 1, n_b, 256), lambda s: (layer, 0, 0, s)),
        ],
        out_specs=[
            pl.BlockSpec((tm, HEAD_COLS), lambda s: (0, s)),
            pl.BlockSpec((tm, GATE_COLS), lambda s: (0, 0)),
            pl.BlockSpec((CONV_W - 1, n_b, 256), lambda s: (0, 0, s)),
        ],
        out_shape=[
            jax.ShapeDtypeStruct((tm, N_HEADS * HEAD_COLS), _F32),
            jax.ShapeDtypeStruct((tm, GATE_COLS), _F32),
            jax.ShapeDtypeStruct((CONV_W - 1, n_b, D_MODEL), _F32),
        ],
        scratch_shapes=[pltpu.VMEM((tm, D_MODEL), _BF)],
        compiler_params=_params(1),
        name="inproj_sample",
    )(x, mod, mod, g, *([w_t] * 7), *([w_tail] * 2), w_gate, conv_w, conv_b, conv_prev)


def _causal_mask(lq):
    row_id = lax.broadcasted_iota(jnp.int32, (lq, CHUNK), 0)
    col_id = lax.broadcasted_iota(jnp.int32, (lq, CHUNK), 1)
    return col_id <= row_id


def _chunk_scores(a, causal):
    dmat = jnp.where(causal, jnp.exp(a["g_row"] - a["m_col"]), 0.0)
    s_bf = (_dot_nt(a["q"], a["k"]) * dmat).astype(_BF)
    q_inter = (a["q"].astype(_F32) * a["inter_col"]).astype(_BF)
    kw_t = (a["k"].T.astype(_F32) * a["ws_row"]).astype(_BF)
    return jnp.concatenate([s_bf, q_inter], axis=1), kw_t


def _chunk_output(a, local, cx_ref, h, hng_h):
    lhs, kw_t = local
    cx = cx_ref[h]
    rhs = jnp.concatenate([a["v_ext"], cx.astype(_BF)], axis=0)
    res = _dot(lhs, rhs)
    cx_ref[h] = a["dec"] * cx + _dot(kw_t, a["v_ext"])
    num = res[:, :DV]
    rden = 1.0 / jnp.maximum(jnp.abs(res[:, DV:]), a["em_col"])
    sq = jnp.sum(num * num, axis=-1, keepdims=True) * (1.0 / DV)
    scale = rden * lax.rsqrt(rden * rden * sq + EPS)
    hn = num * jnp.concatenate([scale, scale], axis=1) * hng_h
    return a["ya"] + a["og"] * hn


def _mlstm_prompt_kernel(mid_ref, gt_ref, hng_ref, c0_ref, n0_ref, m0_ref, cbuf_ref,
                         mg_ref, c_ref, n_ref, m_ref, cx_ref, cols_scr, rows_scr):
    del cbuf_ref
    t = pl.program_id(1)

    @pl.when(t == 0)
    def _():
        m_ref[...] = m0_ref[...]
        for h in range(N_HEADS):
            cx_ref[h, :, 0:DV] = c0_ref[0, h]
            cx_ref[h, :, DV:EXT_COLS] = jnp.broadcast_to(n0_ref[0, h:h + 1, :], (CHUNK, DK)).T

    n_chunks = mid_ref.shape[0] // CHUNK
    rows = n_chunks * 8
    hng = hng_ref[0]
    gates = gt_ref[...]
    b = gates[:, 0:8, :].reshape(rows, CHUNK)
    g = gates[:, 8:16, :].reshape(rows, CHUNK)
    cm = gates[:, 16:24, :].reshape(rows, CHUNK)
    b_last = jnp.broadcast_to(b[:, CHUNK - 1:CHUNK], (rows, CHUNK))
    cm_last = jnp.broadcast_to(cm[:, CHUNK - 1:CHUNK], (rows, CHUNK))

    m_prev = m_ref[0]
    m_prevs = []
    for c in range(n_chunks):
        m_prevs.append(m_prev)
        m_prev = b_last[c * 8:(c + 1) * 8] + jnp.maximum(cm_last[c * 8:(c + 1) * 8], m_prev)
    m_ref[0] = m_prev
    m_prev_all = jnp.concatenate(m_prevs, axis=0)

    m_run = jnp.maximum(cm, m_prev_all)
    inter = jnp.exp(m_prev_all - m_run)
    em = jnp.exp(-(b + m_run))
    m_last = jnp.maximum(cm_last, m_prev_all)
    ws = jnp.exp(g - m_last)
    decay = jnp.exp(m_prev_all - m_last)

    pad = jnp.zeros((CHUNK - 24, CHUNK), _F32)
    for c in range(n_chunks):
        sl = slice(c * 8, (c + 1) * 8)
        cols_scr[c] = jnp.concatenate([m_run[sl], inter[sl], em[sl], pad], axis=0).T
        rows_scr[c, 0:8] = g[sl]
        rows_scr[c, 8:16] = decay[sl]
        rows_scr[c, 16:24] = ws[sl]

    causal = _causal_mask(CHUNK)
    ones = jnp.ones((CHUNK, CHUNK), _BF)

    def chunk_body(c, carry):
        cols = cols_scr[c]
        vec = rows_scr[c]
        rows_c = pl.ds(pl.multiple_of(c * CHUNK, CHUNK), CHUNK)
        heads = []
        for h in range(N_HEADS):
            mb = h * HEAD_COLS
            v = mid_ref[rows_c, mb + _M_V:mb + _M_V + 256]
            heads.append(dict(
                q=mid_ref[rows_c, mb + _M_Q:mb + _M_Q + 128],
                k=mid_ref[rows_c, mb + _M_K:mb + _M_K + 128],
                v_ext=jnp.concatenate([v, ones], axis=1),
                ya=mid_ref[rows_c, mb + _M_YA:mb + _M_YA + 256].astype(_F32),
                og=mid_ref[rows_c, mb + _M_OG:mb + _M_OG + 256].astype(_F32),
                g_row=vec[h:h + 1, :],
                m_col=cols[:, h:h + 1],
                inter_col=cols[:, 8 + h:9 + h],
                em_col=cols[:, 16 + h:17 + h],
                ws_row=vec[16 + h:17 + h, :],
                dec=vec[8 + h:9 + h, 0:1]))
        outs = []
        ahead = 2
        scores = [_chunk_scores(heads[h], causal) for h in range(ahead)]
        for h, a in enumerate(heads):
            if h + ahead < N_HEADS:
                scores.append(_chunk_scores(heads[h + ahead], causal))
            outs.append(_chunk_output(a, scores[h], cx_ref, h, hng[:, h * DV:(h + 1) * DV]))
        mg_ref[rows_c, :] = jnp.concatenate(outs, axis=1).astype(_BF)
        return carry

    lax.fori_loop(0, n_chunks, chunk_body, 0)

    @pl.when(t == pl.num_programs(1) - 1)
    def _():
        for h in range(N_HEADS):
            c_ref[0, 0, h] = cx_ref[h, :, 0:DV]
            n_ref[0, h:h + 1, :] = cx_ref[h, :, DV:EXT_COLS].T[0:1, :]


def _mlstm_prompt_call(layer, mid, gates_t, hng, c0, n0, m0, c_buf, *, seq_len, tb):
    m_rows = mid.shape[0]
    n_seq = m_rows // seq_len
    steps = seq_len // tb
    aliases = {} if c_buf is None else {6: 1}
    if c_buf is None:
        c_buf = jnp.zeros((1,), _F32)
    return pl.pallas_call(
        _mlstm_prompt_kernel,
        grid=(n_seq, steps),
        in_specs=[
            pl.BlockSpec((tb, N_HEADS * HEAD_COLS), lambda b, t: (b * steps + t, 0)),
            pl.BlockSpec((tb // CHUNK, GATE_ROWS, CHUNK), lambda b, t: (b * steps + t, 0, 0)),
            pl.BlockSpec((1, 1, D_MODEL), lambda b, t: (layer, 0, 0)),
            pl.BlockSpec((1, N_HEADS, DK, DV), lambda b, t: (b, 0, 0, 0)),
            pl.BlockSpec((1, N_HEADS, DK), lambda b, t: (b, 0, 0)),
            pl.BlockSpec((1, 8, CHUNK), lambda b, t: (b, 0, 0)),
            pl.BlockSpec(memory_space=pl.ANY),
        ],
        out_specs=[
            pl.BlockSpec((tb, D_MODEL), lambda b, t: (b * steps + t, 0)),
            pl.BlockSpec((1, 1, N_HEADS, DK, DV), lambda b, t: (layer, b, 0, 0, 0)),
            pl.BlockSpec((1, N_HEADS, DK), lambda b, t: (b, 0, 0)),
            pl.BlockSpec((1, 8, CHUNK), lambda b, t: (b, 0, 0)),
        ],
        out_shape=[
            jax.ShapeDtypeStruct((m_rows, D_MODEL), _BF),
            jax.ShapeDtypeStruct((DEPTH, n_seq, N_HEADS, DK, DV), _F32),
            jax.ShapeDtypeStruct((n_seq, N_HEADS, DK), _F32),
            jax.ShapeDtypeStruct((n_seq, 8, CHUNK), _F32),
        ],
        scratch_shapes=[
            pltpu.VMEM((N_HEADS, DK, EXT_COLS), _F32),
            pltpu.VMEM((tb // CHUNK, CHUNK, CHUNK), _F32),
            pltpu.VMEM((tb // CHUNK, 24, CHUNK), _F32),
        ],
        input_output_aliases=aliases,
        compiler_params=_params(2),
        name="mlstm_prompt",
    )(mid, gates_t, hng, c0, n0, m0, c_buf)


def _mlstm_sample_kernel(mid_ref, gt_ref, bias_ref, hng_ref, c0_ref, n0_ref, m0_ref, cbuf_ref,
                         mg_ref, c_ref, n_ref, m_ref,
                         qs_scr, kk_scr, wv_scr, rs_scr, dec_scr, lhs_q, lhs_k, lhs_wv):
    del cbuf_ref
    seq_len, bb, _ = mid_ref.shape
    hng = hng_ref[0]
    bias = bias_ref[0]
    m_prev = m0_ref[0]

    b_t, g_t, cm_t = [], [], []
    for t in range(seq_len):
        pre = _softcap(gt_ref[t] + bias)
        logf = _log_sigmoid(pre[:, GATE_F_COL:GATE_F_COL + CHUNK])
        b_t.append(logf if t == 0 else b_t[-1] + logf)
        g_t.append(pre[:, 0:CHUNK] - b_t[-1])
        cm_t.append(g_t[-1] if t == 0 else jnp.maximum(cm_t[-1], g_t[-1]))
    m_run = [jnp.maximum(cm, m_prev) for cm in cm_t]
    inter = [jnp.exp(m_prev - mr) for mr in m_run]
    em = [jnp.exp(-(b + mr)) for b, mr in zip(b_t, m_run)]
    m_last = m_run[-1]
    ws = [jnp.exp(g - m_last) for g in g_t]
    decay = jnp.exp(m_prev - m_last)
    m_ref[...] = b_t[-1] + m_last
    dec_scr[...] = decay

    lhs_q[...] = jnp.zeros(lhs_q.shape, _F32)
    lhs_k[...] = jnp.zeros(lhs_k.shape, _F32)
    lhs_wv[...] = jnp.zeros(lhs_wv.shape, _F32)
    n_all = n0_ref[0]
    den_part = {}
    pairs = [(t, s) for t in range(seq_len) for s in range(t + 1)]
    for h in range(N_HEADS):
        def col(x):
            return x[:, h:h + 1]
        mb = h * HEAD_COLS
        q = [mid_ref[t, :, mb + _M_Q:mb + _M_Q + DK] for t in range(seq_len)]
        k = [mid_ref[t, :, mb + _M_K:mb + _M_K + DK] for t in range(seq_len)]
        v = [mid_ref[t, :, mb + _M_V:mb + _M_V + DV] for t in range(seq_len)]
        n_h = n_all[:, h * DK:(h + 1) * DK]
        qs = [q[t] * col(inter[t]) for t in range(seq_len)]
        n_new = col(decay) * n_h
        for t in range(seq_len):
            qs_scr[t, :, h * DK:(h + 1) * DK] = qs[t]
            kk_scr[t, :, h * DK:(h + 1) * DK] = k[t]
            wv_scr[t, :, h * DV:(h + 1) * DV] = col(ws[t]) * v[t]
            n_new = n_new + col(ws[t]) * k[t]
        n_ref[:, h * DK:(h + 1) * DK] = n_new
        stacked = jnp.concatenate([q[t] * k[s] for t, s in pairs] + [qs[t] * n_h for t in range(seq_len)], axis=0)
        dots = jnp.sum(stacked, axis=-1, keepdims=True)
        decays = jnp.exp(jnp.concatenate([col(g_t[s]) - col(m_run[t]) for t, s in pairs], axis=0))
        weights = dots[:len(pairs) * bb] * decays
        for t in range(seq_len):
            den = dots[(len(pairs) + t) * bb:(len(pairs) + t + 1) * bb]
            num = None
            for idx, (t2, s) in enumerate(pairs):
                if t2 == t:
                    w = weights[idx * bb:(idx + 1) * bb]
                    num = w * v[s] if num is None else num + w * v[s]
                    den = den + w
            mg_ref[t, :, h * DV:(h + 1) * DV] = num
            den_part[h, t] = den

    n_par = lhs_q.shape[0] // 8

    def body(ii, carry):
        for u in range(n_par):
            i = ii * n_par + u
            rows = slice(8 * u, 8 * u + 8)
            for t in range(seq_len):
                lhs_q[8 * u + t:8 * u + t + 1, :] = qs_scr[t, pl.ds(i, 1), :]
                lhs_k[8 * u + t:8 * u + t + 1, :] = kk_scr[t, pl.ds(i, 1), :]
                lhs_wv[8 * u + t:8 * u + t + 1, :] = wv_scr[t, pl.ds(i, 1), :]
            for h in range(N_HEADS):
                c_old = c0_ref[0, i, h]
                r = _dot(lhs_q[rows, h * DK:(h + 1) * DK].astype(_BF), c_old.astype(_BF))
                for t in range(seq_len):
                    rs_scr[t, pl.ds(i, 1), h * DV:(h + 1) * DV] = r[t:t + 1, :]
                d_c = lax.dot_general(lhs_k[rows, h * DK:(h + 1) * DK].astype(_BF),
                                      lhs_wv[rows, h * DV:(h + 1) * DV].astype(_BF),
                                      (((0,), (0,)), ((), ())), preferred_element_type=_F32)
                c_ref[0, i, h] = dec_scr[pl.ds(i, 1), h:h + 1] * c_old + d_c
        return carry

    lax.fori_loop(0, bb // n_par, body, 0)

    blocks = [(h, t) for h in range(N_HEADS) for t in range(seq_len)]
    dens = jnp.concatenate([jnp.maximum(jnp.abs(den_part[h, t]), em[t][:, h:h + 1]) for h, t in blocks], axis=0)
    hraw = jnp.concatenate([mg_ref[t, :, h * DV:(h + 1) * DV] + rs_scr[t, :, h * DV:(h + 1) * DV]
                            for h, t in blocks], axis=0) * (1.0 / dens)
    ms = jnp.mean(hraw * hraw, axis=-1, keepdims=True)
    hn_all = hraw * lax.rsqrt(ms + EPS)
    for idx, (h, t) in enumerate(blocks):
        mb = h * HEAD_COLS
        hn = hn_all[idx * bb:(idx + 1) * bb] * hng[:, h * DV:(h + 1) * DV]
        ya = mid_ref[t, :, mb + _M_YA:mb + _M_YA + DV]
        og = mid_ref[t, :, mb + _M_OG:mb + _M_OG + DV]
        mg_ref[t, :, h * DV:(h + 1) * DV] = ya + og * hn


def _mlstm_sample_call(layer, mid3, gates3, bias, hng, c0, n0, m0, c_buf, *, bb):
    seq_len, n_seq, _ = mid3.shape
    aliases = {} if c_buf is None else {7: 1}
    if c_buf is None:
        c_buf = jnp.zeros((1,), _F32)
    return pl.pallas_call(
        _mlstm_sample_kernel,
        grid=(n_seq // bb,),
        in_specs=[
            pl.BlockSpec((seq_len, bb, N_HEADS * HEAD_COLS), lambda i: (0, i, 0)),
            pl.BlockSpec((seq_len, bb, GATE_COLS), lambda i: (0, i, 0)),
            pl.BlockSpec((1, 1, GATE_COLS), lambda i: (layer, 0, 0)),
            pl.BlockSpec((1, 1, D_MODEL), lambda i: (layer, 0, 0)),
            pl.BlockSpec((1, bb, N_HEADS, DK, DV), lambda i: (layer, i, 0, 0, 0)),
            pl.BlockSpec((1, bb, N_HEADS * DK), lambda i: (layer, i, 0)),
            pl.BlockSpec((1, bb, CHUNK), lambda i: (layer, i, 0)),
            pl.BlockSpec(memory_space=pl.ANY),
        ],
        out_specs=[
            pl.BlockSpec((seq_len, bb, D_MODEL), lambda i: (0, i, 0)),
            pl.BlockSpec((1, bb, N_HEADS, DK, DV), lambda i: (layer, i, 0, 0, 0)),
            pl.BlockSpec((bb, N_HEADS * DK), lambda i: (i, 0)),
            pl.BlockSpec((bb, CHUNK), lambda i: (i, 0)),
        ],
        out_shape=[
            jax.ShapeDtypeStruct((seq_len, n_seq, D_MODEL), _F32),
            jax.ShapeDtypeStruct((DEPTH, n_seq, N_HEADS, DK, DV), _F32),
            jax.ShapeDtypeStruct((n_seq, N_HEADS * DK), _F32),
            jax.ShapeDtypeStruct((n_seq, CHUNK), _F32),
        ],
        scratch_shapes=[
            pltpu.VMEM((seq_len, bb, N_HEADS * DK), _F32),
            pltpu.VMEM((seq_len, bb, N_HEADS * DK), _F32),
            pltpu.VMEM((seq_len, bb, N_HEADS * DV), _F32),
            pltpu.VMEM((seq_len, bb, N_HEADS * DV), _F32),
            pltpu.VMEM((bb, CHUNK), _F32),
            pltpu.VMEM((32, N_HEADS * DK), _F32),
            pltpu.VMEM((32, N_HEADS * DK), _F32),
            pltpu.VMEM((32, N_HEADS * DV), _F32),
        ],
        input_output_aliases=aliases,
        compiler_params=_params(1),
        name="mlstm_sample",
    )(mid3, gates3, bias, hng, c0, n0, m0, c_buf)


def _mlp_kernel(x_ref, mg_ref, gt1_ref, sc2_ref, sh2_ref, gt2_ref, g2_ref, wo_ref, wu_ref, wd_ref, gf_ref,
                o_ref, xmid, h2, acc, *, mod_reps, final_norm):
    f = pl.program_id(1)

    def mod(ref):
        return ref[0, 0, 0] if mod_reps is None else _rep_rows(ref[0, 0], mod_reps)

    @pl.when(f == 0)
    def _():
        mix = _dot(mg_ref[...].astype(_BF), wo_ref[0])
        xm = x_ref[...] + mod(gt1_ref) * mix
        xmid[...] = xm
        h2[...] = _rms_mod(xm, g2_ref[0], mod(sc2_ref), mod(sh2_ref)).astype(_BF)
        acc[...] = jnp.zeros(acc.shape, _F32)

    a = jnp.maximum(_dot(h2[...], wu_ref[0]), 0.0)
    acc[...] += _dot((a * a).astype(_BF), wd_ref[0])

    @pl.when(f == pl.num_programs(1) - 1)
    def _():
        y = xmid[...] + mod(gt2_ref) * acc[...]
        if final_norm:
            ms = jnp.mean(y * y, axis=-1, keepdims=True)
            y = y * lax.rsqrt(ms + EPS) * gf_ref[...]
        o_ref[...] = y


def _mlp_call(layer, x, merged, mod, row0, g2, w_out, w_up, w_down, g_final, *, tm, tf, seq_len, final_norm):
    m_rows = x.shape[0]
    pieces = (_GT1, _SC2, _SH2, _GT2)
    if seq_len >= tm:
        tiles_per_seq = seq_len // tm
        mod_reps = None
        mod_specs = [_mod_row_spec(layer, p, row0, lambda i, f: i // tiles_per_seq) for p in pieces]
    else:
        mod_reps = seq_len
        mod_specs = [pl.BlockSpec((1, 1, tm // seq_len, D_MODEL), functools.partial(lambda p, i, f: (layer, p, 0, 0), p))
                     for p in pieces]
    kern = functools.partial(_mlp_kernel, mod_reps=mod_reps, final_norm=final_norm)
    return pl.pallas_call(
        kern,
        grid=(m_rows // tm, D_FF // tf),
        in_specs=[
            pl.BlockSpec((tm, D_MODEL), lambda i, f: (i, 0)),
            pl.BlockSpec((tm, D_MODEL), lambda i, f: (i, 0)),
            *mod_specs,
            pl.BlockSpec((1, 1, D_MODEL), lambda i, f: (layer, 0, 0)),
            pl.BlockSpec((1, D_MODEL, D_MODEL), lambda i, f: (layer, 0, 0)),
            pl.BlockSpec((1, D_MODEL, tf), lambda i, f: (layer, 0, f)),
            pl.BlockSpec((1, tf, D_MODEL), lambda i, f: (layer, f, 0)),
            pl.BlockSpec((1, D_MODEL), lambda i, f: (0, 0)),
        ],
        out_specs=pl.BlockSpec((tm, D_MODEL), lambda i, f: (i, 0)),
        out_shape=jax.ShapeDtypeStruct((m_rows, D_MODEL), _F32),
        scratch_shapes=[
            pltpu.VMEM((tm, D_MODEL), _F32),
            pltpu.VMEM((tm, D_MODEL), _BF),
            pltpu.VMEM((tm, D_MODEL), _F32),
        ],
        compiler_params=_params(2),
        name="outproj_mlp",
    )(x, merged, mod, mod, mod, mod, g2, w_out, w_up, w_down, g_final)


def kernel(x_prompt, x_sample, state_conv, state_C, state_n, state_m, c_prompt, c_sample,
           w_ada, b_ada, g_norm1, g_norm2, w_in, b_gate, conv_w, conv_b, hn_g, w_out, w_up, w_down, g_final):
    n_p, seq_p, _ = x_prompt.shape
    n_s, seq_s, _ = x_sample.shape
    rows_s = n_s * seq_s
    tm_p = 1024

    half_rows = jnp.zeros((w_in.shape[-1],), _F32)
    for off in (_OFF_BG, _OFF_O, _OFF_GA, _OFF_GB):
        half_rows = half_rows.at[off:off + D_MODEL].set(1.0)
    row_scale = 1.0 - 0.5 * half_rows
    w_t = (jnp.swapaxes(w_in, 1, 2) * row_scale[None, :, None]).astype(_BF)
    w_tail = w_t[:, _OFF_GA:]
    zrows = jnp.zeros((DEPTH, GATE_F_COL - N_HEADS, D_MODEL), _BF)
    w_gate = jnp.concatenate([w_t[:, _OFF_IG:_OFF_IG + N_HEADS], zrows,
                              w_t[:, _OFF_IG + N_HEADS:_OFF_GA], zrows], axis=1)
    w_out_b = w_out.astype(_BF)
    w_up_b = w_up.astype(_BF)
    w_down_b = w_down.astype(_BF)
    zb = jnp.zeros((DEPTH, 4), _F32)
    bgate = jnp.broadcast_to(
        jnp.concatenate([b_gate[:, :N_HEADS], zb, b_gate[:, N_HEADS:], zb], axis=-1)[:, :, None],
        (DEPTH, BIAS_ROWS, CHUNK))
    zb = jnp.zeros((DEPTH, GATE_F_COL - N_HEADS), _F32)
    bias_row = jnp.concatenate([b_gate[:, :N_HEADS], zb, b_gate[:, N_HEADS:], zb], axis=-1)[:, None, :]
    g1 = g_norm1.reshape(DEPTH, 1, D_MODEL)
    g2 = g_norm2.reshape(DEPTH, 1, D_MODEL)
    hng = hn_g.reshape(DEPTH, 1, D_MODEL)
    cb = conv_b.reshape(DEPTH, 1, D_MODEL)
    gfin = g_final.reshape(1, D_MODEL)

    mod = _ada_call(jnp.concatenate([c_sample, c_prompt], axis=0), w_ada, b_ada)
    mod_rows = mod.reshape(DEPTH, N_MOD, n_s + n_p, 1, D_MODEL)

    xp = x_prompt.reshape(n_p * seq_p, D_MODEL)
    xs = x_sample.transpose(1, 0, 2).reshape(rows_s, D_MODEL)
    conv_s_in = state_conv.transpose(0, 2, 1, 3)
    n_s_in = state_n.reshape(DEPTH, n_s, N_HEADS * DK)
    m_s_in = jnp.pad(state_m, ((0, 0), (0, 0), (0, CHUNK - N_HEADS)))

    zeros_conv = jnp.zeros((n_p, CONV_W - 1, D_MODEL), _F32)
    zeros_c = jnp.zeros((n_p, N_HEADS, DK, DV), _F32)
    zeros_n = jnp.zeros((n_p, N_HEADS, DK), _F32)
    zeros_m = jnp.zeros((n_p, 8, CHUNK), _F32)

    p_conv, p_n, p_m, s_conv, s_n, s_m = [], [], [], [], [], []
    p_c = s_c = None
    for l in range(DEPTH):
        final = l == DEPTH - 1

        mid, gates_t, nconv = _inproj_prompt_call(
            l, xp, mod_rows, n_s, g1, w_t, w_tail, w_gate, bgate, conv_w, cb, zeros_conv, seq_len=seq_p, tm=tm_p)
        merged, p_c, n1, m1 = _mlstm_prompt_call(
            l, mid, gates_t, hng, zeros_c, zeros_n, zeros_m, p_c, seq_len=seq_p, tb=2048)
        xp = _mlp_call(l, xp, merged, mod_rows, n_s, g2, w_out_b, w_up_b, w_down_b, gfin,
                       tm=512, tf=2048, seq_len=seq_p, final_norm=final)
        tiles_per_seq = seq_p // tm_p
        p_conv.append(nconv[tiles_per_seq - 1::tiles_per_seq]); p_n.append(n1); p_m.append(m1[:, :N_HEADS, 0])

        mid, gates, nconv = _inproj_sample_call(
            l, xs, mod, g1, w_t, w_tail, w_gate, conv_w, cb, conv_s_in, seq_len=seq_s)
        merged3, s_c, n1, m1 = _mlstm_sample_call(
            l, mid.reshape(seq_s, n_s, N_HEADS * HEAD_COLS), gates.reshape(seq_s, n_s, GATE_COLS), bias_row, hng,
            state_C, n_s_in, m_s_in, s_c, bb=16)
        xs = _mlp_call(l, xs, merged3.reshape(rows_s, D_MODEL), mod, 0, g2, w_out_b, w_up_b, w_down_b, gfin,
                       tm=rows_s, tf=1024, seq_len=seq_s, final_norm=final)
        s_conv.append(nconv.transpose(1, 0, 2))
        s_n.append(n1.reshape(n_s, N_HEADS, DK)); s_m.append(m1[:, :N_HEADS])

    return (xp.reshape(n_p, seq_p, D_MODEL), xs.reshape(seq_s, n_s, D_MODEL).transpose(1, 0, 2),
            jnp.stack(p_conv), p_c, jnp.stack(p_n), jnp.stack(p_m),
            jnp.stack(s_conv), s_c, jnp.stack(s_n), jnp.stack(s_m))
```
